```python
import jax, jax.numpy as jnp
from jax import lax
import numpy as np

D_MODEL = 1024
BATCH = 16
SEQ = 4096
DEPTH = 4
DEC_BATCH = 8
DEC_SEQ = 8192
PAST_LEN = 128

HEAD_DIM = 64
GRID_W = 64
A_HEADS = 8
A_KV_HEADS = 2
A_ROPE_THETA = 10000.0
A_Q_BLOCK = 128
B_HEADS = 8
B_PATTERNS = ((128, 1), (512, 4), (2048, 16))
B_ROPE_THETA = 500000.0
B_ROPE_DIMS = HEAD_DIM // 4
B_Q_BLOCK = 64
C_HEADS = 12
C_WIN_H = 8
C_WIN_W = 16
D_GROUPS = 4
D_GROUP_DIM = 64

A_WIDTH = A_HEADS * HEAD_DIM
A_KV_WIDTH = A_KV_HEADS * HEAD_DIM
B_WIDTH = B_HEADS * HEAD_DIM
C_WIDTH = C_HEADS * HEAD_DIM
D_WIDTH = D_GROUPS * D_GROUP_DIM
AB_SPLITS = (A_WIDTH, A_KV_WIDTH, A_KV_WIDTH, A_WIDTH, B_WIDTH, B_WIDTH, B_WIDTH, B_WIDTH)
AB_IN = sum(AB_SPLITS)
AB_OUT = A_WIDTH + B_WIDTH
CD_SPLITS = (C_WIDTH, C_WIDTH, C_WIDTH, C_WIDTH, D_WIDTH, D_WIDTH)
CD_IN = sum(CD_SPLITS)
CD_OUT = C_WIDTH + D_WIDTH
N_EVEN = (DEPTH + 1) // 2
N_ODD = DEPTH // 2
EPS = 1e-6
NEG_INF = -1e30

kernel_name = 'hybrid_bidir_encoder_two_groups'


def rmsnorm(x, g):
    xf = x.astype(jnp.float32)
    y = xf * lax.rsqrt(jnp.mean(xf * xf, axis=-1, keepdims=True) + EPS)
    return (y * g.astype(jnp.float32)).astype(x.dtype)


def split_cols(p, sizes):
    idx = [sum(sizes[:i + 1]) for i in range(len(sizes) - 1)]
    return jnp.split(p, idx, axis=-1)


def rope_tables(pos, dim, theta):
    inv = theta ** (-jnp.arange(0, dim, 2, dtype=jnp.float32) / dim)
    ang = pos[:, None] * inv[None, :]
    return jnp.cos(ang), jnp.sin(ang)


def apply_rope(x, cos, sin):
    half = x.shape[-1] // 2
    x1 = x[..., :half].astype(jnp.float32)
    x2 = x[..., half:].astype(jnp.float32)
    c = cos[None, :, None, :]
    s = sin[None, :, None, :]
    return jnp.concatenate([x1 * c - x2 * s, x1 * s + x2 * c], axis=-1).astype(x.dtype)


def axial_rope(x, cos_r, sin_r, cos_c, sin_c):
    half = x.shape[-1] // 2
    return jnp.concatenate([apply_rope(x[..., :half], cos_r, sin_r),
                            apply_rope(x[..., half:], cos_c, sin_c)], axis=-1)


def partial_rope(x, cos, sin):
    return jnp.concatenate([apply_rope(x[..., :B_ROPE_DIMS], cos, sin), x[..., B_ROPE_DIMS:]], axis=-1)


def position_tables(s):
    t = jnp.arange(s)
    row = (t // GRID_W).astype(jnp.float32)
    col = (t % GRID_W).astype(jnp.float32)
    cos_r, sin_r = rope_tables(row, HEAD_DIM // 2, A_ROPE_THETA)
    cos_c, sin_c = rope_tables(col, HEAD_DIM // 2, A_ROPE_THETA)
    cos_b, sin_b = rope_tables(t.astype(jnp.float32), B_ROPE_DIMS, B_ROPE_THETA)
    return (cos_r, sin_r, cos_c, sin_c, cos_b, sin_b)


def adaln(c, ada_w, ada_b):
    mod = jnp.einsum('bd,de->be', jax.nn.silu(c), ada_w) + ada_b
    shift, scale, gate = jnp.split(mod, 3, axis=-1)
    return shift, scale, gate


def dense_gqa_attention(q, k, v):
    b, s, hq, hd = q.shape
    hkv = k.shape[2]
    g = hq // hkv
    nqb = s // A_Q_BLOCK
    qb = q.reshape(b, nqb, A_Q_BLOCK, hkv, g, hd).transpose(1, 0, 2, 3, 4, 5)
    scale = hd ** -0.5

    def block(qi):
        sc = jnp.einsum('bqkgd,bskd->bkgqs', qi, k).astype(jnp.float32) * scale
        p = jax.nn.softmax(sc, axis=-1).astype(v.dtype)
        return jnp.einsum('bkgqs,bskd->bqkgd', p, v)

    o = lax.map(block, qb)
    return o.transpose(1, 0, 2, 3, 4, 5).reshape(b, s, hq, hd)


def sliding_window_attention(q, k, v, radius):
    n, l, h, hd = q.shape
    qbs = B_Q_BLOCK
    nb = -(-l // qbs)
    lp = nb * qbs
    qp = jnp.pad(q, ((0, 0), (0, lp - l), (0, 0), (0, 0)))
    pad_k = ((0, 0), (radius, lp - l + radius), (0, 0), (0, 0))
    kp = jnp.pad(k, pad_k)
    vp = jnp.pad(v, pad_k)
    span = qbs + 2 * radius
    kidx = (jnp.arange(nb) * qbs)[:, None] + jnp.arange(span)[None, :]
    kb = kp[:, kidx]
    vb = vp[:, kidx]
    qb = qp.reshape(n, nb, qbs, h, hd)
    qpos = (jnp.arange(nb) * qbs)[:, None] + jnp.arange(qbs)[None, :]
    kpos = kidx - radius
    valid = ((jnp.abs(qpos[:, :, None] - kpos[:, None, :]) <= radius)
             & (kpos[:, None, :] >= 0) & (kpos[:, None, :] < l))
    sc = jnp.einsum('nbqhd,nbkhd->nbhqk', qb, kb).astype(jnp.float32) * (hd ** -0.5)
    sc = jnp.where(valid[None, :, None], sc, NEG_INF)
    m = jnp.max(sc, axis=-1, keepdims=True)
    e = jnp.exp(sc - m)
    den = jnp.sum(e, axis=-1, keepdims=True)
    o = jnp.einsum('nbhqk,nbkhd->nbqhd', (e / den).astype(v.dtype), vb)
    lse = (m + jnp.log(den))[..., 0]
    o = o.reshape(n, lp, h, hd)[:, :l]
    lse = lse.transpose(0, 1, 3, 2).reshape(n, lp, h)[:, :l]
    return o, lse


def to_residues(t, dil):
    b, s = t.shape[:2]
    rest = t.shape[2:]
    t = t.reshape((b, s // dil, dil) + rest)
    t = jnp.moveaxis(t, 2, 1)
    return t.reshape((b * dil, s // dil) + rest)


def from_residues(t, b, dil):
    rest = t.shape[2:]
    l = t.shape[1]
    t = t.reshape((b, dil, l) + rest)
    t = jnp.moveaxis(t, 1, 2)
    return t.reshape((b, l * dil) + rest)


def dilated_mixture_attention(q, k, v):
    b = q.shape[0]
    outs, lses = [], []
    for window, dil in B_PATTERNS:
        radius = window // (2 * dil)
        o, lse = sliding_window_attention(to_residues(q, dil), to_residues(k, dil), to_residues(v, dil), radius)
        outs.append(from_residues(o, b, dil))
        lses.append(from_residues(lse, b, dil))
    w = jax.nn.softmax(jnp.stack(lses, axis=0), axis=0)
    return jnp.einsum('pbsh,pbshd->bshd', w.astype(q.dtype), jnp.stack(outs, axis=0))


def neighbourhood_attention(q, k, v, rpb):
    b, s, h, hd = q.shape
    rows = s // GRID_W
    kh = min(C_WIN_H, rows)
    kw = C_WIN_W
    kg = k.reshape(b, rows, GRID_W, h, hd)
    vg = v.reshape(b, rows, GRID_W, h, hd)
    qg = q.reshape(b, rows, GRID_W, h, hd).transpose(1, 0, 2, 3, 4)
    cols = jnp.arange(GRID_W)
    col_start = jnp.clip(cols - kw // 2, 0, GRID_W - kw)
    col_mask = (cols[None, :] >= col_start[:, None]) & (cols[None, :] < col_start[:, None] + kw)
    col_idx = jnp.clip(cols[None, :] - cols[:, None] + kw - 1, 0, 2 * kw - 2)
    scale = hd ** -0.5

    def row_block(args):
        r, qr = args
        r0 = jnp.clip(r - kh // 2, 0, rows - kh)
        kr = lax.dynamic_slice_in_dim(kg, r0, kh, axis=1)
        vr = lax.dynamic_slice_in_dim(vg, r0, kh, axis=1)
        row_idx = r0 + jnp.arange(kh) - r + C_WIN_H - 1
        bias = rpb[:, row_idx[None, :, None], col_idx[:, None, :]]
        sc = jnp.einsum('bchd,bjkhd->bhcjk', qr, kr).astype(jnp.float32) * scale + bias[None].astype(jnp.float32)
        sc = jnp.where(col_mask[None, None, :, None, :], sc, NEG_INF)
        p = jax.nn.softmax(sc.reshape(b, h, GRID_W, kh * GRID_W), axis=-1).reshape(sc.shape)
        return jnp.einsum('bhcjk,bjkhd->bchd', p.astype(v.dtype), vr)

    o = lax.map(row_block, (jnp.arange(rows), qg))
    return o.transpose(1, 0, 2, 3, 4).reshape(b, s, h, hd)


def fourier_mixing(u, lin):
    b, s, _ = u.shape
    ug = u.reshape(b, s, D_GROUPS, D_GROUP_DIM).astype(jnp.float32)
    f = jnp.fft.fft2(ug, axes=(1, 3), norm='ortho').real
    return jnp.einsum('bsc,ce->bse', f.reshape(b, s, D_WIDTH).astype(u.dtype), lin)


def even_layer(x, c, pre_g, post_g, ada_w, ada_b, w_in, w_out, qn, kn, tabs):
    b, s, _ = x.shape
    cos_r, sin_r, cos_c, sin_c, cos_b, sin_b = tabs
    shift, scale, gate = adaln(c, ada_w, ada_b)
    h = rmsnorm(x, pre_g) * (1 + scale[:, None, :]) + shift[:, None, :]
    p = jnp.einsum('bsd,de->bse', h, w_in)
    qa, ka, va, ga, qb, kb, vb, gb = split_cols(p, AB_SPLITS)

    def heads(t, n):
        return t.reshape(b, s, n, HEAD_DIM)

    qa = axial_rope(rmsnorm(heads(qa, A_HEADS), qn), cos_r, sin_r, cos_c, sin_c)
    ka = axial_rope(rmsnorm(heads(ka, A_KV_HEADS), kn), cos_r, sin_r, cos_c, sin_c)
    oa = dense_gqa_attention(qa, ka, heads(va, A_KV_HEADS)).reshape(b, s, A_WIDTH)
    qb = partial_rope(heads(qb, B_HEADS), cos_b, sin_b)
    kb = partial_rope(heads(kb, B_HEADS), cos_b, sin_b)
    ob = dilated_mixture_attention(qb, kb, heads(vb, B_HEADS)).reshape(b, s, B_WIDTH)

    mixed = jnp.concatenate([oa * jax.nn.silu(ga), ob * jax.nn.silu(gb)], axis=-1)
    y = rmsnorm(jnp.einsum('bse,ed->bsd', mixed, w_out), post_g)
    return x + gate[:, None, :] * y


def odd_layer(x, c, pre_g, post_g, ada_w, ada_b, w_in, w_out, rpb, lin):
    b, s, _ = x.shape
    shift, scale, gate = adaln(c, ada_w, ada_b)
    h = rmsnorm(x, pre_g) * (1 + scale[:, None, :]) + shift[:, None, :]
    p = jnp.einsum('bsd,de->bse', h, w_in)
    qc, kc, vc, gc, ud, gd = split_cols(p, CD_SPLITS)

    def heads(t):
        return t.reshape(b, s, C_HEADS, HEAD_DIM)

    oc = neighbourhood_attention(heads(qc), heads(kc), heads(vc), rpb).reshape(b, s, C_WIDTH)
    od = fourier_mixing(ud, lin)
    mixed = jnp.concatenate([oc * jax.nn.silu(gc), od * jax.nn.silu(gd)], axis=-1)
    y = rmsnorm(jnp.einsum('bse,ed->bsd', mixed, w_out), post_g)
    return x + gate[:, None, :] * y


def trunk(x, c, pre_g, post_g, ada_w, ada_b, w_in_ab, w_out_ab, qn_a, kn_a, w_in_cd, w_out_cd, rpb_c, lin_d):
    tabs = position_tables(x.shape[1])
    for i in range(DEPTH):
        j = i // 2
        if i % 2 == 0:
            x = even_layer(x, c, pre_g[i], post_g[i], ada_w[i], ada_b[i],
                           w_in_ab[j], w_out_ab[j], qn_a[j], kn_a[j], tabs)
        else:
            x = odd_layer(x, c, pre_g[i], post_g[i], ada_w[i], ada_b[i],
                          w_in_cd[j], w_out_cd[j], rpb_c[j], lin_d[j])
    return x


def setup_inputs(seed: int = 0) -> dict:
    key = jax.random.key(seed)
    ks = jax.random.split(key, 16)
    f32 = jnp.float32
    d = D_MODEL
    return {
        'x_prompt': jax.random.normal(ks[0], (BATCH, SEQ, d), f32),
        'x_sample': jax.random.normal(ks[1], (DEC_BATCH, DEC_SEQ, d), f32),
        'c_prompt': jax.random.normal(ks[2], (BATCH, d), f32),
        'c_sample': jax.random.normal(ks[3], (DEC_BATCH, d), f32),
        'pre_g': 1.0 + 0.1 * jax.random.normal(ks[4], (DEPTH, d), f32),
        'post_g': 1.0 + 0.1 * jax.random.normal(ks[5], (DEPTH, d), f32),
        'ada_w': 0.3 * d ** -0.5 * jax.random.normal(ks[6], (DEPTH, d, 3 * d), f32),
        'ada_b': 0.01 * jax.random.normal(ks[7], (DEPTH, 3 * d), f32),
        'w_in_ab': d ** -0.5 * jax.random.normal(ks[8], (N_EVEN, d, AB_IN), f32),
        'w_out_ab': AB_OUT ** -0.5 * jax.random.normal(ks[9], (N_EVEN, AB_OUT, d), f32),
        'qn_a': 1.0 + 0.1 * jax.random.normal(ks[10], (N_EVEN, HEAD_DIM), f32),
        'kn_a': 1.0 + 0.1 * jax.random.normal(ks[11], (N_EVEN, HEAD_DIM), f32),
        'w_in_cd': d ** -0.5 * jax.random.normal(ks[12], (N_ODD, d, CD_IN), f32),
        'w_out_cd': CD_OUT ** -0.5 * jax.random.normal(ks[13], (N_ODD, CD_OUT, d), f32),
        'rpb_c': 0.1 * jax.random.normal(ks[14], (N_ODD, C_HEADS, 2 * C_WIN_H - 1, 2 * C_WIN_W - 1), f32),
        'lin_d': D_WIDTH ** -0.5 * jax.random.normal(ks[15], (N_ODD, D_WIDTH, D_WIDTH), f32),
    }


def reference(x_prompt, x_sample, c_prompt, c_sample, pre_g, post_g, ada_w, ada_b,
              w_in_ab, w_out_ab, qn_a, kn_a, w_in_cd, w_out_cd, rpb_c, lin_d):
    y_prompt = trunk(x_prompt, c_prompt, pre_g, post_g, ada_w, ada_b, w_in_ab, w_out_ab,
                     qn_a, kn_a, w_in_cd, w_out_cd, rpb_c, lin_d)
    y_sample = trunk(x_sample, c_sample, pre_g, post_g, ada_w, ada_b, w_in_ab, w_out_ab,
                     qn_a, kn_a, w_in_cd, w_out_cd, rpb_c, lin_d)
    return (y_prompt, y_sample)
```

```python
import functools
import math

import numpy as np
import jax
import jax.numpy as jnp
from jax import lax
from jax.experimental import pallas as pl
from jax.experimental.pallas import tpu as pltpu

F32 = jnp.float32
BF16 = jnp.bfloat16

D_MODEL = 1024
DEPTH = 4
HEAD_DIM = 64
GRID_W = 64
A_ROPE_THETA = 10000.0
B_ROPE_THETA = 500000.0
B_ROPE_DIMS = 16
B_DILATIONS = (1, 4, 16)
B_RADIUS = 64
C_HEADS = 12
C_WIN_H = 8
C_WIN_W = 16
D_GROUPS = 4
D_GROUP_DIM = 64
D_WIDTH = D_GROUPS * D_GROUP_DIM
AB_IN = 3328
CD_IN = 3584
EPS = 1e-6
NEG = -1e30
LOG2E = 1.4426950408889634
QK_SCALE = LOG2E * HEAD_DIM ** -0.5

LANES = 128
VMEM_LIMIT = 56 * 1024 * 1024

TM_PROJ = 512
TQ_A = 128
TK_A = 2048
LQ_B = 256
SUB_B = 128
ROWS_C = 4
TN_D1 = 4096
TK2_D = 8


def _params(*sem):
    return pltpu.CompilerParams(dimension_semantics=sem, vmem_limit_bytes=VMEM_LIMIT)


def _silu(x):
    return x / (1.0 + jnp.exp(-x))


def _lane_iota(rows):
    return lax.broadcasted_iota(jnp.int32, (rows, LANES), 1)


def _adaln_kernel(c_ref, w_ref, b_ref, o_ref):
    a = _silu(c_ref[...]).astype(BF16)
    o_ref[...] = jnp.dot(a, w_ref[...].astype(BF16), preferred_element_type=F32) + b_ref[...]


def _adaln(c_all, ada_w, ada_b):
    bt = c_all.shape[0]
    d = D_MODEL
    return pl.pallas_call(
        _adaln_kernel,
        grid=(DEPTH, 3),
        in_specs=[
            pl.BlockSpec((bt, d), lambda l, j: (0, 0)),
            pl.BlockSpec((None, d, d), lambda l, j: (l, 0, j)),
            pl.BlockSpec((None, 1, d), lambda l, j: (l, 0, j)),
        ],
        out_specs=pl.BlockSpec((None, bt, d), lambda l, j: (l, 0, j)),
        out_shape=jax.ShapeDtypeStruct((DEPTH, bt, 3 * d), F32),
        compiler_params=_params("parallel", "parallel"),
        name="adaln",
    )(c_all, ada_w, ada_b.reshape(DEPTH, 1, 3 * d))


def _modulated_norm(x, g, scale, shift):
    ms = jnp.mean(x * x, axis=-1, keepdims=True)
    y = x * lax.rsqrt(ms + EPS) * g
    return (y * (1.0 + scale) + shift).astype(BF16)


def _head_norm(xs, gain, gm):
    x2 = xs * xs
    hi = x2.astype(BF16)
    lo = (x2 - hi.astype(F32)).astype(BF16)
    ss = jnp.dot(hi, gm, preferred_element_type=F32) + jnp.dot(lo, gm, preferred_element_type=F32)
    return xs * lax.rsqrt(ss * (1.0 / HEAD_DIM) + EPS) * gain


def _rope(xs, cos, sin_signed, first_half, shift):
    up = pltpu.roll(xs, LANES - shift, 1)
    dn = pltpu.roll(xs, shift, 1)
    return xs * cos + jnp.where(first_half, up, dn) * sin_signed


def _inproj_even_kernel(x_ref, sc_ref, sh_ref, g_ref, w_ref, qn_ref, kn_ref,
                        ca_ref, sa_ref, cb_ref, sb_ref, gm_ref,
                        qa_ref, ka_ref, va_ref, gate_ref, qb_ref, kb_ref, vb_ref):
    tm = x_ref.shape[0]
    h = _modulated_norm(x_ref[...], g_ref[...], sc_ref[...], sh_ref[...])
    lane = _lane_iota(tm)
    half_a = (lane & 16) == 0
    half_b = (lane & 8) == 0
    lo = lane < HEAD_DIM
    gm = gm_ref[...]
    ca, sa, cb, sb = ca_ref[...], sa_ref[...], cb_ref[...], sb_ref[...]

    def proj(a, b):
        return jnp.dot(h, w_ref[:, a:b], preferred_element_type=F32)

    p = proj(0, 512)
    for j in range(4):
        xs = _head_norm(p[:, LANES * j:LANES * (j + 1)], qn_ref[...], gm)
        qa_ref[:, LANES * j:LANES * (j + 1)] = (_rope(xs, ca, sa, half_a, 16) * QK_SCALE).astype(BF16)
    xs = _head_norm(proj(512, 640), kn_ref[...], gm)
    ka_ref[...] = _rope(xs, ca, sa, half_a, 16).astype(BF16)
    v = proj(640, 768)
    va_ref[:, 0:LANES] = jnp.where(lo, v, 1.0).astype(BF16)
    va_ref[:, LANES:2 * LANES] = jnp.where(lo, 1.0, v).astype(BF16)
    gate_ref[:, 0:512] = _silu(proj(768, 1280)).astype(BF16)
    p = proj(1280, 1792)
    for j in range(4):
        xs = p[:, LANES * j:LANES * (j + 1)]
        qb_ref[:, LANES * j:LANES * (j + 1)] = (_rope(xs, cb, sb, half_b, 8) * QK_SCALE).astype(BF16)
    p = proj(1792, 2304)
    for j in range(4):
        xs = p[:, LANES * j:LANES * (j + 1)]
        kb_ref[:, LANES * j:LANES * (j + 1)] = _rope(xs, cb, sb, half_b, 8).astype(BF16)
    vb_ref[...] = proj(2304, 2816).astype(BF16)
    gate_ref[:, 512:1024] = _silu(proj(2816, 3328)).astype(BF16)


def _tok_spec(tm, width):
    return pl.BlockSpec((None, tm, width), lambda b, i: (b, i, 0))


def _bcast_spec(width):
    return pl.BlockSpec((None, 1, width), lambda b, i: (b, 0, 0))


def _const_spec(shape):
    return pl.BlockSpec(shape, lambda b, i: (0,) * len(shape))


def _inproj_even(x, scale, shift, pre_g, w_in, qn, kn, tabs, gm):
    bsz, s, d = x.shape
    tm = min(TM_PROJ, s)
    ca, sa, cb, sb = tabs
    widths = (512, 128, 256, 1024, 512, 512, 512)
    tab_spec = pl.BlockSpec((tm, LANES), lambda b, i: (i, 0))
    return pl.pallas_call(
        _inproj_even_kernel,
        grid=(bsz, s // tm),
        in_specs=[
            _tok_spec(tm, d), _bcast_spec(d), _bcast_spec(d), _const_spec((1, d)),
            _const_spec((d, AB_IN)), _const_spec((1, LANES)), _const_spec((1, LANES)),
            tab_spec, tab_spec, tab_spec, tab_spec, _const_spec((LANES, LANES)),
        ],
        out_specs=[_tok_spec(tm, w) for w in widths],
        out_shape=[jax.ShapeDtypeStruct((bsz, s, w), BF16) for w in widths],
        compiler_params=_params("parallel", "parallel"),
        name="inproj_even",
    )(x, scale, shift, pre_g, w_in, qn, kn, ca, sa, cb, sb, gm)


def _inproj_odd_kernel(x_ref, sc_ref, sh_ref, g_ref, w_ref, wc_ref,
                       q_ref, k_ref, v_ref, gate_ref, vd_ref):
    h = _modulated_norm(x_ref[...], g_ref[...], sc_ref[...], sh_ref[...])

    def proj(a, b):
        return jnp.dot(h, w_ref[:, a:b], preferred_element_type=F32)

    q_ref[...] = (proj(0, 768) * QK_SCALE).astype(BF16)
    k_ref[...] = proj(768, 1536).astype(BF16)
    v_ref[...] = proj(1536, 2304).astype(BF16)
    gate_ref[:, 0:768] = _silu(proj(2304, 3072)).astype(BF16)
    u = proj(3072, 3328).astype(BF16)
    vc = jnp.dot(u, wc_ref[...], preferred_element_type=F32)
    vd_ref[0] = vc[:, 0:D_WIDTH].astype(BF16)
    vd_ref[1] = vc[:, D_WIDTH:2 * D_WIDTH].astype(BF16)
    gate_ref[:, 768:1024] = _silu(proj(3328, 3584)).astype(BF16)


def _inproj_odd(x, scale, shift, pre_g, w_in, wc):
    bsz, s, d = x.shape
    tm = min(TM_PROJ, s)
    widths = (768, 768, 768, 1024)
    return pl.pallas_call(
        _inproj_odd_kernel,
        grid=(bsz, s // tm),
        in_specs=[
            _tok_spec(tm, d), _bcast_spec(d), _bcast_spec(d), _const_spec((1, d)),
            _const_spec((d, CD_IN)), _const_spec((D_WIDTH, 2 * D_WIDTH)),
        ],
        out_specs=[_tok_spec(tm, w) for w in widths]
        + [pl.BlockSpec((None, 2, tm, D_WIDTH), lambda b, i: (b, 0, i, 0))],
        out_shape=[jax.ShapeDtypeStruct((bsz, s, w), BF16) for w in widths]
        + [jax.ShapeDtypeStruct((bsz, 2, s, D_WIDTH), BF16)],
        compiler_params=_params("parallel", "parallel"),
        name="inproj_odd",
    )(x, scale, shift, pre_g, w_in, wc)


def _attn_a_kernel(q_ref, k_ref, v_ref, o_ref, *, tk):
    tq = q_ref.shape[0]
    s_len = k_ref.shape[0]
    lane = _lane_iota(tq)
    lo = lane < HEAD_DIM
    q = q_ref[...].astype(F32)
    heads = [None] * 8
    for g in range(2):
        keep = lo if g == 0 else jnp.logical_not(lo)
        parts = []
        for hh in range(4):
            hd = 4 * g + hh
            slab = q[:, LANES * (hd // 2):LANES * (hd // 2 + 1)]
            if hd % 2 != g:
                slab = pltpu.roll(slab, HEAD_DIM, 1)
            parts.append(jnp.where(keep, slab, 0.0).astype(BF16))
        qg = jnp.concatenate(parts, axis=0)

        def body(j, carry, qg=qg, g=g):
            m, acc = carry
            off = pl.multiple_of(j * tk, tk)
            kc = k_ref[pl.ds(off, tk), :]
            vc = v_ref[pl.ds(off, tk), LANES * g:LANES * (g + 1)]
            s = lax.dot_general(qg, kc, (((1,), (1,)), ((), ())), preferred_element_type=F32)
            m_new = jnp.maximum(m, jnp.max(s, axis=1, keepdims=True))
            p = jnp.exp2(s - m_new).astype(BF16)
            acc = jnp.exp2(m - m_new) * acc + jnp.dot(p, vc, preferred_element_type=F32)
            return m_new, acc

        m0 = jnp.full((4 * tq, 1), NEG, F32)
        acc0 = jnp.zeros((4 * tq, LANES), F32)
        _, acc = lax.fori_loop(0, s_len // tk, body, (m0, acc0))
        o = acc / pltpu.roll(acc, HEAD_DIM, 1)
        for hh in range(4):
            heads[4 * g + hh] = o[hh * tq:(hh + 1) * tq, :]
    for sl in range(4):
        g = sl // 2
        a0, a1 = heads[2 * sl], heads[2 * sl + 1]
        if g == 1:
            a0 = pltpu.roll(a0, HEAD_DIM, 1)
        else:
            a1 = pltpu.roll(a1, HEAD_DIM, 1)
        o_ref[:, LANES * sl:LANES * (sl + 1)] = jnp.where(lo, a0, a1).astype(BF16)


def _attn_a(q, k, v):
    bsz, s, _ = q.shape
    tq = min(TQ_A, s)
    tk = min(TK_A, s)
    return pl.pallas_call(
        functools.partial(_attn_a_kernel, tk=tk),
        grid=(bsz, s // tq),
        in_specs=[
            pl.BlockSpec((None, tq, 512), lambda b, i: (b, i, 0)),
            pl.BlockSpec((None, s, LANES), lambda b, i: (b, 0, 0)),
            pl.BlockSpec((None, s, 2 * LANES), lambda b, i: (b, 0, 0)),
        ],
        out_specs=pl.BlockSpec((None, tq, 512), lambda b, i: (b, i, 0)),
        out_shape=jax.ShapeDtypeStruct((bsz, s, 512), BF16),
        compiler_params=_params("parallel", "arbitrary"),
        name="attn_a",
    )(q, k, v)


def _window(refs, start, size, cols, lq):
    pieces = []
    for n, r in enumerate(refs):
        a, b = max(start, n * lq), min(start + size, (n + 1) * lq)
        if a < b:
            pieces.append(r[a - n * lq:b - n * lq, cols])
    return pieces[0] if len(pieces) == 1 else jnp.concatenate(pieces, axis=0)


def _attn_b_kernel(q_ref, kp_ref, kc_ref, kn_ref, vp_ref, vc_ref, vn_ref, o_ref, lse_ref, *, l_len):
    lq = q_ref.shape[0]
    sub = min(SUB_B, lq)
    base = pl.program_id(2) * lq
    lane = _lane_iota(sub)
    lo = lane < HEAD_DIM
    rq = lax.broadcasted_iota(jnp.int32, (sub, 2 * sub), 0)
    ck = lax.broadcasted_iota(jnp.int32, (sub, 2 * sub), 1)
    delta = ck - rq - sub // 2
    band = jnp.abs(delta) <= B_RADIUS
    for sb in range(lq // sub):
        r0 = sb * sub
        kpos = jnp.where(band, base + (r0 - sub // 2) + ck, -1)
        valid = jnp.logical_and(kpos >= 0, kpos < l_len)
        start = lq + r0 - sub // 2
        lse_tile = jnp.zeros((sub, LANES), F32)
        for j in range(4):
            cols = slice(LANES * j, LANES * (j + 1))
            qs = q_ref[r0:r0 + sub, cols].astype(F32)
            kw = _window((kp_ref, kc_ref, kn_ref), start, 2 * sub, cols, lq)
            vw = _window((vp_ref, vc_ref, vn_ref), start, 2 * sub, cols, lq)
            outs = []
            for half in range(2):
                keep = lo if half == 0 else jnp.logical_not(lo)
                qh = jnp.where(keep, qs, 0.0).astype(BF16)
                s = lax.dot_general(qh, kw, (((1,), (1,)), ((), ())), preferred_element_type=F32)
                s = jnp.where(valid, s, NEG)
                m = jnp.max(s, axis=1, keepdims=True)
                p = jnp.exp2(s - m)
                l = jnp.sum(p, axis=1, keepdims=True)
                o = jnp.dot(p.astype(BF16), vw, preferred_element_type=F32)
                outs.append(o * (1.0 / l))
                lse_tile = lse_tile + jnp.where(lane == 2 * j + half, m + jnp.log2(l), 0.0)
            o_ref[r0:r0 + sub, cols] = jnp.where(lo, outs[0], outs[1]).astype(BF16)
        lse_ref[r0:r0 + sub, :] = lse_tile


def _attn_b(q, k, v, dil):
    bsz, s, w = q.shape
    l_len = s // dil
    lq = min(LQ_B, l_len)
    nblk = l_len // lq
    view = lambda t: t.reshape(bsz, l_len, dil * w)
    cur = pl.BlockSpec((None, lq, w), lambda b, r, i: (b, i, r))
    prv = pl.BlockSpec((None, lq, w), lambda b, r, i: (b, jnp.maximum(i - 1, 0), r))
    nxt = pl.BlockSpec((None, lq, w), lambda b, r, i: (b, jnp.minimum(i + 1, nblk - 1), r))
    o, lse = pl.pallas_call(
        functools.partial(_attn_b_kernel, l_len=l_len),
        grid=(bsz, dil, nblk),
        in_specs=[cur, prv, cur, nxt, prv, cur, nxt],
        out_specs=[cur, pl.BlockSpec((None, lq, LANES), lambda b, r, i: (b, i, r))],
        out_shape=[jax.ShapeDtypeStruct((bsz, l_len, dil * w), BF16),
                   jax.ShapeDtypeStruct((bsz, l_len, dil * LANES), F32)],
        compiler_params=_params("parallel", "parallel", "parallel"),
        name=f"attn_b_d{dil}",
    )(view(q), view(k), view(k), view(k), view(v), view(v), view(v))
    return o.reshape(bsz, s, w), lse.reshape(bsz, s, LANES)


def _attn_c_kernel(q_ref, kp_ref, kc_ref, kn_ref, vp_ref, vc_ref, vn_ref, bias_ref, o_ref):
    tq = q_ref.shape[0]
    lane = _lane_iota(tq)
    lo = lane < HEAD_DIM
    for pr in range(C_HEADS // 2):
        cols = slice(LANES * pr, LANES * (pr + 1))
        qs = q_ref[:, cols].astype(F32)
        k3 = jnp.concatenate([kp_ref[:, cols], kc_ref[:, cols], kn_ref[:, cols]], axis=0)
        v3 = jnp.concatenate([vp_ref[:, cols], vc_ref[:, cols], vn_ref[:, cols]], axis=0)
        outs = []
        for half in range(2):
            keep = lo if half == 0 else jnp.logical_not(lo)
            qh = jnp.where(keep, qs, 0.0).astype(BF16)
            s = lax.dot_general(qh, k3, (((1,), (1,)), ((), ())), preferred_element_type=F32)
            s = s + bias_ref[2 * pr + half]
            m = jnp.max(s, axis=1, keepdims=True)
            p = jnp.exp2(s - m)
            l = jnp.sum(p, axis=1, keepdims=True)
            o = jnp.dot(p.astype(BF16), v3, preferred_element_type=F32)
            outs.append(o * (1.0 / l))
        o_ref[:, cols] = jnp.where(lo, outs[0], outs[1]).astype(BF16)


def _bias_c(rpb, rows):
    nq, nu = ROWS_C, 3 * ROWS_C
    j = np.arange(nq)[:, None]
    u = np.arange(nu)[None, :]
    assert nq == C_WIN_H // 2 and rows >= C_WIN_H and rows % nq == 0
    row_idx = np.clip((u - nq) - j + C_WIN_H - 1, 0, 2 * C_WIN_H - 2)
    interior = (u - j >= nq - C_WIN_H // 2) & (u - j < nq + C_WIN_H // 2)
    first = np.broadcast_to((u >= nq) & (u < nq + C_WIN_H), (nq, nu))
    last = np.broadcast_to((u >= 2 * nq - C_WIN_H) & (u < 2 * nq), (nq, nu))
    row_ok = np.stack([first, interior, last])
    cols = np.arange(GRID_W)
    col_start = np.clip(cols - C_WIN_W // 2, 0, GRID_W - C_WIN_W)
    col_ok = (cols[None, :] >= col_start[:, None]) & (cols[None, :] < col_start[:, None] + C_WIN_W)
    col_idx = np.clip(cols[None, :] - cols[:, None] + C_WIN_W - 1, 0, 2 * C_WIN_W - 2)
    tab = rpb[:, row_idx[:, :, None, None], col_idx[None, None, :, :]].astype(F32) * LOG2E
    ok = row_ok[:, None, :, :, None, None] & col_ok[None, None, None, None, :, :]
    tab = jnp.where(ok, tab[None], NEG)
    tab = tab.transpose(0, 1, 2, 4, 3, 5)
    return tab.reshape(3, C_HEADS, nq * GRID_W, nu * GRID_W)


def _attn_c(q, k, v, bias):
    bsz, s, w = q.shape
    tq = ROWS_C * GRID_W
    nblk = s // tq
    kind = lambda i: jnp.where(i == 0, 0, jnp.where(i == nblk - 1, 2, 1))
    cur = pl.BlockSpec((None, tq, w), lambda b, i: (b, i, 0))
    prv = pl.BlockSpec((None, tq, w), lambda b, i: (b, jnp.maximum(i - 1, 0), 0))
    nxt = pl.BlockSpec((None, tq, w), lambda b, i: (b, jnp.minimum(i + 1, nblk - 1), 0))
    return pl.pallas_call(
        _attn_c_kernel,
        grid=(bsz, nblk),
        in_specs=[cur, prv, cur, nxt, prv, cur, nxt,
                  pl.BlockSpec((None, C_HEADS, tq, 3 * tq), lambda b, i: (kind(i), 0, 0, 0))],
        out_specs=cur,
        out_shape=jax.ShapeDtypeStruct((bsz, s, w), BF16),
        compiler_params=_params("parallel", "arbitrary"),
        name="attn_c",
    )(q, k, k, k, v, v, v, bias)


S2_D = 64


def _dft_consts(s):
    s1 = s // S2_D
    th2 = 2 * np.pi * np.outer(np.arange(S2_D), np.arange(S2_D)) / S2_D
    c2, n2 = np.cos(th2) / 8.0, np.sin(th2) / 8.0
    w2big = np.block([[c2, n2], [-n2, c2]])
    th1 = 2 * np.pi * np.outer(np.arange(s1), np.arange(s1)) / s1
    w1cat = np.concatenate([np.cos(th1), np.sin(th1)], axis=1) / math.sqrt(s1)
    tht = 2 * np.pi * np.outer(np.arange(S2_D), np.arange(s1)) / s
    return (jnp.asarray(w2big, BF16), jnp.asarray(w1cat, BF16),
            jnp.asarray(np.cos(tht), F32), jnp.asarray(-np.sin(tht), F32))


def _chan_dft_const():
    th = 2 * np.pi * np.outer(np.arange(D_GROUP_DIM), np.arange(D_GROUP_DIM)) / D_GROUP_DIM
    eye = np.eye(D_GROUPS)
    wc = np.concatenate([np.kron(eye, np.cos(th)), np.kron(eye, -np.sin(th))], axis=1) / 8.0
    return jnp.asarray(wc, BF16)


def _dft1_kernel(v_ref, w_ref, tr_ref, ti_ref, o_ref):
    x = jnp.concatenate([v_ref[0], v_ref[1]], axis=0)
    b = jnp.dot(w_ref[...], x, preferred_element_type=F32)
    br, bi = b[0:S2_D], b[S2_D:2 * S2_D]
    tr, ti = tr_ref[...], ti_ref[...]
    o_ref[0] = (br * tr - bi * ti).astype(BF16)
    o_ref[1] = (br * ti + bi * tr).astype(BF16)


def _dft2_kernel(b_ref, w_ref, o_ref):
    for kk in range(b_ref.shape[1]):
        rhs = jnp.concatenate([b_ref[0, kk], b_ref[1, kk]], axis=0)
        x = jnp.dot(w_ref[...], rhs, preferred_element_type=F32)
        o_ref[:, D_WIDTH * kk:D_WIDTH * (kk + 1)] = x.astype(BF16)


def _fourier(vd, consts):
    bsz, _, s, w = vd.shape
    s1 = s // S2_D
    w2big, w1cat, tr, ti = consts
    ncol = s1 * w
    tn = min(TN_D1, ncol)
    twr = jnp.repeat(tr, w, axis=1)
    twi = jnp.repeat(ti, w, axis=1)
    b1 = pl.pallas_call(
        _dft1_kernel,
        grid=(bsz, ncol // tn),
        in_specs=[
            pl.BlockSpec((None, 2, S2_D, tn), lambda b, j: (b, 0, 0, j)),
            pl.BlockSpec((2 * S2_D, 2 * S2_D), lambda b, j: (0, 0)),
            pl.BlockSpec((S2_D, tn), lambda b, j: (0, j)),
            pl.BlockSpec((S2_D, tn), lambda b, j: (0, j)),
        ],
        out_specs=pl.BlockSpec((None, 2, S2_D, tn), lambda b, j: (b, 0, 0, j)),
        out_shape=jax.ShapeDtypeStruct((bsz, 2, S2_D, ncol), BF16),
        compiler_params=_params("parallel", "parallel"),
        name="dft_stage1",
    )(vd.reshape(bsz, 2, S2_D, ncol), w2big, twr, twi)
    tk2 = TK2_D
    f = pl.pallas_call(
        _dft2_kernel,
        grid=(bsz, S2_D // tk2),
        in_specs=[
            pl.BlockSpec((None, 2, tk2, s1, w), lambda b, j: (b, 0, j, 0, 0)),
            pl.BlockSpec((s1, 2 * s1), lambda b, j: (0, 0)),
        ],
        out_specs=pl.BlockSpec((None, s1, tk2 * w), lambda b, j: (b, 0, j)),
        out_shape=jax.ShapeDtypeStruct((bsz, s1, S2_D * w), BF16),
        compiler_params=_params("parallel", "parallel"),
        name="dft_stage2",
    )(b1.reshape(bsz, 2, S2_D, s1, w), w1cat)
    return f.reshape(bsz, s, w)


def _finish(y, x_ref, gate_ref, g_ref, o_ref):
    ms = jnp.mean(y * y, axis=-1, keepdims=True)
    yn = y * lax.rsqrt(ms + EPS) * g_ref[...]
    o_ref[...] = x_ref[...] + gate_ref[...] * yn


def _outproj_even_kernel(oa_ref, o1_ref, o2_ref, o3_ref, l1_ref, l2_ref, l3_ref, gates_ref,
                         ex_ref, w_ref, x_ref, gate_ref, g_ref, o_ref):
    l1, l2, l3 = l1_ref[...], l2_ref[...], l3_ref[...]
    mx = jnp.maximum(jnp.maximum(l1, l2), l3)
    e1, e2, e3 = jnp.exp2(l1 - mx), jnp.exp2(l2 - mx), jnp.exp2(l3 - mx)
    inv = 1.0 / (e1 + e2 + e3)
    ex = ex_ref[...]

    def expand(wgt):
        hi = wgt.astype(BF16)
        lo = (wgt - hi.astype(F32)).astype(BF16)
        return jnp.dot(hi, ex, preferred_element_type=F32) + jnp.dot(lo, ex, preferred_element_type=F32)

    ob = (expand(e1 * inv) * o1_ref[...].astype(F32) + expand(e2 * inv) * o2_ref[...].astype(F32)
          + expand(e3 * inv) * o3_ref[...].astype(F32))
    ma = (oa_ref[...].astype(F32) * gates_ref[:, 0:512].astype(F32)).astype(BF16)
    mb = (ob * gates_ref[:, 512:1024].astype(F32)).astype(BF16)
    y = (jnp.dot(ma, w_ref[0:512, :], preferred_element_type=F32)
         + jnp.dot(mb, w_ref[512:1024, :], preferred_element_type=F32))
    _finish(y, x_ref, gate_ref, g_ref, o_ref)


def _outproj_even(oa, obs, lses, gates, ex, w_out, x, gate, post_g):
    bsz, s, d = x.shape
    tm = min(TM_PROJ, s)
    return pl.pallas_call(
        _outproj_even_kernel,
        grid=(bsz, s // tm),
        in_specs=[_tok_spec(tm, 512)] * 4 + [_tok_spec(tm, LANES)] * 3 + [
            _tok_spec(tm, 1024), _const_spec((LANES, 512)), _const_spec((d, d)),
            _tok_spec(tm, d), _bcast_spec(d), _const_spec((1, d))],
        out_specs=_tok_spec(tm, d),
        out_shape=jax.ShapeDtypeStruct((bsz, s, d), F32),
        compiler_params=_params("parallel", "parallel"),
        name="outproj_even",
    )(oa, *obs, *lses, gates, ex, w_out, x, gate, post_g)


def _outproj_odd_kernel(oc_ref, f_ref, gates_ref, lin_ref, w_ref, x_ref, gate_ref, g_ref, o_ref):
    od = jnp.dot(f_ref[...], lin_ref[...], preferred_element_type=F32)
    mc = (oc_ref[...].astype(F32) * gates_ref[:, 0:768].astype(F32)).astype(BF16)
    md = (od * gates_ref[:, 768:1024].astype(F32)).astype(BF16)
    y = (jnp.dot(mc, w_ref[0:768, :], preferred_element_type=F32)
         + jnp.dot(md, w_ref[768:1024, :], preferred_element_type=F32))
    _finish(y, x_ref, gate_ref, g_ref, o_ref)


def _outproj_odd(oc, f, gates, lin, w_out, x, gate, post_g):
    bsz, s, d = x.shape
    tm = min(TM_PROJ, s)
    return pl.pallas_call(
        _outproj_odd_kernel,
        grid=(bsz, s // tm),
        in_specs=[_tok_spec(tm, 768), _tok_spec(tm, D_WIDTH), _tok_spec(tm, 1024),
                  _const_spec((D_WIDTH, D_WIDTH)), _const_spec((d, d)),
                  _tok_spec(tm, d), _bcast_spec(d), _const_spec((1, d))],
        out_specs=_tok_spec(tm, d),
        out_shape=jax.ShapeDtypeStruct((bsz, s, d), F32),
        compiler_params=_params("parallel", "parallel"),
        name="outproj_odd",
    )(oc, f, gates, lin, w_out, x, gate, post_g)


def _rope_tables(s):
    t = jnp.arange(s)

    def tab(pos, dim, theta):
        inv = theta ** (-jnp.arange(0, dim, 2, dtype=F32) / dim)
        ang = pos[:, None] * inv[None, :]
        return jnp.cos(ang), jnp.sin(ang)

    cr, sr = tab((t // GRID_W).astype(F32), HEAD_DIM // 2, A_ROPE_THETA)
    cc, sc = tab((t % GRID_W).astype(F32), HEAD_DIM // 2, A_ROPE_THETA)
    cb, sb = tab(t.astype(F32), B_ROPE_DIMS, B_ROPE_THETA)
    rest = HEAD_DIM - B_ROPE_DIMS
    ca = jnp.concatenate([cr, cr, cc, cc] * 2, axis=-1)
    sa = jnp.concatenate([-sr, sr, -sc, sc] * 2, axis=-1)
    cb = jnp.concatenate([cb, cb, jnp.ones((s, rest), F32)] * 2, axis=-1)
    sb = jnp.concatenate([-sb, sb, jnp.zeros((s, rest), F32)] * 2, axis=-1)
    return ca, sa, cb, sb


def _even_layer(x, mod, pre_g, post_g, w_in, w_out, qn, kn, tabs, gm, ex):
    shift, scale, gate = mod
    qa, ka, va, gates, qb, kb, vb = _inproj_even(x, scale, shift, pre_g, w_in, qn, kn, tabs, gm)
    oa = _attn_a(qa, ka, va)
    obs, lses = zip(*[_attn_b(qb, kb, vb, dil) for dil in B_DILATIONS])
    return _outproj_even(oa, obs, lses, gates, ex, w_out, x, gate, post_g)


def _odd_layer(x, mod, pre_g, post_g, w_in, w_out, bias, lin, wc, dft):
    shift, scale, gate = mod
    qc, kc, vc, gates, vd = _inproj_odd(x, scale, shift, pre_g, w_in, wc)
    oc = _attn_c(qc, kc, vc, bias)
    f = _fourier(vd, dft)
    return _outproj_odd(oc, f, gates, lin, w_out, x, gate, post_g)


def _trunk(x, mods, pre_g, post_g, w_in_ab, w_out_ab, qn_a, kn_a, w_in_cd, w_out_cd, rpb_c, lin_d):
    bsz, s, d = x.shape
    tabs = _rope_tables(s)
    dft = _dft_consts(s)
    wc = _chan_dft_const()
    lane = np.arange(LANES)
    gm = jnp.asarray(lane[:, None] // HEAD_DIM == lane[None, :] // HEAD_DIM, BF16)
    ex = jnp.asarray(lane[:, None] == np.arange(512)[None, :] // HEAD_DIM, BF16)
    for i in range(DEPTH):
        j = i // 2
        mod = tuple(mods[i][:, None, k * d:(k + 1) * d] for k in range(3))
        pg, qg = pre_g[i][None, :], post_g[i][None, :]
        if i % 2 == 0:
            qn = jnp.tile(qn_a[j], 2)[None, :]
            kn = jnp.tile(kn_a[j], 2)[None, :]
            x = _even_layer(x, mod, pg, qg, w_in_ab[j].astype(BF16), w_out_ab[j].astype(BF16),
                            qn, kn, tabs, gm, ex)
        else:
            bias = _bias_c(rpb_c[j], s // GRID_W)
            x = _odd_layer(x, mod, pg, qg, w_in_cd[j].astype(BF16), w_out_cd[j].astype(BF16),
                           bias, lin_d[j].astype(BF16), wc, dft)
    return x


def kernel(x_prompt, x_sample, c_prompt, c_sample, pre_g, post_g, ada_w, ada_b,
           w_in_ab, w_out_ab, qn_a, kn_a, w_in_cd, w_out_cd, rpb_c, lin_d):
    nb = x_prompt.shape[0]
    mods = _adaln(jnp.concatenate([c_prompt, c_sample], axis=0), ada_w, ada_b)
    args = (pre_g, post_g, w_in_ab, w_out_ab, qn_a, kn_a, w_in_cd, w_out_cd, rpb_c, lin_d)
    y_prompt = _trunk(x_prompt, mods[:, :nb], *args)
    y_sample = _trunk(x_sample, mods[:, nb:], *args)
    return (y_prompt, y_sample)
```

```python
import functools
import math

import numpy as np
import jax
import jax.numpy as jnp
from jax import lax
from jax.experimental import pallas as pl
from jax.experimental.pallas import tpu as pltpu

F32 = jnp.float32
BF16 = jnp.bfloat16

D_MODEL = 1024
DEPTH = 4
HEAD_DIM = 64
GRID_W = 64
A_ROPE_THETA = 10000.0
B_ROPE_THETA = 500000.0
B_ROPE_DIMS = 16
B_DILATIONS = (1, 4, 16)
B_RADIUS = 64
C_HEADS = 12
C_WIN_H = 8
C_WIN_W = 16
D_GROUPS = 4
D_GROUP_DIM = 64
D_WIDTH = D_GROUPS * D_GROUP_DIM
AB_IN = 3328
CD_IN = 3584
EPS = 1e-6
NEG = -1e30
LOG2E = 1.4426950408889634
QK_SCALE = LOG2E * HEAD_DIM ** -0.5

LANES = 128
VMEM_LIMIT = 56 * 1024 * 1024

TM_PROJ = 512
TQ_A = 128
TK_A = 2048
LQ_B = 256
SUB_B = 128
ROWS_C = 4
TN_D1 = 4096
TK2_D = 8
LSE_PARTS = 3


def _params(*sem):
    return pltpu.CompilerParams(dimension_semantics=sem, vmem_limit_bytes=VMEM_LIMIT)


def _silu(x):
    return x / (1.0 + jnp.exp(-x))


def _lane_iota(rows):
    return lax.broadcasted_iota(jnp.int32, (rows, LANES), 1)


def _adaln_kernel(c_ref, w_ref, b_ref, o_ref):
    a = _silu(c_ref[...]).astype(BF16)
    o_ref[...] = jnp.dot(a, w_ref[...].astype(BF16), preferred_element_type=F32) + b_ref[...]


def _adaln(c_all, ada_w, ada_b):
    bt = c_all.shape[0]
    d = D_MODEL
    return pl.pallas_call(
        _adaln_kernel,
        grid=(DEPTH, 3),
        in_specs=[
            pl.BlockSpec((bt, d), lambda l, j: (0, 0)),
            pl.BlockSpec((None, d, d), lambda l, j: (l, 0, j)),
            pl.BlockSpec((None, 1, d), lambda l, j: (l, 0, j)),
        ],
        out_specs=pl.BlockSpec((None, bt, d), lambda l, j: (l, 0, j)),
        out_shape=jax.ShapeDtypeStruct((DEPTH, bt, 3 * d), F32),
        compiler_params=_params("parallel", "parallel"),
        name="adaln",
    )(c_all, ada_w, ada_b.reshape(DEPTH, 1, 3 * d))


def _modulated_norm(x, g, scale, shift):
    ms = jnp.mean(x * x, axis=-1, keepdims=True)
    y = x * lax.rsqrt(ms + EPS) * g
    return (y * (1.0 + scale) + shift).astype(BF16)


def _head_norm(xs, gain, gm):
    x2 = xs * xs
    hi = x2.astype(BF16)
    lo = (x2 - hi.astype(F32)).astype(BF16)
    ss = jnp.dot(hi, gm, preferred_element_type=F32) + jnp.dot(lo, gm, preferred_element_type=F32)
    return xs * lax.rsqrt(ss * (1.0 / HEAD_DIM) + EPS) * gain


def _rope(xs, cos, sin_signed, first_half, shift):
    up = pltpu.roll(xs, LANES - shift, 1)
    dn = pltpu.roll(xs, shift, 1)
    return xs * cos + jnp.where(first_half, up, dn) * sin_signed


def _inproj_even_kernel(x_ref, sc_ref, sh_ref, g_ref, w_ref, qn_ref, kn_ref,
                        ca_ref, sa_ref, cb_ref, sb_ref, gm_ref, p4_ref, p16_ref,
                        qa_ref, ka_ref, va_ref, gate_ref, *b_refs):
    tm = x_ref.shape[0]
    h = _modulated_norm(x_ref[...], g_ref[...], sc_ref[...], sh_ref[...])
    lane = _lane_iota(tm)
    half_a = (lane & 16) == 0
    half_b = (lane & 8) == 0
    lo = lane < HEAD_DIM
    gm = gm_ref[...]
    ca, sa, cb, sb = ca_ref[...], sa_ref[...], cb_ref[...], sb_ref[...]

    def proj(a, b):
        return jnp.dot(h, w_ref[:, a:b], preferred_element_type=F32)

    def emit_b(val, refs):
        nat_ref, r4_ref, r16_ref = refs
        nat_ref[...] = val
        for perm_ref, o_ref in ((p4_ref, r4_ref), (p16_ref, r16_ref)):
            pv = jnp.dot(perm_ref[...], val, preferred_element_type=F32).astype(BF16)
            o_ref[...] = pv.reshape(o_ref.shape)

    p = proj(0, 512)
    for j in range(4):
        xs = _head_norm(p[:, LANES * j:LANES * (j + 1)], qn_ref[...], gm)
        qa_ref[:, LANES * j:LANES * (j + 1)] = (_rope(xs, ca, sa, half_a, 16) * QK_SCALE).astype(BF16)
    xs = _head_norm(proj(512, 640), kn_ref[...], gm)
    ka_ref[...] = _rope(xs, ca, sa, half_a, 16).astype(BF16)
    v = proj(640, 768)
    va_ref[:, 0:LANES] = jnp.where(lo, v, 1.0).astype(BF16)
    va_ref[:, LANES:2 * LANES] = jnp.where(lo, 1.0, v).astype(BF16)
    gate_ref[:, 0:512] = _silu(proj(768, 1280)).astype(BF16)
    p = proj(1280, 1792)
    qb = [(_rope(p[:, LANES * j:LANES * (j + 1)], cb, sb, half_b, 8) * QK_SCALE).astype(BF16)
          for j in range(4)]
    emit_b(jnp.concatenate(qb, axis=1), b_refs[0:3])
    p = proj(1792, 2304)
    kb = [_rope(p[:, LANES * j:LANES * (j + 1)], cb, sb, half_b, 8).astype(BF16) for j in range(4)]
    emit_b(jnp.concatenate(kb, axis=1), b_refs[3:6])
    emit_b(proj(2304, 2816).astype(BF16), b_refs[6:9])
    gate_ref[:, 512:1024] = _silu(proj(2816, 3328)).astype(BF16)


def _tok_spec(tm, width):
    return pl.BlockSpec((None, tm, width), lambda b, i: (b, i, 0))


def _res_spec(tm, dil, width):
    return pl.BlockSpec((None, None, dil, tm // dil, width), lambda b, i: (b, i, 0, 0, 0))


def _bcast_spec(width):
    return pl.BlockSpec((None, 1, width), lambda b, i: (b, 0, 0))


def _const_spec(shape):
    return pl.BlockSpec(shape, lambda b, i: (0,) * len(shape))


def _residue_perm(tm, dil):
    r = np.arange(tm)
    src = (r % (tm // dil)) * dil + r // (tm // dil)
    return np.asarray(r[None, :] == src[:, None], np.float32)


def _inproj_even(x, scale, shift, pre_g, w_in, qn, kn, tabs, gm):
    bsz, s, d = x.shape
    tm = TM_PROJ
    ca, sa, cb, sb = tabs
    widths = (512, 128, 256, 1024)
    tab_spec = pl.BlockSpec((tm, LANES), lambda b, i: (i, 0))
    b_specs, b_shapes = [], []
    for _ in range(3):
        b_specs.append(_tok_spec(tm, 512))
        b_shapes.append(jax.ShapeDtypeStruct((bsz, s, 512), BF16))
        for dil in B_DILATIONS[1:]:
            b_specs.append(_res_spec(tm, dil, 512))
            b_shapes.append(jax.ShapeDtypeStruct((bsz, s // tm, dil, tm // dil, 512), BF16))
    outs = pl.pallas_call(
        _inproj_even_kernel,
        grid=(bsz, s // tm),
        in_specs=[
            _tok_spec(tm, d), _bcast_spec(d), _bcast_spec(d), _const_spec((1, d)),
            _const_spec((d, AB_IN)), _const_spec((1, LANES)), _const_spec((1, LANES)),
            tab_spec, tab_spec, tab_spec, tab_spec, _const_spec((LANES, LANES)),
            _const_spec((tm, tm)), _const_spec((tm, tm)),
        ],
        out_specs=[_tok_spec(tm, w) for w in widths] + b_specs,
        out_shape=[jax.ShapeDtypeStruct((bsz, s, w), BF16) for w in widths] + b_shapes,
        compiler_params=_params("parallel", "parallel"),
        name="inproj_even",
    )(x, scale, shift, pre_g, w_in, qn, kn, ca, sa, cb, sb, gm,
      jnp.asarray(_residue_perm(tm, 4), BF16), jnp.asarray(_residue_perm(tm, 16), BF16))
    qa, ka, va, gates = outs[:4]
    qkv_b = [outs[4 + 3 * n:7 + 3 * n] for n in range(3)]
    return qa, ka, va, gates, qkv_b


def _inproj_odd_kernel(x_ref, sc_ref, sh_ref, g_ref, w_ref, wc_ref,
                       q_ref, k_ref, v_ref, gate_ref, vd_ref):
    h = _modulated_norm(x_ref[...], g_ref[...], sc_ref[...], sh_ref[...])

    def proj(a, b):
        return jnp.dot(h, w_ref[:, a:b], preferred_element_type=F32)

    q_ref[...] = (proj(0, 768) * QK_SCALE).astype(BF16)
    k_ref[...] = proj(768, 1536).astype(BF16)
    v_ref[...] = proj(1536, 2304).astype(BF16)
    gate_ref[:, 0:768] = _silu(proj(2304, 3072)).astype(BF16)
    u = proj(3072, 3328).astype(BF16)
    vc = jnp.dot(u, wc_ref[...], preferred_element_type=F32)
    vd_ref[0] = vc[:, 0:D_WIDTH].astype(BF16)
    vd_ref[1] = vc[:, D_WIDTH:2 * D_WIDTH].astype(BF16)
    gate_ref[:, 768:1024] = _silu(proj(3328, 3584)).astype(BF16)


def _inproj_odd(x, scale, shift, pre_g, w_in, wc):
    bsz, s, d = x.shape
    tm = TM_PROJ
    widths = (768, 768, 768, 1024)
    return pl.pallas_call(
        _inproj_odd_kernel,
        grid=(bsz, s // tm),
        in_specs=[
            _tok_spec(tm, d), _bcast_spec(d), _bcast_spec(d), _const_spec((1, d)),
            _const_spec((d, CD_IN)), _const_spec((D_WIDTH, 2 * D_WIDTH)),
        ],
        out_specs=[_tok_spec(tm, w) for w in widths]
        + [pl.BlockSpec((None, 2, tm, D_WIDTH), lambda b, i: (b, 0, i, 0))],
        out_shape=[jax.ShapeDtypeStruct((bsz, s, w), BF16) for w in widths]
        + [jax.ShapeDtypeStruct((bsz, 2, s, D_WIDTH), BF16)],
        compiler_params=_params("parallel", "parallel"),
        name="inproj_odd",
    )(x, scale, shift, pre_g, w_in, wc)


def _attn_a_kernel(q_ref, k_ref, v_ref, o_ref, *, tk):
    tq = q_ref.shape[0]
    s_len = k_ref.shape[0]
    g = pl.program_id(1)
    lane_half = _lane_iota(tq) // HEAD_DIM
    on_g = lane_half == g
    q = q_ref[...].astype(F32)
    parts = []
    for hh in range(4):
        slab = q[:, LANES * (hh // 2):LANES * (hh // 2 + 1)]
        both = jnp.where(lane_half == hh % 2, slab, pltpu.roll(slab, HEAD_DIM, 1))
        parts.append(jnp.where(on_g, both, 0.0).astype(BF16))
    qg = jnp.concatenate(parts, axis=0)

    def body(j, carry):
        m, acc = carry
        off = pl.multiple_of(j * tk, tk)
        kc = k_ref[pl.ds(off, tk), :]
        vc = v_ref[pl.ds(off, tk), :]
        s = lax.dot_general(qg, kc, (((1,), (1,)), ((), ())), preferred_element_type=F32)
        m_new = jnp.maximum(m, jnp.max(s, axis=1, keepdims=True))
        p = jnp.exp2(s - m_new).astype(BF16)
        acc = jnp.exp2(m - m_new) * acc + jnp.dot(p, vc, preferred_element_type=F32)
        return m_new, acc

    m0 = jnp.full((4 * tq, 1), NEG, F32)
    acc0 = jnp.zeros((4 * tq, LANES), F32)
    _, acc = lax.fori_loop(0, s_len // tk, body, (m0, acc0))
    o = acc / pltpu.roll(acc, HEAD_DIM, 1)
    lo = lane_half == 0
    for sl in range(2):
        pair = []
        for hh in (2 * sl, 2 * sl + 1):
            oh = o[hh * tq:(hh + 1) * tq, :]
            pair.append(jnp.where(on_g, oh, pltpu.roll(oh, HEAD_DIM, 1)))
        o_ref[:, LANES * sl:LANES * (sl + 1)] = jnp.where(lo, pair[0], pair[1]).astype(BF16)


def _attn_a(q, k, v):
    bsz, s, _ = q.shape
    tq = min(TQ_A, s)
    tk = min(TK_A, s)
    return pl.pallas_call(
        functools.partial(_attn_a_kernel, tk=tk),
        grid=(bsz, 2, s // tq),
        in_specs=[
            pl.BlockSpec((None, tq, 2 * LANES), lambda b, g, i: (b, i, g)),
            pl.BlockSpec((None, s, LANES), lambda b, g, i: (b, 0, 0)),
            pl.BlockSpec((None, s, LANES), lambda b, g, i: (b, 0, g)),
        ],
        out_specs=pl.BlockSpec((None, tq, 2 * LANES), lambda b, g, i: (b, i, g)),
        out_shape=jax.ShapeDtypeStruct((bsz, s, 512), BF16),
        compiler_params=_params("parallel", "arbitrary", "arbitrary"),
        name="attn_a",
    )(q, k, v)


def _window(refs, start, size, cols, lq):
    rpt = refs[0].shape[1]
    pieces = []
    for n, r in enumerate(refs):
        for t in range(r.shape[0]):
            t0 = n * lq + t * rpt
            a, b = max(start, t0), min(start + size, t0 + rpt)
            if a < b:
                pieces.append(r[t, a - t0:b - t0, cols])
    return pieces[0] if len(pieces) == 1 else jnp.concatenate(pieces, axis=0)


def _store_rows(ref, r0, val, cols):
    rpt = ref.shape[1]
    n = val.shape[0]
    for t in range(ref.shape[0]):
        a, b = max(r0, t * rpt), min(r0 + n, (t + 1) * rpt)
        if a < b:
            ref[t, a - t * rpt:b - t * rpt, cols] = val[a - r0:b - r0]


def _attn_b_kernel(q_ref, kp_ref, kc_ref, kn_ref, vp_ref, vc_ref, vn_ref, o_ref, lse_ref, *, l_len):
    lq = q_ref.shape[0] * q_ref.shape[1]
    sub = min(SUB_B, lq)
    base = pl.program_id(2) * lq
    lane = _lane_iota(sub)
    lo = lane < HEAD_DIM
    rq = lax.broadcasted_iota(jnp.int32, (sub, 2 * sub), 0)
    ck = lax.broadcasted_iota(jnp.int32, (sub, 2 * sub), 1)
    delta = ck - rq - sub // 2
    band = jnp.abs(delta) <= B_RADIUS
    for sb in range(lq // sub):
        r0 = sb * sub
        kpos = jnp.where(band, base + (r0 - sub // 2) + ck, -1)
        valid = jnp.logical_and(kpos >= 0, kpos < l_len)
        start = lq + r0 - sub // 2
        lse_tile = jnp.zeros((sub, LANES), F32)
        for j in range(4):
            cols = slice(LANES * j, LANES * (j + 1))
            qs = _window((q_ref,), r0, sub, cols, lq).astype(F32)
            kw = _window((kp_ref, kc_ref, kn_ref), start, 2 * sub, cols, lq)
            vw = _window((vp_ref, vc_ref, vn_ref), start, 2 * sub, cols, lq)
            outs = []
            for half in range(2):
                keep = lo if half == 0 else jnp.logical_not(lo)
                qh = jnp.where(keep, qs, 0.0).astype(BF16)
                s = lax.dot_general(qh, kw, (((1,), (1,)), ((), ())), preferred_element_type=F32)
                s = jnp.where(valid, s, NEG)
                m = jnp.max(s, axis=1, keepdims=True)
                p = jnp.exp2(s - m)
                l = jnp.sum(p, axis=1, keepdims=True)
                o = jnp.dot(p.astype(BF16), vw, preferred_element_type=F32)
                outs.append(o * (1.0 / l))
                lse_tile = lse_tile + jnp.where(lane == 2 * j + half, m + jnp.log2(l), 0.0)
            _store_rows(o_ref, r0, jnp.where(lo, outs[0], outs[1]).astype(BF16), cols)
        packed = jnp.zeros((sub, LANES), F32)
        rest = lse_tile
        for part in range(LSE_PARTS):
            term = rest.astype(BF16).astype(F32)
            rest = rest - term
            packed = packed + (pltpu.roll(term, 8 * part, 1) if part else term)
        _store_rows(lse_ref, r0, packed.astype(BF16), slice(None))


def _attn_b(q, k, v, dil):
    bsz, ntile, _, rpt, w = q.shape
    l_len = ntile * rpt
    lq = min(LQ_B, l_len)
    nt = lq // rpt
    nblk = l_len // lq

    def spec(width, shift):
        def index(b, r, i):
            return (b, jnp.clip(i + shift, 0, nblk - 1), r, 0, 0)
        return pl.BlockSpec((None, nt, None, rpt, width), index)

    cur, prv, nxt = spec(w, 0), spec(w, -1), spec(w, 1)
    return pl.pallas_call(
        functools.partial(_attn_b_kernel, l_len=l_len),
        grid=(bsz, dil, nblk),
        in_specs=[cur, prv, cur, nxt, prv, cur, nxt],
        out_specs=[cur, spec(LANES, 0)],
        out_shape=[jax.ShapeDtypeStruct(q.shape, BF16),
                   jax.ShapeDtypeStruct((bsz, ntile, dil, rpt, LANES), BF16)],
        compiler_params=_params("parallel", "parallel", "parallel"),
        name=f"attn_b_d{dil}",
    )(q, k, k, k, v, v, v)


def _attn_c_kernel(q_ref, kp_ref, kc_ref, kn_ref, vp_ref, vc_ref, vn_ref, bias_ref, o_ref):
    tq = q_ref.shape[0]
    lane = _lane_iota(tq)
    lo = lane < HEAD_DIM
    for pr in range(C_HEADS // 2):
        cols = slice(LANES * pr, LANES * (pr + 1))
        qs = q_ref[:, cols].astype(F32)
        k3 = jnp.concatenate([kp_ref[:, cols], kc_ref[:, cols], kn_ref[:, cols]], axis=0)
        v3 = jnp.concatenate([vp_ref[:, cols], vc_ref[:, cols], vn_ref[:, cols]], axis=0)
        outs = []
        for half in range(2):
            keep = lo if half == 0 else jnp.logical_not(lo)
            qh = jnp.where(keep, qs, 0.0).astype(BF16)
            s = lax.dot_general(qh, k3, (((1,), (1,)), ((), ())), preferred_element_type=F32)
            s = s + bias_ref[2 * pr + half]
            m = jnp.max(s, axis=1, keepdims=True)
            p = jnp.exp2(s - m)
            l = jnp.sum(p, axis=1, keepdims=True)
            o = jnp.dot(p.astype(BF16), v3, preferred_element_type=F32)
            outs.append(o * (1.0 / l))
        o_ref[:, cols] = jnp.where(lo, outs[0], outs[1]).astype(BF16)


def _bias_c(rpb):
    nq, nu = ROWS_C, 3 * ROWS_C
    assert nq == C_WIN_H // 2
    nrel = 2 * C_WIN_H - 1
    scaled = rpb.astype(F32) * LOG2E
    per_col = []
    for c in range(GRID_W):
        c0 = min(max(c - C_WIN_W // 2, 0), GRID_W - C_WIN_W)
        win = scaled[:, :, c0 - c + C_WIN_W - 1:c0 - c + 2 * C_WIN_W - 1]
        per_col.append(jnp.pad(win, ((0, 0), (0, 0), (c0, GRID_W - C_WIN_W - c0)), constant_values=NEG))
    tab = jnp.stack(per_col, axis=2)
    dead = jnp.full((C_HEADS, GRID_W, GRID_W), NEG, F32)
    kinds = []
    for kind in range(3):
        rows = []
        for j in range(nq):
            slots = []
            for u in range(nu):
                rel = (u - nq) - j + C_WIN_H - 1
                if kind == 0:
                    ok = nq <= u < nq + C_WIN_H
                elif kind == 2:
                    ok = 2 * nq - C_WIN_H <= u < 2 * nq
                else:
                    ok = nq - C_WIN_H // 2 <= u - j < nq + C_WIN_H // 2
                slots.append(tab[:, rel] if ok and 0 <= rel < nrel else dead)
            rows.append(jnp.concatenate(slots, axis=2))
        kinds.append(jnp.concatenate(rows, axis=1))
    return jnp.stack(kinds, axis=0)


def _attn_c(q, k, v, bias):
    bsz, s, w = q.shape
    tq = ROWS_C * GRID_W
    nblk = s // tq
    assert nblk >= 2
    kind = lambda i: jnp.where(i == 0, 0, jnp.where(i == nblk - 1, 2, 1))
    cur = pl.BlockSpec((None, tq, w), lambda b, i: (b, i, 0))
    prv = pl.BlockSpec((None, tq, w), lambda b, i: (b, jnp.maximum(i - 1, 0), 0))
    nxt = pl.BlockSpec((None, tq, w), lambda b, i: (b, jnp.minimum(i + 1, nblk - 1), 0))
    return pl.pallas_call(
        _attn_c_kernel,
        grid=(bsz, nblk),
        in_specs=[cur, prv, cur, nxt, prv, cur, nxt,
                  pl.BlockSpec((None, C_HEADS, tq, 3 * tq), lambda b, i: (kind(i), 0, 0, 0))],
        out_specs=cur,
        out_shape=jax.ShapeDtypeStruct((bsz, s, w), BF16),
        compiler_params=_params("parallel", "arbitrary"),
        name="attn_c",
    )(q, k, k, k, v, v, v, bias)


S2_D = 64


def _dft_consts(s):
    s1 = s // S2_D
    th2 = 2 * np.pi * np.outer(np.arange(S2_D), np.arange(S2_D)) / S2_D
    c2, n2 = np.cos(th2) / 8.0, np.sin(th2) / 8.0
    w2big = np.block([[c2, n2], [-n2, c2]])
    th1 = 2 * np.pi * np.outer(np.arange(s1), np.arange(s1)) / s1
    w1cat = np.concatenate([np.cos(th1), np.sin(th1)], axis=1) / math.sqrt(s1)
    tht = 2 * np.pi * np.outer(np.arange(S2_D), np.arange(s1)) / s
    return (jnp.asarray(w2big, BF16), jnp.asarray(w1cat, BF16),
            jnp.asarray(np.cos(tht), F32), jnp.asarray(-np.sin(tht), F32))


def _chan_dft_const():
    th = 2 * np.pi * np.outer(np.arange(D_GROUP_DIM), np.arange(D_GROUP_DIM)) / D_GROUP_DIM
    eye = np.eye(D_GROUPS)
    wc = np.concatenate([np.kron(eye, np.cos(th)), np.kron(eye, -np.sin(th))], axis=1) / 8.0
    return jnp.asarray(wc, BF16)


def _dft1_kernel(v_ref, w_ref, tr_ref, ti_ref, o_ref):
    x = jnp.concatenate([v_ref[0], v_ref[1]], axis=0)
    b = jnp.dot(w_ref[...], x, preferred_element_type=F32)
    br, bi = b[0:S2_D], b[S2_D:2 * S2_D]
    tr, ti = tr_ref[...], ti_ref[...]
    o_ref[0] = (br * tr - bi * ti).astype(BF16)
    o_ref[1] = (br * ti + bi * tr).astype(BF16)


def _dft2_kernel(b_ref, w_ref, o_ref):
    for kk in range(b_ref.shape[1]):
        rhs = jnp.concatenate([b_ref[0, kk], b_ref[1, kk]], axis=0)
        x = jnp.dot(w_ref[...], rhs, preferred_element_type=F32)
        o_ref[:, D_WIDTH * kk:D_WIDTH * (kk + 1)] = x.astype(BF16)


def _fourier(vd, consts):
    bsz, _, s, w = vd.shape
    s1 = s // S2_D
    w2big, w1cat, tr, ti = consts
    ncol = s1 * w
    tn = min(TN_D1, ncol)
    twr = jnp.repeat(tr, w, axis=1)
    twi = jnp.repeat(ti, w, axis=1)
    b1 = pl.pallas_call(
        _dft1_kernel,
        grid=(bsz, ncol // tn),
        in_specs=[
            pl.BlockSpec((None, 2, S2_D, tn), lambda b, j: (b, 0, 0, j)),
            pl.BlockSpec((2 * S2_D, 2 * S2_D), lambda b, j: (0, 0)),
            pl.BlockSpec((S2_D, tn), lambda b, j: (0, j)),
            pl.BlockSpec((S2_D, tn), lambda b, j: (0, j)),
        ],
        out_specs=pl.BlockSpec((None, 2, S2_D, tn), lambda b, j: (b, 0, 0, j)),
        out_shape=jax.ShapeDtypeStruct((bsz, 2, S2_D, ncol), BF16),
        compiler_params=_params("parallel", "parallel"),
        name="dft_stage1",
    )(vd.reshape(bsz, 2, S2_D, ncol), w2big, twr, twi)
    tk2 = TK2_D
    f = pl.pallas_call(
        _dft2_kernel,
        grid=(bsz, S2_D // tk2),
        in_specs=[
            pl.BlockSpec((None, 2, tk2, s1, w), lambda b, j: (b, 0, j, 0, 0)),
            pl.BlockSpec((s1, 2 * s1), lambda b, j: (0, 0)),
        ],
        out_specs=pl.BlockSpec((None, s1, tk2 * w), lambda b, j: (b, 0, j)),
        out_shape=jax.ShapeDtypeStruct((bsz, s1, S2_D * w), BF16),
        compiler_params=_params("parallel", "parallel"),
        name="dft_stage2",
    )(b1.reshape(bsz, 2, S2_D, s1, w), w1cat)
    return f.reshape(bsz, s, w)


def _finish(y, x_ref, gate_ref, g_ref, o_ref):
    ms = jnp.mean(y * y, axis=-1, keepdims=True)
    yn = y * lax.rsqrt(ms + EPS) * g_ref[...]
    o_ref[...] = x_ref[...] + gate_ref[...] * yn


def _outproj_even_kernel(oa_ref, o1_ref, o4_ref, o16_ref, l1_ref, l4_ref, l16_ref, gates_ref,
                         ex_ref, p4_ref, p16_ref, w_ref, x_ref, gate_ref, g_ref, o_ref):
    tm = x_ref.shape[0]

    def natural(ref, perm_ref):
        val = ref[...].reshape(tm, ref.shape[-1])
        return jnp.dot(perm_ref[...], val, preferred_element_type=F32)

    def lse(x):
        out = x
        for part in range(1, LSE_PARTS):
            out = out + pltpu.roll(x, LANES - 8 * part, 1)
        return out

    l1 = lse(l1_ref[...].astype(F32))
    l2 = lse(natural(l4_ref, p4_ref))
    l3 = lse(natural(l16_ref, p16_ref))
    mx = jnp.maximum(jnp.maximum(l1, l2), l3)
    e1, e2, e3 = jnp.exp2(l1 - mx), jnp.exp2(l2 - mx), jnp.exp2(l3 - mx)
    inv = 1.0 / (e1 + e2 + e3)
    ex = ex_ref[...]

    def expand(wgt):
        hi = wgt.astype(BF16)
        lo = (wgt - hi.astype(F32)).astype(BF16)
        return jnp.dot(hi, ex, preferred_element_type=F32) + jnp.dot(lo, ex, preferred_element_type=F32)

    ob = (expand(e1 * inv) * o1_ref[...].astype(F32) + expand(e2 * inv) * natural(o4_ref, p4_ref)
          + expand(e3 * inv) * natural(o16_ref, p16_ref))
    ma = (oa_ref[...].astype(F32) * gates_ref[:, 0:512].astype(F32)).astype(BF16)
    mb = (ob * gates_ref[:, 512:1024].astype(F32)).astype(BF16)
    y = (jnp.dot(ma, w_ref[0:512, :], preferred_element_type=F32)
         + jnp.dot(mb, w_ref[512:1024, :], preferred_element_type=F32))
    _finish(y, x_ref, gate_ref, g_ref, o_ref)


def _outproj_even(oa, obs, lses, gates, ex, w_out, x, gate, post_g):
    bsz, s, d = x.shape
    tm = TM_PROJ
    return pl.pallas_call(
        _outproj_even_kernel,
        grid=(bsz, s // tm),
        in_specs=[_tok_spec(tm, 512), _tok_spec(tm, 512), _res_spec(tm, 4, 512), _res_spec(tm, 16, 512),
                  _tok_spec(tm, LANES), _res_spec(tm, 4, LANES), _res_spec(tm, 16, LANES),
                  _tok_spec(tm, 1024), _const_spec((LANES, 512)),
                  _const_spec((tm, tm)), _const_spec((tm, tm)), _const_spec((d, d)),
                  _tok_spec(tm, d), _bcast_spec(d), _const_spec((1, d))],
        out_specs=_tok_spec(tm, d),
        out_shape=jax.ShapeDtypeStruct((bsz, s, d), F32),
        compiler_params=_params("parallel", "parallel"),
        name="outproj_even",
    )(oa, *obs, *lses, gates, ex,
      jnp.asarray(_residue_perm(tm, 4).T, BF16), jnp.asarray(_residue_perm(tm, 16).T, BF16),
      w_out, x, gate, post_g)


def _outproj_odd_kernel(oc_ref, f_ref, gates_ref, lin_ref, w_ref, x_ref, gate_ref, g_ref, o_ref):
    od = jnp.dot(f_ref[...], lin_ref[...], preferred_element_type=F32)
    mc = (oc_ref[...].astype(F32) * gates_ref[:, 0:768].astype(F32)).astype(BF16)
    md = (od * gates_ref[:, 768:1024].astype(F32)).astype(BF16)
    y = (jnp.dot(mc, w_ref[0:768, :], preferred_element_type=F32)
         + jnp.dot(md, w_ref[768:1024, :], preferred_element_type=F32))
    _finish(y, x_ref, gate_ref, g_ref, o_ref)


def _outproj_odd(oc, f, gates, lin, w_out, x, gate, post_g):
    bsz, s, d = x.shape
    tm = TM_PROJ
    return pl.pallas_call(
        _outproj_odd_kernel,
        grid=(bsz, s // tm),
        in_specs=[_tok_spec(tm, 768), _tok_spec(tm, D_WIDTH), _tok_spec(tm, 1024),
                  _const_spec((D_WIDTH, D_WIDTH)), _const_spec((d, d)),
                  _tok_spec(tm, d), _bcast_spec(d), _const_spec((1, d))],
        out_specs=_tok_spec(tm, d),
        out_shape=jax.ShapeDtypeStruct((bsz, s, d), F32),
        compiler_params=_params("parallel", "parallel"),
        name="outproj_odd",
    )(oc, f, gates, lin, w_out, x, gate, post_g)


def _rope_tables(s):
    t = jnp.arange(s)

    def tab(pos, dim, theta):
        inv = theta ** (-jnp.arange(0, dim, 2, dtype=F32) / dim)
        ang = pos[:, None] * inv[None, :]
        return jnp.cos(ang), jnp.sin(ang)

    cr, sr = tab((t // GRID_W).astype(F32), HEAD_DIM // 2, A_ROPE_THETA)
    cc, sc = tab((t % GRID_W).astype(F32), HEAD_DIM // 2, A_ROPE_THETA)
    cb, sb = tab(t.astype(F32), B_ROPE_DIMS, B_ROPE_THETA)
    rest = HEAD_DIM - B_ROPE_DIMS
    ca = jnp.concatenate([cr, cr, cc, cc] * 2, axis=-1)
    sa = jnp.concatenate([-sr, sr, -sc, sc] * 2, axis=-1)
    cb = jnp.concatenate([cb, cb, jnp.ones((s, rest), F32)] * 2, axis=-1)
    sb = jnp.concatenate([-sb, sb, jnp.zeros((s, rest), F32)] * 2, axis=-1)
    return ca, sa, cb, sb


def _even_layer(x, mod, pre_g, post_g, w_in, w_out, qn, kn, tabs, gm, ex):
    bsz, s, _ = x.shape
    shift, scale, gate = mod
    qa, ka, va, gates, qkv_b = _inproj_even(x, scale, shift, pre_g, w_in, qn, kn, tabs, gm)
    oa = _attn_a(qa, ka, va)
    obs, lses = [], []
    for n, dil in enumerate(B_DILATIONS):
        q, k, v = (t[n] for t in qkv_b)
        if dil == 1:
            lq = min(LQ_B, s)
            q, k, v = (t.reshape(bsz, s // lq, 1, lq, t.shape[-1]) for t in (q, k, v))
        o, lse = _attn_b(q, k, v, dil)
        if dil == 1:
            o, lse = o.reshape(bsz, s, o.shape[-1]), lse.reshape(bsz, s, LANES)
        obs.append(o)
        lses.append(lse)
    return _outproj_even(oa, obs, lses, gates, ex, w_out, x, gate, post_g)


def _odd_layer(x, mod, pre_g, post_g, w_in, w_out, bias, lin, wc, dft):
    shift, scale, gate = mod
    qc, kc, vc, gates, vd = _inproj_odd(x, scale, shift, pre_g, w_in, wc)
    oc = _attn_c(qc, kc, vc, bias)
    f = _fourier(vd, dft)
    return _outproj_odd(oc, f, gates, lin, w_out, x, gate, post_g)


def _trunk(x, mods, pre_g, post_g, w_in_ab, w_out_ab, qn_a, kn_a, w_in_cd, w_out_cd, biases, lin_d):
    bsz, s, d = x.shape
    assert s % TM_PROJ == 0 and s % (16 * SUB_B) == 0
    tabs = _rope_tables(s)
    dft = _dft_consts(s)
    wc = _chan_dft_const()
    lane = np.arange(LANES)
    gm = jnp.asarray(lane[:, None] // HEAD_DIM == lane[None, :] // HEAD_DIM, BF16)
    ex = jnp.asarray(lane[:, None] == np.arange(512)[None, :] // HEAD_DIM, BF16)
    for i in range(DEPTH):
        j = i // 2
        mod = tuple(mods[i][:, None, k * d:(k + 1) * d] for k in range(3))
        pg, qg = pre_g[i][None, :], post_g[i][None, :]
        if i % 2 == 0:
            qn = jnp.tile(qn_a[j], 2)[None, :]
            kn = jnp.tile(kn_a[j], 2)[None, :]
            x = _even_layer(x, mod, pg, qg, w_in_ab[j], w_out_ab[j], qn, kn, tabs, gm, ex)
        else:
            x = _odd_layer(x, mod, pg, qg, w_in_cd[j], w_out_cd[j], biases[j], lin_d[j], wc, dft)
    return x


def kernel(x_prompt, x_sample, c_prompt, c_sample, pre_g, post_g, ada_w, ada_b,
           w_in_ab, w_out_ab, qn_a, kn_a, w_in_cd, w_out_cd, rpb_c, lin_d):
    nb = x_prompt.shape[0]
    mods = _adaln(jnp.concatenate([c_prompt, c_sample], axis=0), ada_w, ada_b)
    biases = [_bias_c(rpb_c[j]) for j in range(rpb_c.shape[0])]
    args = (pre_g, post_g, w_in_ab.astype(BF16), w_out_ab.astype(BF16), qn_a, kn_a,
            w_in_cd.astype(BF16), w_out_cd.astype(BF16), biases, lin_d.astype(BF16))
    y_prompt = _trunk(x_prompt, mods[:, :nb], *args)
    y_sample = _trunk(x_sample, mods[:, nb:], *args)
    return (y_prompt, y_sample)
```

```python
import functools
import math

import numpy as np
import jax
import jax.numpy as jnp
from jax import lax
from jax.experimental import pallas as pl
from jax.experimental.pallas import tpu as pltpu

F32 = jnp.float32
BF16 = jnp.bfloat16

D_MODEL = 1024
DEPTH = 4
HEAD_DIM = 64
GRID_W = 64
A_ROPE_THETA = 10000.0
B_ROPE_THETA = 500000.0
B_ROPE_DIMS = 16
B_DILATIONS = (1, 4, 16)
B_RADIUS = 64
C_HEADS = 12
C_WIN_H = 8
C_WIN_W = 16
D_GROUPS = 4
D_GROUP_DIM = 64
D_WIDTH = D_GROUPS * D_GROUP_DIM
AB_IN = 3328
CD_IN = 3584
EPS = 1e-6
NEG = -1e30
LOG2E = 1.4426950408889634
QK_SCALE = LOG2E * HEAD_DIM ** -0.5

LANES = 128
VMEM_LIMIT = 56 * 1024 * 1024

TM_PROJ = 512
TQ_A = 128
TK_A = 2048
LQ_B = 256
SUB_B = 128
ROWS_C = 4
TN_D1 = 4096
TK2_D = 8
LSE_PARTS = 3


def _params(*sem):
    return pltpu.CompilerParams(dimension_semantics=sem, vmem_limit_bytes=VMEM_LIMIT)


def _silu(x):
    return x / (1.0 + jnp.exp(-x))


def _lane_iota(rows):
    return lax.broadcasted_iota(jnp.int32, (rows, LANES), 1)


def _adaln_kernel(c_ref, w_ref, b_ref, o_ref):
    a = _silu(c_ref[...]).astype(BF16)
    o_ref[...] = jnp.dot(a, w_ref[...].astype(BF16), preferred_element_type=F32) + b_ref[...]


def _adaln(c_all, ada_w, ada_b):
    bt = c_all.shape[0]
    d = D_MODEL
    return pl.pallas_call(
        _adaln_kernel,
        grid=(DEPTH, 3),
        in_specs=[
            pl.BlockSpec((bt, d), lambda l, j: (0, 0)),
            pl.BlockSpec((None, d, d), lambda l, j: (l, 0, j)),
            pl.BlockSpec((None, 1, d), lambda l, j: (l, 0, j)),
        ],
        out_specs=pl.BlockSpec((None, bt, d), lambda l, j: (l, 0, j)),
        out_shape=jax.ShapeDtypeStruct((DEPTH, bt, 3 * d), F32),
        compiler_params=_params("parallel", "parallel"),
        name="adaln",
    )(c_all, ada_w, ada_b.reshape(DEPTH, 1, 3 * d))


def _modulated_norm(x, g, scale, shift):
    ms = jnp.mean(x * x, axis=-1, keepdims=True)
    y = x * lax.rsqrt(ms + EPS) * g
    return (y * (1.0 + scale) + shift).astype(BF16)


def _head_norm(xs, gain, gm):
    x2 = xs * xs
    hi = x2.astype(BF16)
    lo = (x2 - hi.astype(F32)).astype(BF16)
    ss = jnp.dot(hi, gm, preferred_element_type=F32) + jnp.dot(lo, gm, preferred_element_type=F32)
    return xs * lax.rsqrt(ss * (1.0 / HEAD_DIM) + EPS) * gain


def _rope(xs, cos, sin_signed, first_half, shift):
    up = pltpu.roll(xs, LANES - shift, 1)
    dn = pltpu.roll(xs, shift, 1)
    return xs * cos + jnp.where(first_half, up, dn) * sin_signed


def _inproj_even_kernel(x_ref, sc_ref, sh_ref, g_ref, w_ref, qn_ref, kn_ref,
                        ca_ref, sa_ref, cb_ref, sb_ref, gm_ref, p4_ref, p16_ref,
                        qa_ref, ka_ref, va_ref, gate_ref, *b_refs):
    tm = x_ref.shape[0]
    h = _modulated_norm(x_ref[...], g_ref[...], sc_ref[...], sh_ref[...])
    lane = _lane_iota(tm)
    half_a = (lane & 16) == 0
    half_b = (lane & 8) == 0
    lo = lane < HEAD_DIM
    gm = gm_ref[...]
    ca, sa, cb, sb = ca_ref[...], sa_ref[...], cb_ref[...], sb_ref[...]

    def proj(a, b):
        return jnp.dot(h, w_ref[:, a:b], preferred_element_type=F32)

    def emit_b(val, refs):
        nat_ref, r4_ref, r16_ref = refs
        nat_ref[...] = val
        for perm_ref, o_ref in ((p4_ref, r4_ref), (p16_ref, r16_ref)):
            pv = jnp.dot(perm_ref[...], val, preferred_element_type=F32).astype(BF16)
            o_ref[...] = pv.reshape(o_ref.shape)

    p = proj(0, 512)
    lane_half = lane // HEAD_DIM
    for j in range(4):
        xs = _head_norm(p[:, LANES * j:LANES * (j + 1)], qn_ref[...], gm)
        xs = _rope(xs, ca, sa, half_a, 16) * QK_SCALE
        swapped = pltpu.roll(xs, HEAD_DIM, 1)
        for half in range(2):
            hd = 2 * j + half
            g, hh = hd // 4, hd % 4
            both = jnp.where(lane_half == half, xs, swapped)
            ht = jnp.where(lane_half == g, both, 0.0).T.astype(BF16)
            for t in range(tm // TQ_A):
                qa_ref[g, t, :, hh * TQ_A:(hh + 1) * TQ_A] = ht[:, t * TQ_A:(t + 1) * TQ_A]
    xs = _head_norm(proj(512, 640), kn_ref[...], gm)
    ka_ref[...] = _rope(xs, ca, sa, half_a, 16).astype(BF16)
    v = proj(640, 768)
    va_ref[0] = jnp.where(lo, v, 1.0).T.astype(BF16)
    va_ref[1] = jnp.where(lo, 1.0, v).T.astype(BF16)
    gate_ref[:, 0:512] = _silu(proj(768, 1280)).astype(BF16)
    p = proj(1280, 1792)
    qb = [(_rope(p[:, LANES * j:LANES * (j + 1)], cb, sb, half_b, 8) * QK_SCALE).astype(BF16)
          for j in range(4)]
    emit_b(jnp.concatenate(qb, axis=1), b_refs[0:3])
    p = proj(1792, 2304)
    kb = [_rope(p[:, LANES * j:LANES * (j + 1)], cb, sb, half_b, 8).astype(BF16) for j in range(4)]
    emit_b(jnp.concatenate(kb, axis=1), b_refs[3:6])
    emit_b(proj(2304, 2816).astype(BF16), b_refs[6:9])
    gate_ref[:, 512:1024] = _silu(proj(2816, 3328)).astype(BF16)


def _tok_spec(tm, width):
    return pl.BlockSpec((None, tm, width), lambda b, i: (b, i, 0))


def _res_spec(tm, dil, width):
    return pl.BlockSpec((None, None, dil, tm // dil, width), lambda b, i: (b, i, 0, 0, 0))


def _bcast_spec(width):
    return pl.BlockSpec((None, 1, width), lambda b, i: (b, 0, 0))


def _const_spec(shape):
    return pl.BlockSpec(shape, lambda b, i: (0,) * len(shape))


def _residue_perm(tm, dil):
    r = np.arange(tm)
    src = (r % (tm // dil)) * dil + r // (tm // dil)
    return np.asarray(r[None, :] == src[:, None], np.float32)


def _inproj_even(x, scale, shift, pre_g, w_in, qn, kn, tabs, gm):
    bsz, s, d = x.shape
    tm = TM_PROJ
    ca, sa, cb, sb = tabs
    tab_spec = pl.BlockSpec((tm, LANES), lambda b, i: (i, 0))
    b_specs, b_shapes = [], []
    for _ in range(3):
        b_specs.append(_tok_spec(tm, 512))
        b_shapes.append(jax.ShapeDtypeStruct((bsz, s, 512), BF16))
        for dil in B_DILATIONS[1:]:
            b_specs.append(_res_spec(tm, dil, 512))
            b_shapes.append(jax.ShapeDtypeStruct((bsz, s // tm, dil, tm // dil, 512), BF16))
    outs = pl.pallas_call(
        _inproj_even_kernel,
        grid=(bsz, s // tm),
        in_specs=[
            _tok_spec(tm, d), _bcast_spec(d), _bcast_spec(d), _const_spec((1, d)),
            _const_spec((d, AB_IN)), _const_spec((1, LANES)), _const_spec((1, LANES)),
            tab_spec, tab_spec, tab_spec, tab_spec, _const_spec((LANES, LANES)),
            _const_spec((tm, tm)), _const_spec((tm, tm)),
        ],
        out_specs=[pl.BlockSpec((None, 2, tm // TQ_A, LANES, 4 * TQ_A), lambda b, i: (b, 0, i, 0, 0)),
                   _tok_spec(tm, LANES),
                   pl.BlockSpec((None, 2, LANES, tm), lambda b, i: (b, 0, 0, i)),
                   _tok_spec(tm, 1024)] + b_specs,
        out_shape=[jax.ShapeDtypeStruct((bsz, 2, s // TQ_A, LANES, 4 * TQ_A), BF16),
                   jax.ShapeDtypeStruct((bsz, s, LANES), BF16),
                   jax.ShapeDtypeStruct((bsz, 2, LANES, s), BF16),
                   jax.ShapeDtypeStruct((bsz, s, 1024), BF16)] + b_shapes,
        compiler_params=_params("parallel", "parallel"),
        name="inproj_even",
    )(x, scale, shift, pre_g, w_in, qn, kn, ca, sa, cb, sb, gm,
      jnp.asarray(_residue_perm(tm, 4), BF16), jnp.asarray(_residue_perm(tm, 16), BF16))
    qa, ka, va, gates = outs[:4]
    qkv_b = [outs[4 + 3 * n:7 + 3 * n] for n in range(3)]
    return qa, ka, va, gates, qkv_b


def _inproj_odd_kernel(x_ref, sc_ref, sh_ref, g_ref, w_ref, wc_ref,
                       q_ref, k_ref, v_ref, gate_ref, vd_ref):
    h = _modulated_norm(x_ref[...], g_ref[...], sc_ref[...], sh_ref[...])

    def proj(a, b):
        return jnp.dot(h, w_ref[:, a:b], preferred_element_type=F32)

    q_ref[...] = (proj(0, 768) * QK_SCALE).astype(BF16)
    k_ref[...] = proj(768, 1536).astype(BF16)
    v_ref[...] = proj(1536, 2304).astype(BF16)
    gate_ref[:, 0:768] = _silu(proj(2304, 3072)).astype(BF16)
    u = proj(3072, 3328).astype(BF16)
    vc = jnp.dot(u, wc_ref[...], preferred_element_type=F32)
    vd_ref[0] = vc[:, 0:D_WIDTH].astype(BF16)
    vd_ref[1] = vc[:, D_WIDTH:2 * D_WIDTH].astype(BF16)
    gate_ref[:, 768:1024] = _silu(proj(3328, 3584)).astype(BF16)


def _inproj_odd(x, scale, shift, pre_g, w_in, wc):
    bsz, s, d = x.shape
    tm = TM_PROJ
    widths = (768, 768, 768, 1024)
    return pl.pallas_call(
        _inproj_odd_kernel,
        grid=(bsz, s // tm),
        in_specs=[
            _tok_spec(tm, d), _bcast_spec(d), _bcast_spec(d), _const_spec((1, d)),
            _const_spec((d, CD_IN)), _const_spec((D_WIDTH, 2 * D_WIDTH)),
        ],
        out_specs=[_tok_spec(tm, w) for w in widths]
        + [pl.BlockSpec((None, 2, tm, D_WIDTH), lambda b, i: (b, 0, i, 0))],
        out_shape=[jax.ShapeDtypeStruct((bsz, s, w), BF16) for w in widths]
        + [jax.ShapeDtypeStruct((bsz, 2, s, D_WIDTH), BF16)],
        compiler_params=_params("parallel", "parallel"),
        name="inproj_odd",
    )(x, scale, shift, pre_g, w_in, wc)


def _attn_a_kernel(q_ref, k_ref, v_ref, o_ref, s_ref, *, tk):
    tq = o_ref.shape[0]
    s_len = k_ref.shape[0]
    g = pl.program_id(1)
    qt = q_ref[...]

    nchunk = s_len // tk
    m = jnp.full((1, 4 * tq), NEG, F32)
    acc = jnp.zeros((LANES, 4 * tq), F32)
    s_ref[0] = jnp.dot(k_ref[0:tk, :], qt, preferred_element_type=F32)
    for j in range(nchunk):
        if j + 1 < nchunk:
            s_ref[(j + 1) % 2] = jnp.dot(k_ref[(j + 1) * tk:(j + 2) * tk, :], qt, preferred_element_type=F32)
        s = s_ref[j % 2]
        vc = v_ref[:, j * tk:(j + 1) * tk]
        m_new = jnp.maximum(m, jnp.max(s, axis=0, keepdims=True))
        p = jnp.exp2(s - m_new).astype(BF16)
        acc = jnp.exp2(m - m_new) * acc + jnp.dot(vc, p, preferred_element_type=F32)
        m = m_new
    first = g == 0
    num = jnp.where(first, acc[0:HEAD_DIM], acc[HEAD_DIM:LANES])
    den = jnp.where(first, acc[HEAD_DIM:HEAD_DIM + 1], acc[0:1])
    o = num / den
    for sl in range(2):
        pair = jnp.concatenate([o[:, (2 * sl) * tq:(2 * sl + 1) * tq],
                                o[:, (2 * sl + 1) * tq:(2 * sl + 2) * tq]], axis=0)
        o_ref[:, LANES * sl:LANES * (sl + 1)] = pair.T.astype(BF16)


def _attn_a(q, k, v):
    bsz, s, _ = k.shape
    tq = TQ_A
    tk = min(TK_A, s)
    return pl.pallas_call(
        functools.partial(_attn_a_kernel, tk=tk),
        grid=(bsz, 2, s // tq),
        in_specs=[
            pl.BlockSpec((None, None, None, LANES, 4 * tq), lambda b, g, i: (b, g, i, 0, 0)),
            pl.BlockSpec((None, s, LANES), lambda b, g, i: (b, 0, 0)),
            pl.BlockSpec((None, None, LANES, s), lambda b, g, i: (b, g, 0, 0)),
        ],
        out_specs=pl.BlockSpec((None, tq, 2 * LANES), lambda b, g, i: (b, i, g)),
        out_shape=jax.ShapeDtypeStruct((bsz, s, 512), BF16),
        scratch_shapes=[pltpu.VMEM((2, tk, 4 * tq), F32)],
        compiler_params=_params("parallel", "arbitrary", "arbitrary"),
        name="attn_a",
    )(q, k, v)


def _window(refs, start, size, cols, lq):
    rpt = refs[0].shape[1]
    pieces = []
    for n, r in enumerate(refs):
        for t in range(r.shape[0]):
            t0 = n * lq + t * rpt
            a, b = max(start, t0), min(start + size, t0 + rpt)
            if a < b:
                pieces.append(r[t, a - t0:b - t0, cols])
    return pieces[0] if len(pieces) == 1 else jnp.concatenate(pieces, axis=0)


def _store_rows(ref, r0, val, cols):
    rpt = ref.shape[1]
    n = val.shape[0]
    for t in range(ref.shape[0]):
        a, b = max(r0, t * rpt), min(r0 + n, (t + 1) * rpt)
        if a < b:
            ref[t, a - t * rpt:b - t * rpt, cols] = val[a - r0:b - r0]


def _attn_b_kernel(q_ref, kp_ref, kc_ref, kn_ref, vp_ref, vc_ref, vn_ref, o_ref, lse_ref, *, l_len):
    lq = q_ref.shape[0] * q_ref.shape[1]
    sub = min(SUB_B, lq)
    base = pl.program_id(2) * lq
    lane = _lane_iota(sub)
    lo = lane < HEAD_DIM
    rq = lax.broadcasted_iota(jnp.int32, (sub, 2 * sub), 0)
    ck = lax.broadcasted_iota(jnp.int32, (sub, 2 * sub), 1)
    delta = ck - rq - sub // 2
    band = jnp.abs(delta) <= B_RADIUS
    for sb in range(lq // sub):
        r0 = sb * sub
        kpos = jnp.where(band, base + (r0 - sub // 2) + ck, -1)
        mask = jnp.where(jnp.logical_and(kpos >= 0, kpos < l_len), 0.0, NEG)
        mask2 = jnp.concatenate([mask, mask], axis=0)
        start = lq + r0 - sub // 2
        lse_tile = jnp.zeros((sub, LANES), F32)
        for j in range(4):
            cols = slice(LANES * j, LANES * (j + 1))
            qs = _window((q_ref,), r0, sub, cols, lq).astype(F32)
            kw = _window((kp_ref, kc_ref, kn_ref), start, 2 * sub, cols, lq)
            vw = _window((vp_ref, vc_ref, vn_ref), start, 2 * sub, cols, lq)
            q2 = jnp.concatenate([jnp.where(lo, qs, 0.0), jnp.where(lo, 0.0, qs)], axis=0).astype(BF16)
            s = lax.dot_general(q2, kw, (((1,), (1,)), ((), ())), preferred_element_type=F32) + mask2
            m = jnp.max(s, axis=1, keepdims=True)
            p = jnp.exp2(s - m)
            l = jnp.sum(p, axis=1, keepdims=True)
            o = jnp.dot(p.astype(BF16), vw, preferred_element_type=F32) * (1.0 / l)
            lse = m + jnp.log2(l)
            lse_tile = (lse_tile + jnp.where(lane == 2 * j, lse[0:sub], 0.0)
                        + jnp.where(lane == 2 * j + 1, lse[sub:2 * sub], 0.0))
            _store_rows(o_ref, r0, jnp.where(lo, o[0:sub], o[sub:2 * sub]).astype(BF16), cols)
        packed = jnp.zeros((sub, LANES), F32)
        rest = lse_tile
        for part in range(LSE_PARTS):
            term = rest.astype(BF16).astype(F32)
            rest = rest - term
            packed = packed + (pltpu.roll(term, 8 * part, 1) if part else term)
        _store_rows(lse_ref, r0, packed.astype(BF16), slice(None))


def _attn_b(q, k, v, dil):
    bsz, ntile, _, rpt, w = q.shape
    l_len = ntile * rpt
    lq = min(LQ_B, l_len)
    nt = lq // rpt
    nblk = l_len // lq

    def spec(width, shift):
        def index(b, r, i):
            return (b, jnp.clip(i + shift, 0, nblk - 1), r, 0, 0)
        return pl.BlockSpec((None, nt, None, rpt, width), index)

    cur, prv, nxt = spec(w, 0), spec(w, -1), spec(w, 1)
    return pl.pallas_call(
        functools.partial(_attn_b_kernel, l_len=l_len),
        grid=(bsz, dil, nblk),
        in_specs=[cur, prv, cur, nxt, prv, cur, nxt],
        out_specs=[cur, spec(LANES, 0)],
        out_shape=[jax.ShapeDtypeStruct(q.shape, BF16),
                   jax.ShapeDtypeStruct((bsz, ntile, dil, rpt, LANES), BF16)],
        compiler_params=_params("parallel", "parallel", "parallel"),
        name=f"attn_b_d{dil}",
    )(q, k, k, k, v, v, v)


def _attn_c_kernel(q_ref, kp_ref, kc_ref, kn_ref, vp_ref, vc_ref, vn_ref, bias_ref, o_ref):
    tq = q_ref.shape[0]
    lane = _lane_iota(tq)
    lo = lane < HEAD_DIM
    for pr in range(C_HEADS // 2):
        cols = slice(LANES * pr, LANES * (pr + 1))
        qs = q_ref[:, cols].astype(F32)
        k3 = jnp.concatenate([kp_ref[:, cols], kc_ref[:, cols], kn_ref[:, cols]], axis=0)
        v3 = jnp.concatenate([vp_ref[:, cols], vc_ref[:, cols], vn_ref[:, cols]], axis=0)
        q2 = jnp.concatenate([jnp.where(lo, qs, 0.0), jnp.where(lo, 0.0, qs)], axis=0).astype(BF16)
        s = lax.dot_general(q2, k3, (((1,), (1,)), ((), ())), preferred_element_type=F32)
        s = s + bias_ref[2 * pr:2 * pr + 2].reshape(2 * tq, 3 * tq)
        m = jnp.max(s, axis=1, keepdims=True)
        p = jnp.exp2(s - m)
        l = jnp.sum(p, axis=1, keepdims=True)
        o = jnp.dot(p.astype(BF16), v3, preferred_element_type=F32) * (1.0 / l)
        o_ref[:, cols] = jnp.where(lo, o[0:tq], o[tq:2 * tq]).astype(BF16)


def _bias_c(rpb):
    nq, nu = ROWS_C, 3 * ROWS_C
    assert nq == C_WIN_H // 2
    nrel = 2 * C_WIN_H - 1
    scaled = rpb.astype(F32) * LOG2E
    per_col = []
    for c in range(GRID_W):
        c0 = min(max(c - C_WIN_W // 2, 0), GRID_W - C_WIN_W)
        win = scaled[:, :, c0 - c + C_WIN_W - 1:c0 - c + 2 * C_WIN_W - 1]
        per_col.append(jnp.pad(win, ((0, 0), (0, 0), (c0, GRID_W - C_WIN_W - c0)), constant_values=NEG))
    tab = jnp.stack(per_col, axis=2)
    dead = jnp.full((C_HEADS, GRID_W, GRID_W), NEG, F32)
    kinds = []
    for kind in range(3):
        rows = []
        for j in range(nq):
            slots = []
            for u in range(nu):
                rel = (u - nq) - j + C_WIN_H - 1
                if kind == 0:
                    ok = nq <= u < nq + C_WIN_H
                elif kind == 2:
                    ok = 2 * nq - C_WIN_H <= u < 2 * nq
                else:
                    ok = nq - C_WIN_H // 2 <= u - j < nq + C_WIN_H // 2
                slots.append(tab[:, rel] if ok and 0 <= rel < nrel else dead)
            rows.append(jnp.concatenate(slots, axis=2))
        kinds.append(jnp.concatenate(rows, axis=1))
    return jnp.stack(kinds, axis=0)


def _attn_c(q, k, v, bias):
    bsz, s, w = q.shape
    tq = ROWS_C * GRID_W
    nblk = s // tq
    assert nblk >= 2
    kind = lambda i: jnp.where(i == 0, 0, jnp.where(i == nblk - 1, 2, 1))
    cur = pl.BlockSpec((None, tq, w), lambda b, i: (b, i, 0))
    prv = pl.BlockSpec((None, tq, w), lambda b, i: (b, jnp.maximum(i - 1, 0), 0))
    nxt = pl.BlockSpec((None, tq, w), lambda b, i: (b, jnp.minimum(i + 1, nblk - 1), 0))
    return pl.pallas_call(
        _attn_c_kernel,
        grid=(bsz, nblk),
        in_specs=[cur, prv, cur, nxt, prv, cur, nxt,
                  pl.BlockSpec((None, C_HEADS, tq, 3 * tq), lambda b, i: (kind(i), 0, 0, 0))],
        out_specs=cur,
        out_shape=jax.ShapeDtypeStruct((bsz, s, w), BF16),
        compiler_params=_params("parallel", "arbitrary"),
        name="attn_c",
    )(q, k, k, k, v, v, v, bias)


S2_D = 64


def _dft_consts(s):
    s1 = s // S2_D
    th2 = 2 * np.pi * np.outer(np.arange(S2_D), np.arange(S2_D)) / S2_D
    c2, n2 = np.cos(th2) / 8.0, np.sin(th2) / 8.0
    w2big = np.block([[c2, n2], [-n2, c2]])
    th1 = 2 * np.pi * np.outer(np.arange(s1), np.arange(s1)) / s1
    w1cat = np.concatenate([np.cos(th1), np.sin(th1)], axis=1) / math.sqrt(s1)
    tht = 2 * np.pi * np.outer(np.arange(S2_D), np.arange(s1)) / s
    return (jnp.asarray(w2big, BF16), jnp.asarray(w1cat, BF16),
            jnp.asarray(np.cos(tht), F32), jnp.asarray(-np.sin(tht), F32))


def _chan_dft_const():
    th = 2 * np.pi * np.outer(np.arange(D_GROUP_DIM), np.arange(D_GROUP_DIM)) / D_GROUP_DIM
    eye = np.eye(D_GROUPS)
    wc = np.concatenate([np.kron(eye, np.cos(th)), np.kron(eye, -np.sin(th))], axis=1) / 8.0
    return jnp.asarray(wc, BF16)


def _dft1_kernel(v_ref, w_ref, tr_ref, ti_ref, o_ref):
    x = jnp.concatenate([v_ref[0], v_ref[1]], axis=0)
    b = jnp.dot(w_ref[...], x, preferred_element_type=F32)
    br, bi = b[0:S2_D], b[S2_D:2 * S2_D]
    tr, ti = tr_ref[...], ti_ref[...]
    o_ref[0] = (br * tr - bi * ti).astype(BF16)
    o_ref[1] = (br * ti + bi * tr).astype(BF16)


def _dft2_kernel(b_ref, w_ref, o_ref):
    for kk in range(b_ref.shape[1]):
        rhs = jnp.concatenate([b_ref[0, kk], b_ref[1, kk]], axis=0)
        x = jnp.dot(w_ref[...], rhs, preferred_element_type=F32)
        o_ref[:, D_WIDTH * kk:D_WIDTH * (kk + 1)] = x.astype(BF16)


def _fourier(vd, consts):
    bsz, _, s, w = vd.shape
    s1 = s // S2_D
    w2big, w1cat, tr, ti = consts
    ncol = s1 * w
    tn = min(TN_D1, ncol)
    twr = jnp.repeat(tr, w, axis=1)
    twi = jnp.repeat(ti, w, axis=1)
    b1 = pl.pallas_call(
        _dft1_kernel,
        grid=(bsz, ncol // tn),
        in_specs=[
            pl.BlockSpec((None, 2, S2_D, tn), lambda b, j: (b, 0, 0, j)),
            pl.BlockSpec((2 * S2_D, 2 * S2_D), lambda b, j: (0, 0)),
            pl.BlockSpec((S2_D, tn), lambda b, j: (0, j)),
            pl.BlockSpec((S2_D, tn), lambda b, j: (0, j)),
        ],
        out_specs=pl.BlockSpec((None, 2, S2_D, tn), lambda b, j: (b, 0, 0, j)),
        out_shape=jax.ShapeDtypeStruct((bsz, 2, S2_D, ncol), BF16),
        compiler_params=_params("parallel", "parallel"),
        name="dft_stage1",
    )(vd.reshape(bsz, 2, S2_D, ncol), w2big, twr, twi)
    tk2 = TK2_D
    f = pl.pallas_call(
        _dft2_kernel,
        grid=(bsz, S2_D // tk2),
        in_specs=[
            pl.BlockSpec((None, 2, tk2, s1, w), lambda b, j: (b, 0, j, 0, 0)),
            pl.BlockSpec((s1, 2 * s1), lambda b, j: (0, 0)),
        ],
        out_specs=pl.BlockSpec((None, s1, tk2 * w), lambda b, j: (b, 0, j)),
        out_shape=jax.ShapeDtypeStruct((bsz, s1, S2_D * w), BF16),
        compiler_params=_params("parallel", "parallel"),
        name="dft_stage2",
    )(b1.reshape(bsz, 2, S2_D, s1, w), w1cat)
    return f.reshape(bsz, s, w)


def _finish(y, x_ref, gate_ref, g_ref, o_ref):
    ms = jnp.mean(y * y, axis=-1, keepdims=True)
    yn = y * lax.rsqrt(ms + EPS) * g_ref[...]
    o_ref[...] = x_ref[...] + gate_ref[...] * yn


def _outproj_even_kernel(oa_ref, o1_ref, o4_ref, o16_ref, l1_ref, l4_ref, l16_ref, gates_ref,
                         ex_ref, p4_ref, p16_ref, w_ref, x_ref, gate_ref, g_ref, o_ref):
    tm = x_ref.shape[0]

    def natural(ref, perm_ref):
        val = ref[...].reshape(tm, ref.shape[-1])
        return jnp.dot(perm_ref[...], val, preferred_element_type=F32)

    def lse(x):
        out = x
        for part in range(1, LSE_PARTS):
            out = out + pltpu.roll(x, LANES - 8 * part, 1)
        return out

    l1 = lse(l1_ref[...].astype(F32))
    l2 = lse(natural(l4_ref, p4_ref))
    l3 = lse(natural(l16_ref, p16_ref))
    mx = jnp.maximum(jnp.maximum(l1, l2), l3)
    e1, e2, e3 = jnp.exp2(l1 - mx), jnp.exp2(l2 - mx), jnp.exp2(l3 - mx)
    inv = 1.0 / (e1 + e2 + e3)
    ex = ex_ref[...]

    def expand(wgt):
        hi = wgt.astype(BF16)
        lo = (wgt - hi.astype(F32)).astype(BF16)
        return jnp.dot(hi, ex, preferred_element_type=F32) + jnp.dot(lo, ex, preferred_element_type=F32)

    ob = (expand(e1 * inv) * o1_ref[...].astype(F32) + expand(e2 * inv) * natural(o4_ref, p4_ref)
          + expand(e3 * inv) * natural(o16_ref, p16_ref))
    ma = (oa_ref[...].astype(F32) * gates_ref[:, 0:512].astype(F32)).astype(BF16)
    mb = (ob * gates_ref[:, 512:1024].astype(F32)).astype(BF16)
    y = (jnp.dot(ma, w_ref[0:512, :], preferred_element_type=F32)
         + jnp.dot(mb, w_ref[512:1024, :], preferred_element_type=F32))
    _finish(y, x_ref, gate_ref, g_ref, o_ref)


def _outproj_even(oa, obs, lses, gates, ex, w_out, x, gate, post_g):
    bsz, s, d = x.shape
    tm = TM_PROJ
    return pl.pallas_call(
        _outproj_even_kernel,
        grid=(bsz, s // tm),
        in_specs=[_tok_spec(tm, 512), _tok_spec(tm, 512), _res_spec(tm, 4, 512), _res_spec(tm, 16, 512),
                  _tok_spec(tm, LANES), _res_spec(tm, 4, LANES), _res_spec(tm, 16, LANES),
                  _tok_spec(tm, 1024), _const_spec((LANES, 512)),
                  _const_spec((tm, tm)), _const_spec((tm, tm)), _const_spec((d, d)),
                  _tok_spec(tm, d), _bcast_spec(d), _const_spec((1, d))],
        out_specs=_tok_spec(tm, d),
        out_shape=jax.ShapeDtypeStruct((bsz, s, d), F32),
        compiler_params=_params("parallel", "parallel"),
        name="outproj_even",
    )(oa, *obs, *lses, gates, ex,
      jnp.asarray(_residue_perm(tm, 4).T, BF16), jnp.asarray(_residue_perm(tm, 16).T, BF16),
      w_out, x, gate, post_g)


def _outproj_odd_kernel(oc_ref, f_ref, gates_ref, lin_ref, w_ref, x_ref, gate_ref, g_ref, o_ref):
    od = jnp.dot(f_ref[...], lin_ref[...], preferred_element_type=F32)
    mc = (oc_ref[...].astype(F32) * gates_ref[:, 0:768].astype(F32)).astype(BF16)
    md = (od * gates_ref[:, 768:1024].astype(F32)).astype(BF16)
    y = (jnp.dot(mc, w_ref[0:768, :], preferred_element_type=F32)
         + jnp.dot(md, w_ref[768:1024, :], preferred_element_type=F32))
    _finish(y, x_ref, gate_ref, g_ref, o_ref)


def _outproj_odd(oc, f, gates, lin, w_out, x, gate, post_g):
    bsz, s, d = x.shape
    tm = TM_PROJ
    return pl.pallas_call(
        _outproj_odd_kernel,
        grid=(bsz, s // tm),
        in_specs=[_tok_spec(tm, 768), _tok_spec(tm, D_WIDTH), _tok_spec(tm, 1024),
                  _const_spec((D_WIDTH, D_WIDTH)), _const_spec((d, d)),
                  _tok_spec(tm, d), _bcast_spec(d), _const_spec((1, d))],
        out_specs=_tok_spec(tm, d),
        out_shape=jax.ShapeDtypeStruct((bsz, s, d), F32),
        compiler_params=_params("parallel", "parallel"),
        name="outproj_odd",
    )(oc, f, gates, lin, w_out, x, gate, post_g)


def _rope_tables(s):
    t = jnp.arange(s)

    def tab(pos, dim, theta):
        inv = theta ** (-jnp.arange(0, dim, 2, dtype=F32) / dim)
        ang = pos[:, None] * inv[None, :]
        return jnp.cos(ang), jnp.sin(ang)

    cr, sr = tab((t // GRID_W).astype(F32), HEAD_DIM // 2, A_ROPE_THETA)
    cc, sc = tab((t % GRID_W).astype(F32), HEAD_DIM // 2, A_ROPE_THETA)
    cb, sb = tab(t.astype(F32), B_ROPE_DIMS, B_ROPE_THETA)
    rest = HEAD_DIM - B_ROPE_DIMS
    ca = jnp.concatenate([cr, cr, cc, cc] * 2, axis=-1)
    sa = jnp.concatenate([-sr, sr, -sc, sc] * 2, axis=-1)
    cb = jnp.concatenate([cb, cb, jnp.ones((s, rest), F32)] * 2, axis=-1)
    sb = jnp.concatenate([-sb, sb, jnp.zeros((s, rest), F32)] * 2, axis=-1)
    return ca, sa, cb, sb


def _even_layer(x, mod, pre_g, post_g, w_in, w_out, qn, kn, tabs, gm, ex):
    bsz, s, _ = x.shape
    shift, scale, gate = mod
    qa, ka, va, gates, qkv_b = _inproj_even(x, scale, shift, pre_g, w_in, qn, kn, tabs, gm)
    oa = _attn_a(qa, ka, va)
    obs, lses = [], []
    for n, dil in enumerate(B_DILATIONS):
        q, k, v = (t[n] for t in qkv_b)
        if dil == 1:
            lq = min(LQ_B, s)
            q, k, v = (t.reshape(bsz, s // lq, 1, lq, t.shape[-1]) for t in (q, k, v))
        o, lse = _attn_b(q, k, v, dil)
        if dil == 1:
            o, lse = o.reshape(bsz, s, o.shape[-1]), lse.reshape(bsz, s, LANES)
        obs.append(o)
        lses.append(lse)
    return _outproj_even(oa, obs, lses, gates, ex, w_out, x, gate, post_g)


def _odd_layer(x, mod, pre_g, post_g, w_in, w_out, bias, lin, wc, dft):
    shift, scale, gate = mod
    qc, kc, vc, gates, vd = _inproj_odd(x, scale, shift, pre_g, w_in, wc)
    oc = _attn_c(qc, kc, vc, bias)
    f = _fourier(vd, dft)
    return _outproj_odd(oc, f, gates, lin, w_out, x, gate, post_g)


def _trunk(x, mods, pre_g, post_g, w_in_ab, w_out_ab, qn_a, kn_a, w_in_cd, w_out_cd, biases, lin_d):
    bsz, s, d = x.shape
    assert s % TM_PROJ == 0 and s % (16 * SUB_B) == 0
    tabs = _rope_tables(s)
    dft = _dft_consts(s)
    wc = _chan_dft_const()
    lane = np.arange(LANES)
    gm = jnp.asarray(lane[:, None] // HEAD_DIM == lane[None, :] // HEAD_DIM, BF16)
    ex = jnp.asarray(lane[:, None] == np.arange(512)[None, :] // HEAD_DIM, BF16)
    for i in range(DEPTH):
        j = i // 2
        mod = tuple(mods[i][:, None, k * d:(k + 1) * d] for k in range(3))
        pg, qg = pre_g[i][None, :], post_g[i][None, :]
        if i % 2 == 0:
            qn = jnp.tile(qn_a[j], 2)[None, :]
            kn = jnp.tile(kn_a[j], 2)[None, :]
            x = _even_layer(x, mod, pg, qg, w_in_ab[j], w_out_ab[j], qn, kn, tabs, gm, ex)
        else:
            x = _odd_layer(x, mod, pg, qg, w_in_cd[j], w_out_cd[j], biases[j], lin_d[j], wc, dft)
    return x


def kernel(x_prompt, x_sample, c_prompt, c_sample, pre_g, post_g, ada_w, ada_b,
           w_in_ab, w_out_ab, qn_a, kn_a, w_in_cd, w_out_cd, rpb_c, lin_d):
    nb = x_prompt.shape[0]
    mods = _adaln(jnp.concatenate([c_prompt, c_sample], axis=0), ada_w, ada_b)
    biases = [_bias_c(rpb_c[j]) for j in range(rpb_c.shape[0])]
    args = (pre_g, post_g, w_in_ab.astype(BF16), w_out_ab.astype(BF16), qn_a, kn_a,
            w_in_cd.astype(BF16), w_out_cd.astype(BF16), biases, lin_d.astype(BF16))
    y_prompt = _trunk(x_prompt, mods[:, :nb], *args)
    y_sample = _trunk(x_sample, mods[:, nb:], *args)
    return (y_prompt, y_sample)
```

```python
import functools
import math

import numpy as np
import jax
import jax.numpy as jnp
from jax import lax
from jax.experimental import pallas as pl
from jax.experimental.pallas import tpu as pltpu

F32 = jnp.float32
BF16 = jnp.bfloat16

D_MODEL = 1024
DEPTH = 4
HEAD_DIM = 64
GRID_W = 64
A_ROPE_THETA = 10000.0
B_ROPE_THETA = 500000.0
B_ROPE_DIMS = 16
B_DILATIONS = (1, 4, 16)
B_RADIUS = 64
C_HEADS = 12
C_WIN_H = 8
C_WIN_W = 16
D_GROUPS = 4
D_GROUP_DIM = 64
D_WIDTH = D_GROUPS * D_GROUP_DIM
AB_IN = 3328
CD_IN = 3584
EPS = 1e-6
NEG = -1e30
LOG2E = 1.4426950408889634
QK_SCALE = LOG2E * HEAD_DIM ** -0.5

LANES = 128
VMEM_LIMIT = 56 * 1024 * 1024

TM_PROJ = 512
T_RES = 256
TQ_A = 256
TK_A = 2048
LQ_B = 256
SUB_B = 128
ROWS_C = 4
TN_D1 = 4096
TK2_D = 8
LSE_PARTS = 3


def _params(*sem):
    return pltpu.CompilerParams(dimension_semantics=sem, vmem_limit_bytes=VMEM_LIMIT)


def _silu(x):
    return x / (1.0 + jnp.exp(-x))


def _lane_iota(rows):
    return lax.broadcasted_iota(jnp.int32, (rows, LANES), 1)


def _adaln_kernel(c_ref, w_ref, b_ref, o_ref):
    a = _silu(c_ref[...]).astype(BF16)
    o_ref[...] = jnp.dot(a, w_ref[...].astype(BF16), preferred_element_type=F32) + b_ref[...]


def _adaln(c_all, ada_w, ada_b):
    bt = c_all.shape[0]
    d = D_MODEL
    return pl.pallas_call(
        _adaln_kernel,
        grid=(DEPTH, 3),
        in_specs=[
            pl.BlockSpec((bt, d), lambda l, j: (0, 0)),
            pl.BlockSpec((None, d, d), lambda l, j: (l, 0, j)),
            pl.BlockSpec((None, 1, d), lambda l, j: (l, 0, j)),
        ],
        out_specs=pl.BlockSpec((None, bt, d), lambda l, j: (l, 0, j)),
        out_shape=jax.ShapeDtypeStruct((DEPTH, bt, 3 * d), F32),
        compiler_params=_params("parallel", "parallel"),
        name="adaln",
    )(c_all, ada_w, ada_b.reshape(DEPTH, 1, 3 * d))


def _modulated_norm(x, g, scale, shift):
    ms = jnp.mean(x * x, axis=-1, keepdims=True)
    y = x * lax.rsqrt(ms + EPS) * g
    return (y * (1.0 + scale) + shift).astype(BF16)


def _head_norm(xs, gain, lo):
    x2 = xs * xs
    s_lo = jnp.sum(jnp.where(lo, x2, 0.0), axis=1, keepdims=True)
    s_hi = jnp.sum(jnp.where(lo, 0.0, x2), axis=1, keepdims=True)
    ss = jnp.where(lo, s_lo, s_hi)
    return xs * lax.rsqrt(ss * (1.0 / HEAD_DIM) + EPS) * gain


def _rope(xs, cos, sin_signed, first_half, shift):
    up = pltpu.roll(xs, LANES - shift, 1)
    dn = pltpu.roll(xs, shift, 1)
    return xs * cos + jnp.where(first_half, up, dn) * sin_signed


def _inproj_even_kernel(x_ref, sc_ref, sh_ref, g_ref, w_ref, qn_ref, kn_ref,
                        ca_ref, sa_ref, cb_ref, sb_ref, p4_ref, p16_ref,
                        qa_ref, ka_ref, va_ref, gate_ref, *b_refs):
    tm = x_ref.shape[0]
    h = _modulated_norm(x_ref[...], g_ref[...], sc_ref[...], sh_ref[...])
    lane = _lane_iota(tm)
    half_a = (lane & 16) == 0
    half_b = (lane & 8) == 0
    lo = lane < HEAD_DIM
    ca, sa, cb, sb = ca_ref[...], sa_ref[...], cb_ref[...], sb_ref[...]

    def proj(a, b):
        return jnp.dot(h, w_ref[:, a:b], preferred_element_type=F32)

    def emit_b(val, refs):
        nat_ref, r4_ref, r16_ref = refs
        nat_ref[...] = val
        for perm_ref, o_ref in ((p4_ref, r4_ref), (p16_ref, r16_ref)):
            for t in range(tm // T_RES):
                pv = jnp.dot(perm_ref[...], val[t * T_RES:(t + 1) * T_RES], preferred_element_type=F32)
                o_ref[t] = pv.astype(BF16).reshape(o_ref.shape[1:])

    p = proj(0, 512)
    lane_half = lane // HEAD_DIM
    for j in range(4):
        xs = _head_norm(p[:, LANES * j:LANES * (j + 1)], qn_ref[...], lo)
        xs = _rope(xs, ca, sa, half_a, 16) * QK_SCALE
        swapped = pltpu.roll(xs, HEAD_DIM, 1)
        for half in range(2):
            hd = 2 * j + half
            g, hh = hd // 4, hd % 4
            both = jnp.where(lane_half == half, xs, swapped)
            ht = jnp.where(lane_half == g, both, 0.0).T.astype(BF16)
            for t in range(tm // TQ_A):
                qa_ref[g, t, :, hh * TQ_A:(hh + 1) * TQ_A] = ht[:, t * TQ_A:(t + 1) * TQ_A]
    xs = _head_norm(proj(512, 640), kn_ref[...], lo)
    ka_ref[...] = _rope(xs, ca, sa, half_a, 16).astype(BF16)
    v = proj(640, 768)
    va_ref[0] = jnp.where(lo, v, 1.0).T.astype(BF16)
    va_ref[1] = jnp.where(lo, 1.0, v).T.astype(BF16)
    gate_ref[:, 0:512] = _silu(proj(768, 1280)).astype(BF16)
    p = proj(1280, 1792)
    qb = [(_rope(p[:, LANES * j:LANES * (j + 1)], cb, sb, half_b, 8) * QK_SCALE).astype(BF16)
          for j in range(4)]
    emit_b(jnp.concatenate(qb, axis=1), b_refs[0:3])
    p = proj(1792, 2304)
    kb = [_rope(p[:, LANES * j:LANES * (j + 1)], cb, sb, half_b, 8).astype(BF16) for j in range(4)]
    emit_b(jnp.concatenate(kb, axis=1), b_refs[3:6])
    emit_b(proj(2304, 2816).astype(BF16), b_refs[6:9])
    gate_ref[:, 512:1024] = _silu(proj(2816, 3328)).astype(BF16)


def _tok_spec(tm, width):
    return pl.BlockSpec((None, tm, width), lambda b, i: (b, i, 0))


def _res_spec(tm, dil, width):
    return pl.BlockSpec((None, tm // T_RES, dil, T_RES // dil, width), lambda b, i: (b, i, 0, 0, 0))


def _bcast_spec(width):
    return pl.BlockSpec((None, 1, width), lambda b, i: (b, 0, 0))


def _const_spec(shape):
    return pl.BlockSpec(shape, lambda b, i: (0,) * len(shape))


def _residue_perm(tm, dil):
    r = np.arange(tm)
    src = (r % (tm // dil)) * dil + r // (tm // dil)
    return np.asarray(r[None, :] == src[:, None], np.float32)


def _inproj_even(x, scale, shift, pre_g, w_in, qn, kn, tabs):
    bsz, s, d = x.shape
    tm = TM_PROJ
    ca, sa, cb, sb = tabs
    tab_spec = pl.BlockSpec((tm, LANES), lambda b, i: (i, 0))
    b_specs, b_shapes = [], []
    for _ in range(3):
        b_specs.append(_tok_spec(tm, 512))
        b_shapes.append(jax.ShapeDtypeStruct((bsz, s, 512), BF16))
        for dil in B_DILATIONS[1:]:
            b_specs.append(_res_spec(tm, dil, 512))
            b_shapes.append(jax.ShapeDtypeStruct((bsz, s // T_RES, dil, T_RES // dil, 512), BF16))
    outs = pl.pallas_call(
        _inproj_even_kernel,
        grid=(bsz, s // tm),
        in_specs=[
            _tok_spec(tm, d), _bcast_spec(d), _bcast_spec(d), _const_spec((1, d)),
            _const_spec((d, AB_IN)), _const_spec((1, LANES)), _const_spec((1, LANES)),
            tab_spec, tab_spec, tab_spec, tab_spec,
            _const_spec((T_RES, T_RES)), _const_spec((T_RES, T_RES)),
        ],
        out_specs=[pl.BlockSpec((None, 2, tm // TQ_A, LANES, 4 * TQ_A), lambda b, i: (b, 0, i, 0, 0)),
                   _tok_spec(tm, LANES),
                   pl.BlockSpec((None, 2, LANES, tm), lambda b, i: (b, 0, 0, i)),
                   _tok_spec(tm, 1024)] + b_specs,
        out_shape=[jax.ShapeDtypeStruct((bsz, 2, s // TQ_A, LANES, 4 * TQ_A), BF16),
                   jax.ShapeDtypeStruct((bsz, s, LANES), BF16),
                   jax.ShapeDtypeStruct((bsz, 2, LANES, s), BF16),
                   jax.ShapeDtypeStruct((bsz, s, 1024), BF16)] + b_shapes,
        compiler_params=_params("parallel", "parallel"),
        name="inproj_even",
    )(x, scale, shift, pre_g, w_in, qn, kn, ca, sa, cb, sb,
      jnp.asarray(_residue_perm(T_RES, 4), BF16), jnp.asarray(_residue_perm(T_RES, 16), BF16))
    qa, ka, va, gates = outs[:4]
    qkv_b = [outs[4 + 3 * n:7 + 3 * n] for n in range(3)]
    return qa, ka, va, gates, qkv_b


def _inproj_odd_kernel(x_ref, sc_ref, sh_ref, g_ref, w_ref, wc_ref,
                       q_ref, k_ref, v_ref, gate_ref, vd_ref):
    h = _modulated_norm(x_ref[...], g_ref[...], sc_ref[...], sh_ref[...])

    def proj(a, b):
        return jnp.dot(h, w_ref[:, a:b], preferred_element_type=F32)

    q_ref[...] = (proj(0, 768) * QK_SCALE).astype(BF16)
    k_ref[...] = proj(768, 1536).astype(BF16)
    v_ref[...] = proj(1536, 2304).astype(BF16)
    gate_ref[:, 0:768] = _silu(proj(2304, 3072)).astype(BF16)
    u = proj(3072, 3328).astype(BF16)
    vc = jnp.dot(u, wc_ref[...], preferred_element_type=F32)
    vd_ref[0] = vc[:, 0:D_WIDTH].astype(BF16)
    vd_ref[1] = vc[:, D_WIDTH:2 * D_WIDTH].astype(BF16)
    gate_ref[:, 768:1024] = _silu(proj(3328, 3584)).astype(BF16)


def _inproj_odd(x, scale, shift, pre_g, w_in, wc):
    bsz, s, d = x.shape
    tm = TM_PROJ
    widths = (768, 768, 768, 1024)
    return pl.pallas_call(
        _inproj_odd_kernel,
        grid=(bsz, s // tm),
        in_specs=[
            _tok_spec(tm, d), _bcast_spec(d), _bcast_spec(d), _const_spec((1, d)),
            _const_spec((d, CD_IN)), _const_spec((D_WIDTH, 2 * D_WIDTH)),
        ],
        out_specs=[_tok_spec(tm, w) for w in widths]
        + [pl.BlockSpec((None, 2, tm, D_WIDTH), lambda b, i: (b, 0, i, 0))],
        out_shape=[jax.ShapeDtypeStruct((bsz, s, w), BF16) for w in widths]
        + [jax.ShapeDtypeStruct((bsz, 2, s, D_WIDTH), BF16)],
        compiler_params=_params("parallel", "parallel"),
        name="inproj_odd",
    )(x, scale, shift, pre_g, w_in, wc)


def _attn_a_kernel(q_ref, k_ref, v_ref, o_ref, s_ref, *, tk):
    tq = o_ref.shape[0]
    s_len = k_ref.shape[0]
    g = pl.program_id(1)
    qt = q_ref[...]

    nchunk = s_len // tk
    m = jnp.full((1, 4 * tq), NEG, F32)
    acc = jnp.zeros((LANES, 4 * tq), F32)
    s_ref[0] = jnp.dot(k_ref[0:tk, :], qt, preferred_element_type=F32)
    for j in range(nchunk):
        if j + 1 < nchunk:
            s_ref[(j + 1) % 2] = jnp.dot(k_ref[(j + 1) * tk:(j + 2) * tk, :], qt, preferred_element_type=F32)
        s = s_ref[j % 2]
        vc = v_ref[:, j * tk:(j + 1) * tk]
        m_new = jnp.maximum(m, jnp.max(s, axis=0, keepdims=True))
        p = jnp.exp2(s - m_new).astype(BF16)
        acc = jnp.exp2(m - m_new) * acc + jnp.dot(vc, p, preferred_element_type=F32)
        m = m_new
    first = g == 0
    num = jnp.where(first, acc[0:HEAD_DIM], acc[HEAD_DIM:LANES])
    den = jnp.where(first, acc[HEAD_DIM:HEAD_DIM + 1], acc[0:1])
    o = num / den
    for sl in range(2):
        pair = jnp.concatenate([o[:, (2 * sl) * tq:(2 * sl + 1) * tq],
                                o[:, (2 * sl + 1) * tq:(2 * sl + 2) * tq]], axis=0)
        o_ref[:, LANES * sl:LANES * (sl + 1)] = pair.T.astype(BF16)


def _attn_a(q, k, v):
    bsz, s, _ = k.shape
    tq = TQ_A
    tk = min(TK_A, s)
    return pl.pallas_call(
        functools.partial(_attn_a_kernel, tk=tk),
        grid=(bsz, 2, s // tq),
        in_specs=[
            pl.BlockSpec((None, None, None, LANES, 4 * tq), lambda b, g, i: (b, g, i, 0, 0)),
            pl.BlockSpec((None, s, LANES), lambda b, g, i: (b, 0, 0)),
            pl.BlockSpec((None, None, LANES, s), lambda b, g, i: (b, g, 0, 0)),
        ],
        out_specs=pl.BlockSpec((None, tq, 2 * LANES), lambda b, g, i: (b, i, g)),
        out_shape=jax.ShapeDtypeStruct((bsz, s, 512), BF16),
        scratch_shapes=[pltpu.VMEM((2, tk, 4 * tq), F32)],
        compiler_params=_params("parallel", "arbitrary", "arbitrary"),
        name="attn_a",
    )(q, k, v)


def _window(refs, start, size, cols, lq):
    rpt = refs[0].shape[1]
    pieces = []
    for n, r in enumerate(refs):
        for t in range(r.shape[0]):
            t0 = n * lq + t * rpt
            a, b = max(start, t0), min(start + size, t0 + rpt)
            if a < b:
                pieces.append(r[t, a - t0:b - t0, cols])
    return pieces[0] if len(pieces) == 1 else jnp.concatenate(pieces, axis=0)


def _store_rows(ref, r0, val, cols):
    rpt = ref.shape[1]
    n = val.shape[0]
    for t in range(ref.shape[0]):
        a, b = max(r0, t * rpt), min(r0 + n, (t + 1) * rpt)
        if a < b:
            ref[t, a - t * rpt:b - t * rpt, cols] = val[a - r0:b - r0]


def _attn_b_kernel(q_ref, kp_ref, kc_ref, kn_ref, vp_ref, vc_ref, vn_ref, o_ref, lse_ref, s_ref, *, l_len):
    lq = q_ref.shape[0] * q_ref.shape[1]
    sub = min(SUB_B, lq)
    base = pl.program_id(2) * lq
    lane = _lane_iota(sub)
    lo = lane < HEAD_DIM
    rq = lax.broadcasted_iota(jnp.int32, (sub, 2 * sub), 0)
    ck = lax.broadcasted_iota(jnp.int32, (sub, 2 * sub), 1)
    delta = ck - rq - sub // 2
    band = jnp.abs(delta) <= B_RADIUS
    items = [(sb, j) for sb in range(lq // sub) for j in range(4)]

    def scores(n):
        sb, j = items[n]
        r0 = sb * sub
        cols = slice(LANES * j, LANES * (j + 1))
        qs = _window((q_ref,), r0, sub, cols, lq).astype(F32)
        kw = _window((kp_ref, kc_ref, kn_ref), lq + r0 - sub // 2, 2 * sub, cols, lq)
        q2 = jnp.concatenate([jnp.where(lo, qs, 0.0), jnp.where(lo, 0.0, qs)], axis=0).astype(BF16)
        s_ref[n % 2] = lax.dot_general(q2, kw, (((1,), (1,)), ((), ())), preferred_element_type=F32)

    scores(0)
    for n, (sb, j) in enumerate(items):
        if n + 1 < len(items):
            scores(n + 1)
        r0 = sb * sub
        cols = slice(LANES * j, LANES * (j + 1))
        if j == 0:
            kpos = jnp.where(band, base + (r0 - sub // 2) + ck, -1)
            mask = jnp.where(jnp.logical_and(kpos >= 0, kpos < l_len), 0.0, NEG)
            mask2 = jnp.concatenate([mask, mask], axis=0)
            lse_tile = jnp.zeros((sub, LANES), F32)
        vw = _window((vp_ref, vc_ref, vn_ref), lq + r0 - sub // 2, 2 * sub, cols, lq)
        s = s_ref[n % 2] + mask2
        m = jnp.max(s, axis=1, keepdims=True)
        p = jnp.exp2(s - m)
        l = jnp.sum(p, axis=1, keepdims=True)
        o = jnp.dot(p.astype(BF16), vw, preferred_element_type=F32) * (1.0 / l)
        lse = m + jnp.log2(l)
        lse_tile = (lse_tile + jnp.where(lane == 2 * j, lse[0:sub], 0.0)
                    + jnp.where(lane == 2 * j + 1, lse[sub:2 * sub], 0.0))
        _store_rows(o_ref, r0, jnp.where(lo, o[0:sub], o[sub:2 * sub]).astype(BF16), cols)
        if j == 3:
            packed = jnp.zeros((sub, LANES), F32)
            rest = lse_tile
            for part in range(LSE_PARTS):
                term = rest.astype(BF16).astype(F32)
                rest = rest - term
                packed = packed + (pltpu.roll(term, 8 * part, 1) if part else term)
            _store_rows(lse_ref, r0, packed.astype(BF16), slice(None))


def _attn_b(q, k, v, dil):
    bsz, ntile, _, rpt, w = q.shape
    l_len = ntile * rpt
    lq = min(LQ_B, l_len)
    nt = lq // rpt
    nblk = l_len // lq

    def spec(width, shift):
        def index(b, r, i):
            return (b, jnp.clip(i + shift, 0, nblk - 1), r, 0, 0)
        return pl.BlockSpec((None, nt, None, rpt, width), index)

    cur, prv, nxt = spec(w, 0), spec(w, -1), spec(w, 1)
    return pl.pallas_call(
        functools.partial(_attn_b_kernel, l_len=l_len),
        grid=(bsz, dil, nblk),
        in_specs=[cur, prv, cur, nxt, prv, cur, nxt],
        out_specs=[cur, spec(LANES, 0)],
        out_shape=[jax.ShapeDtypeStruct(q.shape, BF16),
                   jax.ShapeDtypeStruct((bsz, ntile, dil, rpt, LANES), BF16)],
        scratch_shapes=[pltpu.VMEM((2, 2 * min(SUB_B, lq), 2 * min(SUB_B, lq)), F32)],
        compiler_params=_params("parallel", "parallel", "parallel"),
        name=f"attn_b_d{dil}",
    )(q, k, k, k, v, v, v)


def _attn_c_kernel(q_ref, kp_ref, kc_ref, kn_ref, vp_ref, vc_ref, vn_ref, bias_ref, o_ref, s_ref):
    tq = q_ref.shape[0]
    lane = _lane_iota(tq)
    lo = lane < HEAD_DIM
    npair = C_HEADS // 2

    def scores(pr):
        cols = slice(LANES * pr, LANES * (pr + 1))
        qs = q_ref[:, cols].astype(F32)
        k3 = jnp.concatenate([kp_ref[:, cols], kc_ref[:, cols], kn_ref[:, cols]], axis=0)
        q2 = jnp.concatenate([jnp.where(lo, qs, 0.0), jnp.where(lo, 0.0, qs)], axis=0).astype(BF16)
        s_ref[pr % 2] = lax.dot_general(q2, k3, (((1,), (1,)), ((), ())), preferred_element_type=F32)

    scores(0)
    for pr in range(npair):
        if pr + 1 < npair:
            scores(pr + 1)
        cols = slice(LANES * pr, LANES * (pr + 1))
        v3 = jnp.concatenate([vp_ref[:, cols], vc_ref[:, cols], vn_ref[:, cols]], axis=0)
        s = s_ref[pr % 2] + bias_ref[2 * pr:2 * pr + 2].reshape(2 * tq, 3 * tq)
        m = jnp.max(s, axis=1, keepdims=True)
        p = jnp.exp2(s - m)
        l = jnp.sum(p, axis=1, keepdims=True)
        o = jnp.dot(p.astype(BF16), v3, preferred_element_type=F32) * (1.0 / l)
        o_ref[:, cols] = jnp.where(lo, o[0:tq], o[tq:2 * tq]).astype(BF16)


def _bias_c(rpb):
    nq, nu = ROWS_C, 3 * ROWS_C
    assert nq == C_WIN_H // 2
    nrel = 2 * C_WIN_H - 1
    scaled = rpb.astype(F32) * LOG2E
    per_col = []
    for c in range(GRID_W):
        c0 = min(max(c - C_WIN_W // 2, 0), GRID_W - C_WIN_W)
        win = scaled[:, :, c0 - c + C_WIN_W - 1:c0 - c + 2 * C_WIN_W - 1]
        per_col.append(jnp.pad(win, ((0, 0), (0, 0), (c0, GRID_W - C_WIN_W - c0)), constant_values=NEG))
    tab = jnp.stack(per_col, axis=2)
    dead = jnp.full((C_HEADS, GRID_W, GRID_W), NEG, F32)
    kinds = []
    for kind in range(3):
        rows = []
        for j in range(nq):
            slots = []
            for u in range(nu):
                rel = (u - nq) - j + C_WIN_H - 1
                if kind == 0:
                    ok = nq <= u < nq + C_WIN_H
                elif kind == 2:
                    ok = 2 * nq - C_WIN_H <= u < 2 * nq
                else:
                    ok = nq - C_WIN_H // 2 <= u - j < nq + C_WIN_H // 2
                slots.append(tab[:, rel] if ok and 0 <= rel < nrel else dead)
            rows.append(jnp.concatenate(slots, axis=2))
        kinds.append(jnp.concatenate(rows, axis=1))
    return jnp.stack(kinds, axis=0)


def _attn_c(q, k, v, bias):
    bsz, s, w = q.shape
    tq = ROWS_C * GRID_W
    nblk = s // tq
    assert nblk >= 2
    kind = lambda i: jnp.where(i == 0, 0, jnp.where(i == nblk - 1, 2, 1))
    cur = pl.BlockSpec((None, tq, w), lambda b, i: (b, i, 0))
    prv = pl.BlockSpec((None, tq, w), lambda b, i: (b, jnp.maximum(i - 1, 0), 0))
    nxt = pl.BlockSpec((None, tq, w), lambda b, i: (b, jnp.minimum(i + 1, nblk - 1), 0))
    return pl.pallas_call(
        _attn_c_kernel,
        grid=(bsz, nblk),
        in_specs=[cur, prv, cur, nxt, prv, cur, nxt,
                  pl.BlockSpec((None, C_HEADS, tq, 3 * tq), lambda b, i: (kind(i), 0, 0, 0))],
        out_specs=cur,
        out_shape=jax.ShapeDtypeStruct((bsz, s, w), BF16),
        scratch_shapes=[pltpu.VMEM((2, 2 * tq, 3 * tq), F32)],
        compiler_params=_params("parallel", "arbitrary"),
        name="attn_c",
    )(q, k, k, k, v, v, v, bias)


S2_D = 64


def _dft_consts(s):
    s1 = s // S2_D
    th2 = 2 * np.pi * np.outer(np.arange(S2_D), np.arange(S2_D)) / S2_D
    c2, n2 = np.cos(th2) / 8.0, np.sin(th2) / 8.0
    w2big = np.block([[c2, n2], [-n2, c2]])
    th1 = 2 * np.pi * np.outer(np.arange(s1), np.arange(s1)) / s1
    w1cat = np.concatenate([np.cos(th1), np.sin(th1)], axis=1) / math.sqrt(s1)
    tht = 2 * np.pi * np.outer(np.arange(S2_D), np.arange(s1)) / s
    return (jnp.asarray(w2big, BF16), jnp.asarray(w1cat, BF16),
            jnp.asarray(np.cos(tht), F32), jnp.asarray(-np.sin(tht), F32))


def _chan_dft_const():
    th = 2 * np.pi * np.outer(np.arange(D_GROUP_DIM), np.arange(D_GROUP_DIM)) / D_GROUP_DIM
    eye = np.eye(D_GROUPS)
    wc = np.concatenate([np.kron(eye, np.cos(th)), np.kron(eye, -np.sin(th))], axis=1) / 8.0
    return jnp.asarray(wc, BF16)


def _dft1_kernel(v_ref, w_ref, tr_ref, ti_ref, o_ref):
    x = jnp.concatenate([v_ref[0], v_ref[1]], axis=0)
    b = jnp.dot(w_ref[...], x, preferred_element_type=F32)
    br, bi = b[0:S2_D], b[S2_D:2 * S2_D]
    tr, ti = tr_ref[...], ti_ref[...]
    o_ref[0] = (br * tr - bi * ti).astype(BF16)
    o_ref[1] = (br * ti + bi * tr).astype(BF16)


def _dft2_kernel(b_ref, w_ref, o_ref):
    for kk in range(b_ref.shape[1]):
        rhs = jnp.concatenate([b_ref[0, kk], b_ref[1, kk]], axis=0)
        x = jnp.dot(w_ref[...], rhs, preferred_element_type=F32)
        o_ref[:, D_WIDTH * kk:D_WIDTH * (kk + 1)] = x.astype(BF16)


def _fourier(vd, consts):
    bsz, _, s, w = vd.shape
    s1 = s // S2_D
    w2big, w1cat, tr, ti = consts
    ncol = s1 * w
    tn = min(TN_D1, ncol)
    twr = jnp.repeat(tr, w, axis=1)
    twi = jnp.repeat(ti, w, axis=1)
    b1 = pl.pallas_call(
        _dft1_kernel,
        grid=(bsz, ncol // tn),
        in_specs=[
            pl.BlockSpec((None, 2, S2_D, tn), lambda b, j: (b, 0, 0, j)),
            pl.BlockSpec((2 * S2_D, 2 * S2_D), lambda b, j: (0, 0)),
            pl.BlockSpec((S2_D, tn), lambda b, j: (0, j)),
            pl.BlockSpec((S2_D, tn), lambda b, j: (0, j)),
        ],
        out_specs=pl.BlockSpec((None, 2, S2_D, tn), lambda b, j: (b, 0, 0, j)),
        out_shape=jax.ShapeDtypeStruct((bsz, 2, S2_D, ncol), BF16),
        compiler_params=_params("parallel", "parallel"),
        name="dft_stage1",
    )(vd.reshape(bsz, 2, S2_D, ncol), w2big, twr, twi)
    tk2 = TK2_D
    f = pl.pallas_call(
        _dft2_kernel,
        grid=(bsz, S2_D // tk2),
        in_specs=[
            pl.BlockSpec((None, 2, tk2, s1, w), lambda b, j: (b, 0, j, 0, 0)),
            pl.BlockSpec((s1, 2 * s1), lambda b, j: (0, 0)),
        ],
        out_specs=pl.BlockSpec((None, s1, tk2 * w), lambda b, j: (b, 0, j)),
        out_shape=jax.ShapeDtypeStruct((bsz, s1, S2_D * w), BF16),
        compiler_params=_params("parallel", "parallel"),
        name="dft_stage2",
    )(b1.reshape(bsz, 2, S2_D, s1, w), w1cat)
    return f.reshape(bsz, s, w)


def _finish(y, x_ref, gate_ref, g_ref, o_ref):
    ms = jnp.mean(y * y, axis=-1, keepdims=True)
    yn = y * lax.rsqrt(ms + EPS) * g_ref[...]
    o_ref[...] = x_ref[...] + gate_ref[...] * yn


def _outproj_even_kernel(oa_ref, o1_ref, o4_ref, o16_ref, l1_ref, l4_ref, l16_ref, gates_ref,
                         ex_ref, p4_ref, p16_ref, w_ref, x_ref, gate_ref, g_ref, o_ref):
    tm = x_ref.shape[0]

    def natural(ref, perm_ref):
        return jnp.concatenate(
            [jnp.dot(perm_ref[...], ref[t].reshape(T_RES, ref.shape[-1]), preferred_element_type=F32)
             for t in range(tm // T_RES)], axis=0)

    def lse(x):
        out = x
        for part in range(1, LSE_PARTS):
            out = out + pltpu.roll(x, LANES - 8 * part, 1)
        return out

    l1 = lse(l1_ref[...].astype(F32))
    l2 = lse(natural(l4_ref, p4_ref))
    l3 = lse(natural(l16_ref, p16_ref))
    mx = jnp.maximum(jnp.maximum(l1, l2), l3)
    e1, e2, e3 = jnp.exp2(l1 - mx), jnp.exp2(l2 - mx), jnp.exp2(l3 - mx)
    inv = 1.0 / (e1 + e2 + e3)
    ex = ex_ref[...]

    def expand(wgt):
        hi = wgt.astype(BF16)
        lo = (wgt - hi.astype(F32)).astype(BF16)
        return jnp.dot(hi, ex, preferred_element_type=F32) + jnp.dot(lo, ex, preferred_element_type=F32)

    ob = (expand(e1 * inv) * o1_ref[...].astype(F32) + expand(e2 * inv) * natural(o4_ref, p4_ref)
          + expand(e3 * inv) * natural(o16_ref, p16_ref))
    ma = (oa_ref[...].astype(F32) * gates_ref[:, 0:512].astype(F32)).astype(BF16)
    mb = (ob * gates_ref[:, 512:1024].astype(F32)).astype(BF16)
    y = (jnp.dot(ma, w_ref[0:512, :], preferred_element_type=F32)
         + jnp.dot(mb, w_ref[512:1024, :], preferred_element_type=F32))
    _finish(y, x_ref, gate_ref, g_ref, o_ref)


def _outproj_even(oa, obs, lses, gates, ex, w_out, x, gate, post_g):
    bsz, s, d = x.shape
    tm = TM_PROJ
    return pl.pallas_call(
        _outproj_even_kernel,
        grid=(bsz, s // tm),
        in_specs=[_tok_spec(tm, 512), _tok_spec(tm, 512), _res_spec(tm, 4, 512), _res_spec(tm, 16, 512),
                  _tok_spec(tm, LANES), _res_spec(tm, 4, LANES), _res_spec(tm, 16, LANES),
                  _tok_spec(tm, 1024), _const_spec((LANES, 512)),
                  _const_spec((T_RES, T_RES)), _const_spec((T_RES, T_RES)), _const_spec((d, d)),
                  _tok_spec(tm, d), _bcast_spec(d), _const_spec((1, d))],
        out_specs=_tok_spec(tm, d),
        out_shape=jax.ShapeDtypeStruct((bsz, s, d), F32),
        compiler_params=_params("parallel", "parallel"),
        name="outproj_even",
    )(oa, *obs, *lses, gates, ex,
      jnp.asarray(_residue_perm(T_RES, 4).T, BF16), jnp.asarray(_residue_perm(T_RES, 16).T, BF16),
      w_out, x, gate, post_g)


def _outproj_odd_kernel(oc_ref, f_ref, gates_ref, lin_ref, w_ref, x_ref, gate_ref, g_ref, o_ref):
    od = jnp.dot(f_ref[...], lin_ref[...], preferred_element_type=F32)
    mc = (oc_ref[...].astype(F32) * gates_ref[:, 0:768].astype(F32)).astype(BF16)
    md = (od * gates_ref[:, 768:1024].astype(F32)).astype(BF16)
    y = (jnp.dot(mc, w_ref[0:768, :], preferred_element_type=F32)
         + jnp.dot(md, w_ref[768:1024, :], preferred_element_type=F32))
    _finish(y, x_ref, gate_ref, g_ref, o_ref)


def _outproj_odd(oc, f, gates, lin, w_out, x, gate, post_g):
    bsz, s, d = x.shape
    tm = TM_PROJ
    return pl.pallas_call(
        _outproj_odd_kernel,
        grid=(bsz, s // tm),
        in_specs=[_tok_spec(tm, 768), _tok_spec(tm, D_WIDTH), _tok_spec(tm, 1024),
                  _const_spec((D_WIDTH, D_WIDTH)), _const_spec((d, d)),
                  _tok_spec(tm, d), _bcast_spec(d), _const_spec((1, d))],
        out_specs=_tok_spec(tm, d),
        out_shape=jax.ShapeDtypeStruct((bsz, s, d), F32),
        compiler_params=_params("parallel", "parallel"),
        name="outproj_odd",
    )(oc, f, gates, lin, w_out, x, gate, post_g)


def _rope_tables(s):
    t = jnp.arange(s)

    def tab(pos, dim, theta):
        inv = theta ** (-jnp.arange(0, dim, 2, dtype=F32) / dim)
        ang = pos[:, None] * inv[None, :]
        return jnp.cos(ang), jnp.sin(ang)

    cr, sr = tab((t // GRID_W).astype(F32), HEAD_DIM // 2, A_ROPE_THETA)
    cc, sc = tab((t % GRID_W).astype(F32), HEAD_DIM // 2, A_ROPE_THETA)
    cb, sb = tab(t.astype(F32), B_ROPE_DIMS, B_ROPE_THETA)
    rest = HEAD_DIM - B_ROPE_DIMS
    ca = jnp.concatenate([cr, cr, cc, cc] * 2, axis=-1)
    sa = jnp.concatenate([-sr, sr, -sc, sc] * 2, axis=-1)
    cb = jnp.concatenate([cb, cb, jnp.ones((s, rest), F32)] * 2, axis=-1)
    sb = jnp.concatenate([-sb, sb, jnp.zeros((s, rest), F32)] * 2, axis=-1)
    return ca, sa, cb, sb


def _even_layer(x, mod, pre_g, post_g, w_in, w_out, qn, kn, tabs, ex):
    bsz, s, _ = x.shape
    shift, scale, gate = mod
    qa, ka, va, gates, qkv_b = _inproj_even(x, scale, shift, pre_g, w_in, qn, kn, tabs)
    oa = _attn_a(qa, ka, va)
    obs, lses = [], []
    for n, dil in enumerate(B_DILATIONS):
        q, k, v = (t[n] for t in qkv_b)
        if dil == 1:
            lq = min(LQ_B, s)
            q, k, v = (t.reshape(bsz, s // lq, 1, lq, t.shape[-1]) for t in (q, k, v))
        o, lse = _attn_b(q, k, v, dil)
        if dil == 1:
            o, lse = o.reshape(bsz, s, o.shape[-1]), lse.reshape(bsz, s, LANES)
        obs.append(o)
        lses.append(lse)
    return _outproj_even(oa, obs, lses, gates, ex, w_out, x, gate, post_g)


def _odd_layer(x, mod, pre_g, post_g, w_in, w_out, bias, lin, wc, dft):
    shift, scale, gate = mod
    qc, kc, vc, gates, vd = _inproj_odd(x, scale, shift, pre_g, w_in, wc)
    oc = _attn_c(qc, kc, vc, bias)
    f = _fourier(vd, dft)
    return _outproj_odd(oc, f, gates, lin, w_out, x, gate, post_g)


def _trunk(x, mods, pre_g, post_g, w_in_ab, w_out_ab, qn_a, kn_a, w_in_cd, w_out_cd, biases, lin_d):
    bsz, s, d = x.shape
    assert s % TM_PROJ == 0 and s % (16 * SUB_B) == 0
    tabs = _rope_tables(s)
    dft = _dft_consts(s)
    wc = _chan_dft_const()
    lane = np.arange(LANES)
    ex = jnp.asarray(lane[:, None] == np.arange(512)[None, :] // HEAD_DIM, BF16)
    for i in range(DEPTH):
        j = i // 2
        mod = tuple(mods[i][:, None, k * d:(k + 1) * d] for k in range(3))
        pg, qg = pre_g[i][None, :], post_g[i][None, :]
        if i % 2 == 0:
            qn = jnp.tile(qn_a[j], 2)[None, :]
            kn = jnp.tile(kn_a[j], 2)[None, :]
            x = _even_layer(x, mod, pg, qg, w_in_ab[j], w_out_ab[j], qn, kn, tabs, ex)
        else:
            x = _odd_layer(x, mod, pg, qg, w_in_cd[j], w_out_cd[j], biases[j], lin_d[j], wc, dft)
    return x


def kernel(x_prompt, x_sample, c_prompt, c_sample, pre_g, post_g, ada_w, ada_b,
           w_in_ab, w_out_ab, qn_a, kn_a, w_in_cd, w_out_cd, rpb_c, lin_d):
    nb = x_prompt.shape[0]
    mods = _adaln(jnp.concatenate([c_prompt, c_sample], axis=0), ada_w, ada_b)
    biases = [_bias_c(rpb_c[j]) for j in range(rpb_c.shape[0])]
    args = (pre_g, post_g, w_in_ab.astype(BF16), w_out_ab.astype(BF16), qn_a, kn_a,
            w_in_cd.astype(BF16), w_out_cd.astype(BF16), biases, lin_d.astype(BF16))
    y_prompt = _trunk(x_prompt, mods[:, :nb], *args)
    y_sample = _trunk(x_sample, mods[:, nb:], *args)
    return (y_prompt, y_sample)
```

```python
import functools
import math

import numpy as np
import jax
import jax.numpy as jnp
from jax import lax
from jax.experimental import pallas as pl
from jax.experimental.pallas import tpu as pltpu

F32 = jnp.float32
BF16 = jnp.bfloat16

D_MODEL = 1024
DEPTH = 4
HEAD_DIM = 64
GRID_W = 64
A_ROPE_THETA = 10000.0
B_ROPE_THETA = 500000.0
B_ROPE_DIMS = 16
B_DILATIONS = (1, 4, 16)
B_RADIUS = 64
C_HEADS = 12
C_WIN_H = 8
C_WIN_W = 16
D_GROUPS = 4
D_GROUP_DIM = 64
D_WIDTH = D_GROUPS * D_GROUP_DIM
AB_IN = 3328
CD_IN = 3584
EPS = 1e-6
NEG = -1e30
LOG2E = 1.4426950408889634
QK_SCALE = LOG2E * HEAD_DIM ** -0.5

LANES = 128
VMEM_LIMIT = 56 * 1024 * 1024

TM_PROJ = 512
T_RES = 256
TQ_A = 256
TK_A = 2048
LQ_B = 256
SUB_B = 128
ROWS_C = 4
TN_D1 = 4096
TK2_D = 8
LSE_PARTS = 3


def _params(*sem):
    return pltpu.CompilerParams(dimension_semantics=sem, vmem_limit_bytes=VMEM_LIMIT)


def _silu(x):
    return x / (1.0 + jnp.exp(-x))


def _lane_iota(rows):
    return lax.broadcasted_iota(jnp.int32, (rows, LANES), 1)


def _adaln_kernel(c_ref, w_ref, b_ref, o_ref):
    a = _silu(c_ref[...]).astype(BF16)
    o_ref[...] = jnp.dot(a, w_ref[...].astype(BF16), preferred_element_type=F32) + b_ref[...]


def _adaln(c_all, ada_w, ada_b):
    bt = c_all.shape[0]
    d = D_MODEL
    return pl.pallas_call(
        _adaln_kernel,
        grid=(DEPTH, 3),
        in_specs=[
            pl.BlockSpec((bt, d), lambda l, j: (0, 0)),
            pl.BlockSpec((None, d, d), lambda l, j: (l, 0, j)),
            pl.BlockSpec((None, 1, d), lambda l, j: (l, 0, j)),
        ],
        out_specs=pl.BlockSpec((None, bt, d), lambda l, j: (l, 0, j)),
        out_shape=jax.ShapeDtypeStruct((DEPTH, bt, 3 * d), F32),
        compiler_params=_params("parallel", "parallel"),
        name="adaln",
    )(c_all, ada_w, ada_b.reshape(DEPTH, 1, 3 * d))


def _modulated_norm(x, g, scale, shift):
    ms = jnp.mean(x * x, axis=-1, keepdims=True)
    y = x * lax.rsqrt(ms + EPS) * g
    return (y * (1.0 + scale) + shift).astype(BF16)


def _head_norm(xs, gain, lo):
    x2 = xs * xs
    s_lo = jnp.sum(jnp.where(lo, x2, 0.0), axis=1, keepdims=True)
    s_hi = jnp.sum(jnp.where(lo, 0.0, x2), axis=1, keepdims=True)
    ss = jnp.where(lo, s_lo, s_hi)
    return xs * lax.rsqrt(ss * (1.0 / HEAD_DIM) + EPS) * gain


def _rope(xs, cos, sin_signed, first_half, shift):
    up = pltpu.roll(xs, LANES - shift, 1)
    dn = pltpu.roll(xs, shift, 1)
    return xs * cos + jnp.where(first_half, up, dn) * sin_signed


def _inproj_even_kernel(x_ref, sc_ref, sh_ref, g_ref, w_ref, qn_ref, kn_ref,
                        ca_ref, sa_ref, cb_ref, sb_ref, p4_ref, p16_ref,
                        qa_ref, ka_ref, va_ref, gate_ref, *b_refs):
    tm = x_ref.shape[0]
    h = _modulated_norm(x_ref[...], g_ref[...], sc_ref[...], sh_ref[...])
    lane = _lane_iota(tm)
    half_a = (lane & 16) == 0
    half_b = (lane & 8) == 0
    lo = lane < HEAD_DIM
    ca, sa, cb, sb = ca_ref[...], sa_ref[...], cb_ref[...], sb_ref[...]
    full = jnp.dot(h, w_ref[...], preferred_element_type=F32)

    def proj(a, b):
        return full[:, a:b]

    def emit_b(vals):
        for n, val in enumerate(vals):
            b_refs[3 * n][...] = val
        cat = jnp.concatenate(vals, axis=1)
        for d, perm_ref in enumerate((p4_ref, p16_ref)):
            for t in range(tm // T_RES):
                pv = jnp.dot(perm_ref[...], cat[t * T_RES:(t + 1) * T_RES], preferred_element_type=F32)
                for n in range(3):
                    o_ref = b_refs[3 * n + 1 + d]
                    o_ref[t] = pv[:, 512 * n:512 * (n + 1)].astype(BF16).reshape(o_ref.shape[1:])

    p = proj(0, 512)
    lane_half = lane // HEAD_DIM
    for j in range(4):
        xs = _head_norm(p[:, LANES * j:LANES * (j + 1)], qn_ref[...], lo)
        xs = _rope(xs, ca, sa, half_a, 16) * QK_SCALE
        swapped = pltpu.roll(xs, HEAD_DIM, 1)
        for half in range(2):
            hd = 2 * j + half
            g, hh = hd // 4, hd % 4
            both = jnp.where(lane_half == half, xs, swapped)
            ht = jnp.where(lane_half == g, both, 0.0).T.astype(BF16)
            for t in range(tm // TQ_A):
                qa_ref[g, t, :, hh * TQ_A:(hh + 1) * TQ_A] = ht[:, t * TQ_A:(t + 1) * TQ_A]
    xs = _head_norm(proj(512, 640), kn_ref[...], lo)
    ka_ref[...] = _rope(xs, ca, sa, half_a, 16).astype(BF16)
    v = proj(640, 768)
    va_ref[0] = jnp.where(lo, v, 1.0).T.astype(BF16)
    va_ref[1] = jnp.where(lo, 1.0, v).T.astype(BF16)
    gate_ref[:, 0:512] = _silu(proj(768, 1280)).astype(BF16)
    p = proj(1280, 1792)
    qb = [(_rope(p[:, LANES * j:LANES * (j + 1)], cb, sb, half_b, 8) * QK_SCALE).astype(BF16)
          for j in range(4)]
    p = proj(1792, 2304)
    kb = [_rope(p[:, LANES * j:LANES * (j + 1)], cb, sb, half_b, 8).astype(BF16) for j in range(4)]
    emit_b([jnp.concatenate(qb, axis=1), jnp.concatenate(kb, axis=1), proj(2304, 2816).astype(BF16)])
    gate_ref[:, 512:1024] = _silu(proj(2816, 3328)).astype(BF16)


def _tok_spec(tm, width):
    return pl.BlockSpec((None, tm, width), lambda b, i: (b, i, 0))


def _res_spec(tm, dil, width):
    return pl.BlockSpec((None, tm // T_RES, dil, T_RES // dil, width), lambda b, i: (b, i, 0, 0, 0))


def _bcast_spec(width):
    return pl.BlockSpec((None, 1, width), lambda b, i: (b, 0, 0))


def _const_spec(shape):
    return pl.BlockSpec(shape, lambda b, i: (0,) * len(shape))


def _residue_perm(tm, dil):
    r = np.arange(tm)
    src = (r % (tm // dil)) * dil + r // (tm // dil)
    return np.asarray(r[None, :] == src[:, None], np.float32)


def _inproj_even(x, scale, shift, pre_g, w_in, qn, kn, tabs):
    bsz, s, d = x.shape
    tm = TM_PROJ
    ca, sa, cb, sb = tabs
    tab_spec = pl.BlockSpec((tm, LANES), lambda b, i: (i, 0))
    b_specs, b_shapes = [], []
    for _ in range(3):
        b_specs.append(_tok_spec(tm, 512))
        b_shapes.append(jax.ShapeDtypeStruct((bsz, s, 512), BF16))
        for dil in B_DILATIONS[1:]:
            b_specs.append(_res_spec(tm, dil, 512))
            b_shapes.append(jax.ShapeDtypeStruct((bsz, s // T_RES, dil, T_RES // dil, 512), BF16))
    outs = pl.pallas_call(
        _inproj_even_kernel,
        grid=(bsz, s // tm),
        in_specs=[
            _tok_spec(tm, d), _bcast_spec(d), _bcast_spec(d), _const_spec((1, d)),
            _const_spec((d, AB_IN)), _const_spec((1, LANES)), _const_spec((1, LANES)),
            tab_spec, tab_spec, tab_spec, tab_spec,
            _const_spec((T_RES, T_RES)), _const_spec((T_RES, T_RES)),
        ],
        out_specs=[pl.BlockSpec((None, 2, tm // TQ_A, LANES, 4 * TQ_A), lambda b, i: (b, 0, i, 0, 0)),
                   _tok_spec(tm, LANES),
                   pl.BlockSpec((None, 2, LANES, tm), lambda b, i: (b, 0, 0, i)),
                   _tok_spec(tm, 1024)] + b_specs,
        out_shape=[jax.ShapeDtypeStruct((bsz, 2, s // TQ_A, LANES, 4 * TQ_A), BF16),
                   jax.ShapeDtypeStruct((bsz, s, LANES), BF16),
                   jax.ShapeDtypeStruct((bsz, 2, LANES, s), BF16),
                   jax.ShapeDtypeStruct((bsz, s, 1024), BF16)] + b_shapes,
        compiler_params=_params("parallel", "parallel"),
        name="inproj_even",
    )(x, scale, shift, pre_g, w_in, qn, kn, ca, sa, cb, sb,
      jnp.asarray(_residue_perm(T_RES, 4), BF16), jnp.asarray(_residue_perm(T_RES, 16), BF16))
    qa, ka, va, gates = outs[:4]
    qkv_b = [outs[4 + 3 * n:7 + 3 * n] for n in range(3)]
    return qa, ka, va, gates, qkv_b


def _inproj_odd_kernel(x_ref, sc_ref, sh_ref, g_ref, w_ref, wc_ref,
                       q_ref, k_ref, v_ref, gate_ref, vd_ref):
    h = _modulated_norm(x_ref[...], g_ref[...], sc_ref[...], sh_ref[...])
    full = jnp.dot(h, w_ref[...], preferred_element_type=F32)

    def proj(a, b):
        return full[:, a:b]

    q_ref[...] = (proj(0, 768) * QK_SCALE).astype(BF16)
    k_ref[...] = proj(768, 1536).astype(BF16)
    v_ref[...] = proj(1536, 2304).astype(BF16)
    gate_ref[:, 0:768] = _silu(proj(2304, 3072)).astype(BF16)
    u = proj(3072, 3328).astype(BF16)
    vc = jnp.dot(u, wc_ref[...], preferred_element_type=F32)
    vd_ref[0] = vc[:, 0:D_WIDTH].astype(BF16)
    vd_ref[1] = vc[:, D_WIDTH:2 * D_WIDTH].astype(BF16)
    gate_ref[:, 768:1024] = _silu(proj(3328, 3584)).astype(BF16)


def _inproj_odd(x, scale, shift, pre_g, w_in, wc):
    bsz, s, d = x.shape
    tm = TM_PROJ
    widths = (768, 768, 768, 1024)
    return pl.pallas_call(
        _inproj_odd_kernel,
        grid=(bsz, s // tm),
        in_specs=[
            _tok_spec(tm, d), _bcast_spec(d), _bcast_spec(d), _const_spec((1, d)),
            _const_spec((d, CD_IN)), _const_spec((D_WIDTH, 2 * D_WIDTH)),
        ],
        out_specs=[_tok_spec(tm, w) for w in widths]
        + [pl.BlockSpec((None, 2, tm, D_WIDTH), lambda b, i: (b, 0, i, 0))],
        out_shape=[jax.ShapeDtypeStruct((bsz, s, w), BF16) for w in widths]
        + [jax.ShapeDtypeStruct((bsz, 2, s, D_WIDTH), BF16)],
        compiler_params=_params("parallel", "parallel"),
        name="inproj_odd",
    )(x, scale, shift, pre_g, w_in, wc)


def _attn_a_kernel(q_ref, k_ref, v_ref, o_ref, s_ref, *, tk):
    tq = o_ref.shape[0]
    s_len = k_ref.shape[0]
    g = pl.program_id(1)
    qt = q_ref[...]

    nchunk = s_len // tk
    m = jnp.full((1, 4 * tq), NEG, F32)
    acc = jnp.zeros((LANES, 4 * tq), F32)
    s_ref[0] = jnp.dot(k_ref[0:tk, :], qt, preferred_element_type=F32)
    for j in range(nchunk):
        if j + 1 < nchunk:
            s_ref[(j + 1) % 2] = jnp.dot(k_ref[(j + 1) * tk:(j + 2) * tk, :], qt, preferred_element_type=F32)
        s = s_ref[j % 2]
        vc = v_ref[:, j * tk:(j + 1) * tk]
        m_new = jnp.maximum(m, jnp.max(s, axis=0, keepdims=True))
        p = jnp.exp2(s - m_new).astype(BF16)
        acc = jnp.exp2(m - m_new) * acc + jnp.dot(vc, p, preferred_element_type=F32)
        m = m_new
    first = g == 0
    num = jnp.where(first, acc[0:HEAD_DIM], acc[HEAD_DIM:LANES])
    den = jnp.where(first, acc[HEAD_DIM:HEAD_DIM + 1], acc[0:1])
    o = num / den
    for sl in range(2):
        pair = jnp.concatenate([o[:, (2 * sl) * tq:(2 * sl + 1) * tq],
                                o[:, (2 * sl + 1) * tq:(2 * sl + 2) * tq]], axis=0)
        o_ref[:, LANES * sl:LANES * (sl + 1)] = pair.T.astype(BF16)


def _attn_a(q, k, v):
    bsz, s, _ = k.shape
    tq = TQ_A
    tk = min(TK_A, s // 2)
    return pl.pallas_call(
        functools.partial(_attn_a_kernel, tk=tk),
        grid=(bsz, 2, s // tq),
        in_specs=[
            pl.BlockSpec((None, None, None, LANES, 4 * tq), lambda b, g, i: (b, g, i, 0, 0)),
            pl.BlockSpec((None, s, LANES), lambda b, g, i: (b, 0, 0)),
            pl.BlockSpec((None, None, LANES, s), lambda b, g, i: (b, g, 0, 0)),
        ],
        out_specs=pl.BlockSpec((None, tq, 2 * LANES), lambda b, g, i: (b, i, g)),
        out_shape=jax.ShapeDtypeStruct((bsz, s, 512), BF16),
        scratch_shapes=[pltpu.VMEM((2, tk, 4 * tq), F32)],
        compiler_params=_params("parallel", "arbitrary", "arbitrary"),
        name="attn_a",
    )(q, k, v)


def _window(refs, start, size, cols, lq):
    rpt = refs[0].shape[1]
    pieces = []
    for n, r in enumerate(refs):
        for t in range(r.shape[0]):
            t0 = n * lq + t * rpt
            a, b = max(start, t0), min(start + size, t0 + rpt)
            if a < b:
                pieces.append(r[t, a - t0:b - t0, cols])
    return pieces[0] if len(pieces) == 1 else jnp.concatenate(pieces, axis=0)


def _store_rows(ref, r0, val, cols):
    rpt = ref.shape[1]
    n = val.shape[0]
    for t in range(ref.shape[0]):
        a, b = max(r0, t * rpt), min(r0 + n, (t + 1) * rpt)
        if a < b:
            ref[t, a - t * rpt:b - t * rpt, cols] = val[a - r0:b - r0]


def _attn_b_kernel(q_ref, kp_ref, kc_ref, kn_ref, vp_ref, vc_ref, vn_ref, o_ref, lse_ref, s_ref, *, l_len):
    lq = q_ref.shape[0] * q_ref.shape[1]
    sub = min(SUB_B, lq)
    base = pl.program_id(2) * lq
    lane = _lane_iota(sub)
    lo = lane < HEAD_DIM
    rq = lax.broadcasted_iota(jnp.int32, (sub, 2 * sub), 0)
    ck = lax.broadcasted_iota(jnp.int32, (sub, 2 * sub), 1)
    delta = ck - rq - sub // 2
    band = jnp.abs(delta) <= B_RADIUS
    items = [(sb, j) for sb in range(lq // sub) for j in range(4)]

    def scores(n):
        sb, j = items[n]
        r0 = sb * sub
        cols = slice(LANES * j, LANES * (j + 1))
        qs = _window((q_ref,), r0, sub, cols, lq).astype(F32)
        kw = _window((kp_ref, kc_ref, kn_ref), lq + r0 - sub // 2, 2 * sub, cols, lq)
        q2 = jnp.concatenate([jnp.where(lo, qs, 0.0), jnp.where(lo, 0.0, qs)], axis=0).astype(BF16)
        s_ref[n % 2] = lax.dot_general(q2, kw, (((1,), (1,)), ((), ())), preferred_element_type=F32)

    scores(0)
    for n, (sb, j) in enumerate(items):
        if n + 1 < len(items):
            scores(n + 1)
        r0 = sb * sub
        cols = slice(LANES * j, LANES * (j + 1))
        if j == 0:
            kpos = jnp.where(band, base + (r0 - sub // 2) + ck, -1)
            mask = jnp.where(jnp.logical_and(kpos >= 0, kpos < l_len), 0.0, NEG)
            mask2 = jnp.concatenate([mask, mask], axis=0)
            lse_tile = jnp.zeros((sub, LANES), F32)
        vw = _window((vp_ref, vc_ref, vn_ref), lq + r0 - sub // 2, 2 * sub, cols, lq)
        s = s_ref[n % 2] + mask2
        m = jnp.max(s, axis=1, keepdims=True)
        p = jnp.exp2(s - m)
        l = jnp.sum(p, axis=1, keepdims=True)
        o = jnp.dot(p.astype(BF16), vw, preferred_element_type=F32) * (1.0 / l)
        lse = m + jnp.log2(l)
        lse_tile = (lse_tile + jnp.where(lane == 2 * j, lse[0:sub], 0.0)
                    + jnp.where(lane == 2 * j + 1, lse[sub:2 * sub], 0.0))
        _store_rows(o_ref, r0, jnp.where(lo, o[0:sub], o[sub:2 * sub]).astype(BF16), cols)
        if j == 3:
            packed = jnp.zeros((sub, LANES), F32)
            rest = lse_tile
            for part in range(LSE_PARTS):
                term = rest.astype(BF16).astype(F32)
                rest = rest - term
                packed = packed + (pltpu.roll(term, 8 * part, 1) if part else term)
            _store_rows(lse_ref, r0, packed.astype(BF16), slice(None))


def _attn_b(q, k, v, dil):
    bsz, ntile, _, rpt, w = q.shape
    l_len = ntile * rpt
    lq = min(LQ_B, l_len)
    nt = lq // rpt
    nblk = l_len // lq

    def spec(width, shift):
        def index(b, r, i):
            return (b, jnp.clip(i + shift, 0, nblk - 1), r, 0, 0)
        return pl.BlockSpec((None, nt, None, rpt, width), index)

    cur, prv, nxt = spec(w, 0), spec(w, -1), spec(w, 1)
    return pl.pallas_call(
        functools.partial(_attn_b_kernel, l_len=l_len),
        grid=(bsz, dil, nblk),
        in_specs=[cur, prv, cur, nxt, prv, cur, nxt],
        out_specs=[cur, spec(LANES, 0)],
        out_shape=[jax.ShapeDtypeStruct(q.shape, BF16),
                   jax.ShapeDtypeStruct((bsz, ntile, dil, rpt, LANES), BF16)],
        scratch_shapes=[pltpu.VMEM((2, 2 * min(SUB_B, lq), 2 * min(SUB_B, lq)), F32)],
        compiler_params=_params("parallel", "parallel", "parallel"),
        name=f"attn_b_d{dil}",
    )(q, k, k, k, v, v, v)


def _attn_c_kernel(q_ref, kp_ref, kc_ref, kn_ref, vp_ref, vc_ref, vn_ref, bias_ref, o_ref, s_ref):
    tq = q_ref.shape[0]
    lane = _lane_iota(tq)
    lo = lane < HEAD_DIM
    npair = C_HEADS // 2

    def scores(pr):
        cols = slice(LANES * pr, LANES * (pr + 1))
        qs = q_ref[:, cols].astype(F32)
        k3 = jnp.concatenate([kp_ref[:, cols], kc_ref[:, cols], kn_ref[:, cols]], axis=0)
        q2 = jnp.concatenate([jnp.where(lo, qs, 0.0), jnp.where(lo, 0.0, qs)], axis=0).astype(BF16)
        s_ref[pr % 2] = lax.dot_general(q2, k3, (((1,), (1,)), ((), ())), preferred_element_type=F32)

    scores(0)
    for pr in range(npair):
        if pr + 1 < npair:
            scores(pr + 1)
        cols = slice(LANES * pr, LANES * (pr + 1))
        v3 = jnp.concatenate([vp_ref[:, cols], vc_ref[:, cols], vn_ref[:, cols]], axis=0)
        s = s_ref[pr % 2] + bias_ref[2 * pr:2 * pr + 2].reshape(2 * tq, 3 * tq)
        m = jnp.max(s, axis=1, keepdims=True)
        p = jnp.exp2(s - m)
        l = jnp.sum(p, axis=1, keepdims=True)
        o = jnp.dot(p.astype(BF16), v3, preferred_element_type=F32) * (1.0 / l)
        o_ref[:, cols] = jnp.where(lo, o[0:tq], o[tq:2 * tq]).astype(BF16)


def _bias_c(rpb):
    nq, nu = ROWS_C, 3 * ROWS_C
    assert nq == C_WIN_H // 2
    nrel = 2 * C_WIN_H - 1
    scaled = rpb.astype(F32) * LOG2E
    per_col = []
    for c in range(GRID_W):
        c0 = min(max(c - C_WIN_W // 2, 0), GRID_W - C_WIN_W)
        win = scaled[:, :, c0 - c + C_WIN_W - 1:c0 - c + 2 * C_WIN_W - 1]
        per_col.append(jnp.pad(win, ((0, 0), (0, 0), (c0, GRID_W - C_WIN_W - c0)), constant_values=NEG))
    tab = jnp.stack(per_col, axis=2)
    dead = jnp.full((C_HEADS, GRID_W, GRID_W), NEG, F32)
    kinds = []
    for kind in range(3):
        rows = []
        for j in range(nq):
            slots = []
            for u in range(nu):
                rel = (u - nq) - j + C_WIN_H - 1
                if kind == 0:
                    ok = nq <= u < nq + C_WIN_H
                elif kind == 2:
                    ok = 2 * nq - C_WIN_H <= u < 2 * nq
                else:
                    ok = nq - C_WIN_H // 2 <= u - j < nq + C_WIN_H // 2
                slots.append(tab[:, rel] if ok and 0 <= rel < nrel else dead)
            rows.append(jnp.concatenate(slots, axis=2))
        kinds.append(jnp.concatenate(rows, axis=1))
    return jnp.stack(kinds, axis=0)


def _attn_c(q, k, v, bias):
    bsz, s, w = q.shape
    tq = ROWS_C * GRID_W
    nblk = s // tq
    assert nblk >= 2
    kind = lambda i: jnp.where(i == 0, 0, jnp.where(i == nblk - 1, 2, 1))
    cur = pl.BlockSpec((None, tq, w), lambda b, i: (b, i, 0))
    prv = pl.BlockSpec((None, tq, w), lambda b, i: (b, jnp.maximum(i - 1, 0), 0))
    nxt = pl.BlockSpec((None, tq, w), lambda b, i: (b, jnp.minimum(i + 1, nblk - 1), 0))
    return pl.pallas_call(
        _attn_c_kernel,
        grid=(bsz, nblk),
        in_specs=[cur, prv, cur, nxt, prv, cur, nxt,
                  pl.BlockSpec((None, C_HEADS, tq, 3 * tq), lambda b, i: (kind(i), 0, 0, 0))],
        out_specs=cur,
        out_shape=jax.ShapeDtypeStruct((bsz, s, w), BF16),
        scratch_shapes=[pltpu.VMEM((2, 2 * tq, 3 * tq), F32)],
        compiler_params=_params("parallel", "arbitrary"),
        name="attn_c",
    )(q, k, k, k, v, v, v, bias)


S2_D = 64


def _dft_consts(s):
    s1 = s // S2_D
    th2 = 2 * np.pi * np.outer(np.arange(S2_D), np.arange(S2_D)) / S2_D
    c2, n2 = np.cos(th2) / 8.0, np.sin(th2) / 8.0
    w2big = np.block([[c2, n2], [-n2, c2]])
    th1 = 2 * np.pi * np.outer(np.arange(s1), np.arange(s1)) / s1
    w1cat = np.concatenate([np.cos(th1), np.sin(th1)], axis=1) / math.sqrt(s1)
    tht = 2 * np.pi * np.outer(np.arange(S2_D), np.arange(s1)) / s
    return (jnp.asarray(w2big, BF16), jnp.asarray(w1cat, BF16),
            jnp.asarray(np.cos(tht), F32), jnp.asarray(-np.sin(tht), F32))


def _chan_dft_const():
    th = 2 * np.pi * np.outer(np.arange(D_GROUP_DIM), np.arange(D_GROUP_DIM)) / D_GROUP_DIM
    eye = np.eye(D_GROUPS)
    wc = np.concatenate([np.kron(eye, np.cos(th)), np.kron(eye, -np.sin(th))], axis=1) / 8.0
    return jnp.asarray(wc, BF16)


def _dft1_kernel(v_ref, w_ref, tr_ref, ti_ref, o_ref):
    x = jnp.concatenate([v_ref[0], v_ref[1]], axis=0)
    b = jnp.dot(w_ref[...], x, preferred_element_type=F32)
    br, bi = b[0:S2_D], b[S2_D:2 * S2_D]
    tr, ti = tr_ref[...], ti_ref[...]
    o_ref[0] = (br * tr - bi * ti).astype(BF16)
    o_ref[1] = (br * ti + bi * tr).astype(BF16)


def _dft2_kernel(b_ref, w_ref, o_ref):
    for kk in range(b_ref.shape[1]):
        rhs = jnp.concatenate([b_ref[0, kk], b_ref[1, kk]], axis=0)
        x = jnp.dot(w_ref[...], rhs, preferred_element_type=F32)
        o_ref[:, D_WIDTH * kk:D_WIDTH * (kk + 1)] = x.astype(BF16)


def _fourier(vd, consts):
    bsz, _, s, w = vd.shape
    s1 = s // S2_D
    w2big, w1cat, tr, ti = consts
    ncol = s1 * w
    tn = min(TN_D1, ncol)
    twr = jnp.repeat(tr, w, axis=1)
    twi = jnp.repeat(ti, w, axis=1)
    b1 = pl.pallas_call(
        _dft1_kernel,
        grid=(bsz, ncol // tn),
        in_specs=[
            pl.BlockSpec((None, 2, S2_D, tn), lambda b, j: (b, 0, 0, j)),
            pl.BlockSpec((2 * S2_D, 2 * S2_D), lambda b, j: (0, 0)),
            pl.BlockSpec((S2_D, tn), lambda b, j: (0, j)),
            pl.BlockSpec((S2_D, tn), lambda b, j: (0, j)),
        ],
        out_specs=pl.BlockSpec((None, 2, S2_D, tn), lambda b, j: (b, 0, 0, j)),
        out_shape=jax.ShapeDtypeStruct((bsz, 2, S2_D, ncol), BF16),
        compiler_params=_params("parallel", "parallel"),
        name="dft_stage1",
    )(vd.reshape(bsz, 2, S2_D, ncol), w2big, twr, twi)
    tk2 = TK2_D
    f = pl.pallas_call(
        _dft2_kernel,
        grid=(bsz, S2_D // tk2),
        in_specs=[
            pl.BlockSpec((None, 2, tk2, s1, w), lambda b, j: (b, 0, j, 0, 0)),
            pl.BlockSpec((s1, 2 * s1), lambda b, j: (0, 0)),
        ],
        out_specs=pl.BlockSpec((None, s1, tk2 * w), lambda b, j: (b, 0, j)),
        out_shape=jax.ShapeDtypeStruct((bsz, s1, S2_D * w), BF16),
        compiler_params=_params("parallel", "parallel"),
        name="dft_stage2",
    )(b1.reshape(bsz, 2, S2_D, s1, w), w1cat)
    return f.reshape(bsz, s, w)


def _finish(y, x_ref, gate_ref, g_ref, o_ref):
    ms = jnp.mean(y * y, axis=-1, keepdims=True)
    yn = y * lax.rsqrt(ms + EPS) * g_ref[...]
    o_ref[...] = x_ref[...] + gate_ref[...] * yn


def _outproj_even_kernel(oa_ref, o1_ref, o4_ref, o16_ref, l1_ref, l4_ref, l16_ref, gates_ref,
                         ex_ref, p4_ref, p16_ref, w_ref, x_ref, gate_ref, g_ref, o_ref):
    tm = x_ref.shape[0]
    ex = ex_ref[...]

    def natural(o_res, l_res, perm_ref):
        tiles = []
        for t in range(tm // T_RES):
            both = jnp.concatenate([o_res[t].reshape(T_RES, 512), l_res[t].reshape(T_RES, LANES)], axis=1)
            tiles.append(jnp.dot(perm_ref[...], both, preferred_element_type=F32))
        nat = jnp.concatenate(tiles, axis=0)
        return nat[:, 0:512], nat[:, 512:512 + LANES]

    def lse(x):
        out = x
        for part in range(1, LSE_PARTS):
            out = out + pltpu.roll(x, LANES - 8 * part, 1)
        return out

    o2, l2 = natural(o4_ref, l4_ref, p4_ref)
    o3, l3 = natural(o16_ref, l16_ref, p16_ref)
    l1, l2, l3 = lse(l1_ref[...].astype(F32)), lse(l2), lse(l3)
    mx = jnp.maximum(jnp.maximum(l1, l2), l3)
    e1, e2, e3 = jnp.exp2(l1 - mx), jnp.exp2(l2 - mx), jnp.exp2(l3 - mx)
    inv = 1.0 / (e1 + e2 + e3)
    terms = []
    for wgt in (e1 * inv, e2 * inv, e3 * inv):
        hi = wgt.astype(BF16)
        terms += [hi, (wgt - hi.astype(F32)).astype(BF16)]
    wide = jnp.dot(jnp.concatenate(terms, axis=0), ex, preferred_element_type=F32)
    w1, w2, w3 = (wide[2 * n * tm:(2 * n + 1) * tm] + wide[(2 * n + 1) * tm:(2 * n + 2) * tm] for n in range(3))
    ob = w1 * o1_ref[...].astype(F32) + w2 * o2 + w3 * o3
    ma = (oa_ref[...].astype(F32) * gates_ref[:, 0:512].astype(F32)).astype(BF16)
    mb = (ob * gates_ref[:, 512:1024].astype(F32)).astype(BF16)
    y = jnp.dot(jnp.concatenate([ma, mb], axis=1), w_ref[...], preferred_element_type=F32)
    _finish(y, x_ref, gate_ref, g_ref, o_ref)


def _outproj_even(oa, obs, lses, gates, ex, w_out, x, gate, post_g):
    bsz, s, d = x.shape
    tm = TM_PROJ
    return pl.pallas_call(
        _outproj_even_kernel,
        grid=(bsz, s // tm),
        in_specs=[_tok_spec(tm, 512), _tok_spec(tm, 512), _res_spec(tm, 4, 512), _res_spec(tm, 16, 512),
                  _tok_spec(tm, LANES), _res_spec(tm, 4, LANES), _res_spec(tm, 16, LANES),
                  _tok_spec(tm, 1024), _const_spec((LANES, 512)),
                  _const_spec((T_RES, T_RES)), _const_spec((T_RES, T_RES)), _const_spec((d, d)),
                  _tok_spec(tm, d), _bcast_spec(d), _const_spec((1, d))],
        out_specs=_tok_spec(tm, d),
        out_shape=jax.ShapeDtypeStruct((bsz, s, d), F32),
        compiler_params=_params("parallel", "parallel"),
        name="outproj_even",
    )(oa, *obs, *lses, gates, ex,
      jnp.asarray(_residue_perm(T_RES, 4).T, BF16), jnp.asarray(_residue_perm(T_RES, 16).T, BF16),
      w_out, x, gate, post_g)


def _outproj_odd_kernel(oc_ref, f_ref, gates_ref, lin_ref, w_ref, x_ref, gate_ref, g_ref, o_ref):
    od = jnp.dot(f_ref[...], lin_ref[...], preferred_element_type=F32)
    mc = (oc_ref[...].astype(F32) * gates_ref[:, 0:768].astype(F32)).astype(BF16)
    md = (od * gates_ref[:, 768:1024].astype(F32)).astype(BF16)
    y = jnp.dot(jnp.concatenate([mc, md], axis=1), w_ref[...], preferred_element_type=F32)
    _finish(y, x_ref, gate_ref, g_ref, o_ref)


def _outproj_odd(oc, f, gates, lin, w_out, x, gate, post_g):
    bsz, s, d = x.shape
    tm = TM_PROJ
    return pl.pallas_call(
        _outproj_odd_kernel,
        grid=(bsz, s // tm),
        in_specs=[_tok_spec(tm, 768), _tok_spec(tm, D_WIDTH), _tok_spec(tm, 1024),
                  _const_spec((D_WIDTH, D_WIDTH)), _const_spec((d, d)),
                  _tok_spec(tm, d), _bcast_spec(d), _const_spec((1, d))],
        out_specs=_tok_spec(tm, d),
        out_shape=jax.ShapeDtypeStruct((bsz, s, d), F32),
        compiler_params=_params("parallel", "parallel"),
        name="outproj_odd",
    )(oc, f, gates, lin, w_out, x, gate, post_g)


def _rope_tables(s):
    t = jnp.arange(s)

    def tab(pos, dim, theta):
        inv = theta ** (-jnp.arange(0, dim, 2, dtype=F32) / dim)
        ang = pos[:, None] * inv[None, :]
        return jnp.cos(ang), jnp.sin(ang)

    cr, sr = tab((t // GRID_W).astype(F32), HEAD_DIM // 2, A_ROPE_THETA)
    cc, sc = tab((t % GRID_W).astype(F32), HEAD_DIM // 2, A_ROPE_THETA)
    cb, sb = tab(t.astype(F32), B_ROPE_DIMS, B_ROPE_THETA)
    rest = HEAD_DIM - B_ROPE_DIMS
    ca = jnp.concatenate([cr, cr, cc, cc] * 2, axis=-1)
    sa = jnp.concatenate([-sr, sr, -sc, sc] * 2, axis=-1)
    cb = jnp.concatenate([cb, cb, jnp.ones((s, rest), F32)] * 2, axis=-1)
    sb = jnp.concatenate([-sb, sb, jnp.zeros((s, rest), F32)] * 2, axis=-1)
    return ca, sa, cb, sb


def _even_layer(x, mod, pre_g, post_g, w_in, w_out, qn, kn, tabs, ex):
    bsz, s, _ = x.shape
    shift, scale, gate = mod
    qa, ka, va, gates, qkv_b = _inproj_even(x, scale, shift, pre_g, w_in, qn, kn, tabs)
    oa = _attn_a(qa, ka, va)
    obs, lses = [], []
    for n, dil in enumerate(B_DILATIONS):
        q, k, v = (t[n] for t in qkv_b)
        if dil == 1:
            lq = min(LQ_B, s)
            q, k, v = (t.reshape(bsz, s // lq, 1, lq, t.shape[-1]) for t in (q, k, v))
        o, lse = _attn_b(q, k, v, dil)
        if dil == 1:
            o, lse = o.reshape(bsz, s, o.shape[-1]), lse.reshape(bsz, s, LANES)
        obs.append(o)
        lses.append(lse)
    return _outproj_even(oa, obs, lses, gates, ex, w_out, x, gate, post_g)


def _odd_layer(x, mod, pre_g, post_g, w_in, w_out, bias, lin, wc, dft):
    shift, scale, gate = mod
    qc, kc, vc, gates, vd = _inproj_odd(x, scale, shift, pre_g, w_in, wc)
    oc = _attn_c(qc, kc, vc, bias)
    f = _fourier(vd, dft)
    return _outproj_odd(oc, f, gates, lin, w_out, x, gate, post_g)


def _trunk(x, mods, pre_g, post_g, w_in_ab, w_out_ab, qn_a, kn_a, w_in_cd, w_out_cd, biases, lin_d):
    bsz, s, d = x.shape
    assert s % TM_PROJ == 0 and s % (16 * SUB_B) == 0
    tabs = _rope_tables(s)
    dft = _dft_consts(s)
    wc = _chan_dft_const()
    lane = np.arange(LANES)
    ex = jnp.asarray(lane[:, None] == np.arange(512)[None, :] // HEAD_DIM, BF16)
    for i in range(DEPTH):
        j = i // 2
        mod = tuple(mods[i][:, None, k * d:(k + 1) * d] for k in range(3))
        pg, qg = pre_g[i][None, :], post_g[i][None, :]
        if i % 2 == 0:
            qn = jnp.tile(qn_a[j], 2)[None, :]
            kn = jnp.tile(kn_a[j], 2)[None, :]
            x = _even_layer(x, mod, pg, qg, w_in_ab[j], w_out_ab[j], qn, kn, tabs, ex)
        else:
            x = _odd_layer(x, mod, pg, qg, w_in_cd[j], w_out_cd[j], biases[j], lin_d[j], wc, dft)
    return x


def kernel(x_prompt, x_sample, c_prompt, c_sample, pre_g, post_g, ada_w, ada_b,
           w_in_ab, w_out_ab, qn_a, kn_a, w_in_cd, w_out_cd, rpb_c, lin_d):
    nb = x_prompt.shape[0]
    mods = _adaln(jnp.concatenate([c_prompt, c_sample], axis=0), ada_w, ada_b)
    biases = [_bias_c(rpb_c[j]) for j in range(rpb_c.shape[0])]
    args = (pre_g, post_g, w_in_ab.astype(BF16), w_out_ab.astype(BF16), qn_a, kn_a,
            w_in_cd.astype(BF16), w_out_cd.astype(BF16), biases, lin_d.astype(BF16))
    y_prompt = _trunk(x_prompt, mods[:, :nb], *args)
    y_sample = _trunk(x_sample, mods[:, nb:], *args)
    return (y_prompt, y_sample)
```

```python
import functools
import math

import numpy as np
import jax
import jax.numpy as jnp
from jax import lax
from jax.experimental import pallas as pl
from jax.experimental.pallas import tpu as pltpu

F32 = jnp.float32
BF16 = jnp.bfloat16

D_MODEL = 1024
DEPTH = 4
HEAD_DIM = 64
GRID_W = 64
A_ROPE_THETA = 10000.0
B_ROPE_THETA = 500000.0
B_ROPE_DIMS = 16
B_DILATIONS = (1, 4, 16)
B_RADIUS = 64
C_HEADS = 12
C_WIN_H = 8
C_WIN_W = 16
D_GROUPS = 4
D_GROUP_DIM = 64
D_WIDTH = D_GROUPS * D_GROUP_DIM
AB_IN = 3328
CD_IN = 3584
EPS = 1e-6
NEG = -1e30
LOG2E = 1.4426950408889634
QK_SCALE = LOG2E * HEAD_DIM ** -0.5

LANES = 128
VMEM_LIMIT = 56 * 1024 * 1024

TM_PROJ = 512
T_RES = 256
TQ_A = 256
TK_A = 2048
LQ_B = 256
SUB_B = 128
ROWS_C = 4
TN_D1 = 4096
TK2_D = 8
LSE_PARTS = 3


def _params(*sem):
    return pltpu.CompilerParams(dimension_semantics=sem, vmem_limit_bytes=VMEM_LIMIT)


def _silu(x):
    return x / (1.0 + jnp.exp(-x))


def _lane_iota(rows):
    return lax.broadcasted_iota(jnp.int32, (rows, LANES), 1)


def _adaln_kernel(c_ref, w_ref, b_ref, o_ref):
    a = _silu(c_ref[...]).astype(BF16)
    o_ref[...] = jnp.dot(a, w_ref[...].astype(BF16), preferred_element_type=F32) + b_ref[...]


def _adaln(c_all, ada_w, ada_b):
    bt = c_all.shape[0]
    d = D_MODEL
    return pl.pallas_call(
        _adaln_kernel,
        grid=(DEPTH, 3),
        in_specs=[
            pl.BlockSpec((bt, d), lambda l, j: (0, 0)),
            pl.BlockSpec((None, d, d), lambda l, j: (l, 0, j)),
            pl.BlockSpec((None, 1, d), lambda l, j: (l, 0, j)),
        ],
        out_specs=pl.BlockSpec((None, bt, d), lambda l, j: (l, 0, j)),
        out_shape=jax.ShapeDtypeStruct((DEPTH, bt, 3 * d), F32),
        compiler_params=_params("parallel", "parallel"),
        name="adaln",
    )(c_all, ada_w, ada_b.reshape(DEPTH, 1, 3 * d))


def _modulated_norm(x, g, scale, shift):
    ms = jnp.mean(x * x, axis=-1, keepdims=True)
    y = x * lax.rsqrt(ms + EPS) * g
    return (y * (1.0 + scale) + shift).astype(BF16)


def _head_norm(xs, gain, lo):
    x2 = xs * xs
    s_lo = jnp.sum(jnp.where(lo, x2, 0.0), axis=1, keepdims=True)
    s_hi = jnp.sum(jnp.where(lo, 0.0, x2), axis=1, keepdims=True)
    ss = jnp.where(lo, s_lo, s_hi)
    return xs * lax.rsqrt(ss * (1.0 / HEAD_DIM) + EPS) * gain


def _rope(xs, cos, sin_signed, first_half, shift):
    up = pltpu.roll(xs, LANES - shift, 1)
    dn = pltpu.roll(xs, shift, 1)
    return xs * cos + jnp.where(first_half, up, dn) * sin_signed


def _inproj_even_kernel(x_ref, sc_ref, sh_ref, g_ref, w_ref, qn_ref, kn_ref,
                        ca_ref, sa_ref, cb_ref, sb_ref, p4_ref, p16_ref,
                        qa_ref, ka_ref, va_ref, gate_ref, *b_refs):
    tm = x_ref.shape[0]
    h = _modulated_norm(x_ref[...], g_ref[...], sc_ref[...], sh_ref[...])
    lane = _lane_iota(tm)
    half_a = (lane & 16) == 0
    half_b = (lane & 8) == 0
    lo = lane < HEAD_DIM
    ca, sa, cb, sb = ca_ref[...], sa_ref[...], cb_ref[...], sb_ref[...]
    full = jnp.dot(h, w_ref[...], preferred_element_type=F32)

    def proj(a, b):
        return full[:, a:b]

    def emit_b(vals):
        for n, val in enumerate(vals):
            b_refs[3 * n][...] = val
        cat = jnp.concatenate(vals, axis=1)
        for d, perm_ref in enumerate((p4_ref, p16_ref)):
            for t in range(tm // T_RES):
                pv = jnp.dot(perm_ref[...], cat[t * T_RES:(t + 1) * T_RES], preferred_element_type=F32)
                for n in range(3):
                    o_ref = b_refs[3 * n + 1 + d]
                    o_ref[t] = pv[:, 512 * n:512 * (n + 1)].astype(BF16).reshape(o_ref.shape[1:])

    p = proj(0, 512)
    lane_half = lane // HEAD_DIM
    for j in range(4):
        xs = _head_norm(p[:, LANES * j:LANES * (j + 1)], qn_ref[...], lo)
        xs = _rope(xs, ca, sa, half_a, 16) * QK_SCALE
        swapped = pltpu.roll(xs, HEAD_DIM, 1)
        for half in range(2):
            hd = 2 * j + half
            g, hh = hd // 4, hd % 4
            both = jnp.where(lane_half == half, xs, swapped)
            ht = jnp.where(lane_half == g, both, 0.0).T.astype(BF16)
            for t in range(tm // TQ_A):
                qa_ref[g, t, :, hh * TQ_A:(hh + 1) * TQ_A] = ht[:, t * TQ_A:(t + 1) * TQ_A]
    xs = _head_norm(proj(512, 640), kn_ref[...], lo)
    ka_ref[...] = _rope(xs, ca, sa, half_a, 16).astype(BF16)
    v = proj(640, 768)
    va_ref[0] = jnp.where(lo, v, 1.0).T.astype(BF16)
    va_ref[1] = jnp.where(lo, 1.0, v).T.astype(BF16)
    gate_ref[:, 0:512] = _silu(proj(768, 1280)).astype(BF16)
    p = proj(1280, 1792)
    qb = [(_rope(p[:, LANES * j:LANES * (j + 1)], cb, sb, half_b, 8) * QK_SCALE).astype(BF16)
          for j in range(4)]
    p = proj(1792, 2304)
    kb = [_rope(p[:, LANES * j:LANES * (j + 1)], cb, sb, half_b, 8).astype(BF16) for j in range(4)]
    emit_b([jnp.concatenate(qb, axis=1), jnp.concatenate(kb, axis=1), proj(2304, 2816).astype(BF16)])
    gate_ref[:, 512:1024] = _silu(proj(2816, 3328)).astype(BF16)


def _tok_spec(tm, width):
    return pl.BlockSpec((None, tm, width), lambda b, i: (b, i, 0))


def _res_spec(tm, dil, width):
    return pl.BlockSpec((None, tm // T_RES, dil, T_RES // dil, width), lambda b, i: (b, i, 0, 0, 0))


def _bcast_spec(width):
    return pl.BlockSpec((None, 1, width), lambda b, i: (b, 0, 0))


def _const_spec(shape):
    return pl.BlockSpec(shape, lambda b, i: (0,) * len(shape))


def _residue_perm(tm, dil):
    r = np.arange(tm)
    src = (r % (tm // dil)) * dil + r // (tm // dil)
    return np.asarray(r[None, :] == src[:, None], np.float32)


def _inproj_even(x, scale, shift, pre_g, w_in, qn, kn, tabs):
    bsz, s, d = x.shape
    tm = TM_PROJ
    ca, sa, cb, sb = tabs
    tab_spec = pl.BlockSpec((tm, LANES), lambda b, i: (i, 0))
    b_specs, b_shapes = [], []
    for _ in range(3):
        b_specs.append(_tok_spec(tm, 512))
        b_shapes.append(jax.ShapeDtypeStruct((bsz, s, 512), BF16))
        for dil in B_DILATIONS[1:]:
            b_specs.append(_res_spec(tm, dil, 512))
            b_shapes.append(jax.ShapeDtypeStruct((bsz, s // T_RES, dil, T_RES // dil, 512), BF16))
    outs = pl.pallas_call(
        _inproj_even_kernel,
        grid=(bsz, s // tm),
        in_specs=[
            _tok_spec(tm, d), _bcast_spec(d), _bcast_spec(d), _const_spec((1, d)),
            _const_spec((d, AB_IN)), _const_spec((1, LANES)), _const_spec((1, LANES)),
            tab_spec, tab_spec, tab_spec, tab_spec,
            _const_spec((T_RES, T_RES)), _const_spec((T_RES, T_RES)),
        ],
        out_specs=[pl.BlockSpec((None, 2, tm // TQ_A, LANES, 4 * TQ_A), lambda b, i: (b, 0, i, 0, 0)),
                   _tok_spec(tm, LANES),
                   pl.BlockSpec((None, 2, LANES, tm), lambda b, i: (b, 0, 0, i)),
                   _tok_spec(tm, 1024)] + b_specs,
        out_shape=[jax.ShapeDtypeStruct((bsz, 2, s // TQ_A, LANES, 4 * TQ_A), BF16),
                   jax.ShapeDtypeStruct((bsz, s, LANES), BF16),
                   jax.ShapeDtypeStruct((bsz, 2, LANES, s), BF16),
                   jax.ShapeDtypeStruct((bsz, s, 1024), BF16)] + b_shapes,
        compiler_params=_params("parallel", "parallel"),
        name="inproj_even",
    )(x, scale, shift, pre_g, w_in, qn, kn, ca, sa, cb, sb,
      jnp.asarray(_residue_perm(T_RES, 4), BF16), jnp.asarray(_residue_perm(T_RES, 16), BF16))
    qa, ka, va, gates = outs[:4]
    qkv_b = [outs[4 + 3 * n:7 + 3 * n] for n in range(3)]
    return qa, ka, va, gates, qkv_b


def _inproj_odd_kernel(x_ref, sc_ref, sh_ref, g_ref, w_ref, wc_ref,
                       q_ref, k_ref, v_ref, gate_ref, vd_ref):
    h = _modulated_norm(x_ref[...], g_ref[...], sc_ref[...], sh_ref[...])
    full = jnp.dot(h, w_ref[...], preferred_element_type=F32)

    def proj(a, b):
        return full[:, a:b]

    for j in range(C_HEADS // 2):
        cols = slice(LANES * j, LANES * (j + 1))
        q_ref[cols, :] = (proj(0, 768)[:, cols] * QK_SCALE).T.astype(BF16)
        v_ref[cols, :] = (proj(1536, 2304)[:, cols] * 1.0).T.astype(BF16)
    k_ref[...] = proj(768, 1536).astype(BF16)
    gate_ref[:, 0:768] = _silu(proj(2304, 3072)).astype(BF16)
    u = proj(3072, 3328).astype(BF16)
    vc = jnp.dot(u, wc_ref[...], preferred_element_type=F32)
    vd_ref[0] = vc[:, 0:D_WIDTH].astype(BF16)
    vd_ref[1] = vc[:, D_WIDTH:2 * D_WIDTH].astype(BF16)
    gate_ref[:, 768:1024] = _silu(proj(3328, 3584)).astype(BF16)


def _inproj_odd(x, scale, shift, pre_g, w_in, wc):
    bsz, s, d = x.shape
    tm = TM_PROJ
    chan_spec = pl.BlockSpec((None, 768, tm), lambda b, i: (b, 0, i))
    return pl.pallas_call(
        _inproj_odd_kernel,
        grid=(bsz, s // tm),
        in_specs=[
            _tok_spec(tm, d), _bcast_spec(d), _bcast_spec(d), _const_spec((1, d)),
            _const_spec((d, CD_IN)), _const_spec((D_WIDTH, 2 * D_WIDTH)),
        ],
        out_specs=[chan_spec, _tok_spec(tm, 768), chan_spec, _tok_spec(tm, 1024),
                   pl.BlockSpec((None, 2, tm, D_WIDTH), lambda b, i: (b, 0, i, 0))],
        out_shape=[jax.ShapeDtypeStruct((bsz, 768, s), BF16), jax.ShapeDtypeStruct((bsz, s, 768), BF16),
                   jax.ShapeDtypeStruct((bsz, 768, s), BF16), jax.ShapeDtypeStruct((bsz, s, 1024), BF16),
                   jax.ShapeDtypeStruct((bsz, 2, s, D_WIDTH), BF16)],
        compiler_params=_params("parallel", "parallel"),
        name="inproj_odd",
    )(x, scale, shift, pre_g, w_in, wc)


def _attn_a_kernel(q_ref, k_ref, v_ref, o_ref, s_ref, *, tk):
    tq = o_ref.shape[0]
    s_len = k_ref.shape[0]
    g = pl.program_id(1)
    qt = q_ref[...]

    nchunk = s_len // tk
    m = jnp.full((1, 4 * tq), NEG, F32)
    acc = jnp.zeros((LANES, 4 * tq), F32)
    s_ref[0] = jnp.dot(k_ref[0:tk, :], qt, preferred_element_type=F32)
    for j in range(nchunk):
        if j + 1 < nchunk:
            s_ref[(j + 1) % 2] = jnp.dot(k_ref[(j + 1) * tk:(j + 2) * tk, :], qt, preferred_element_type=F32)
        s = s_ref[j % 2]
        vc = v_ref[:, j * tk:(j + 1) * tk]
        m_new = jnp.maximum(m, jnp.max(s, axis=0, keepdims=True))
        p = jnp.exp2(s - m_new).astype(BF16)
        acc = jnp.exp2(m - m_new) * acc + jnp.dot(vc, p, preferred_element_type=F32)
        m = m_new
    first = g == 0
    num = jnp.where(first, acc[0:HEAD_DIM], acc[HEAD_DIM:LANES])
    den = jnp.where(first, acc[HEAD_DIM:HEAD_DIM + 1], acc[0:1])
    o = num / den
    for sl in range(2):
        pair = jnp.concatenate([o[:, (2 * sl) * tq:(2 * sl + 1) * tq],
                                o[:, (2 * sl + 1) * tq:(2 * sl + 2) * tq]], axis=0)
        o_ref[:, LANES * sl:LANES * (sl + 1)] = pair.T.astype(BF16)


def _attn_a(q, k, v):
    bsz, s, _ = k.shape
    tq = TQ_A
    tk = min(TK_A, s // 2)
    return pl.pallas_call(
        functools.partial(_attn_a_kernel, tk=tk),
        grid=(bsz, 2, s // tq),
        in_specs=[
            pl.BlockSpec((None, None, None, LANES, 4 * tq), lambda b, g, i: (b, g, i, 0, 0)),
            pl.BlockSpec((None, s, LANES), lambda b, g, i: (b, 0, 0)),
            pl.BlockSpec((None, None, LANES, s), lambda b, g, i: (b, g, 0, 0)),
        ],
        out_specs=pl.BlockSpec((None, tq, 2 * LANES), lambda b, g, i: (b, i, g)),
        out_shape=jax.ShapeDtypeStruct((bsz, s, 512), BF16),
        scratch_shapes=[pltpu.VMEM((2, tk, 4 * tq), F32)],
        compiler_params=_params("parallel", "arbitrary", "arbitrary"),
        name="attn_a",
    )(q, k, v)


def _window(refs, start, size, cols, lq):
    rpt = refs[0].shape[1]
    pieces = []
    for n, r in enumerate(refs):
        for t in range(r.shape[0]):
            t0 = n * lq + t * rpt
            a, b = max(start, t0), min(start + size, t0 + rpt)
            if a < b:
                pieces.append(r[t, a - t0:b - t0, cols])
    return pieces[0] if len(pieces) == 1 else jnp.concatenate(pieces, axis=0)


def _store_rows(ref, r0, val, cols):
    rpt = ref.shape[1]
    n = val.shape[0]
    for t in range(ref.shape[0]):
        a, b = max(r0, t * rpt), min(r0 + n, (t + 1) * rpt)
        if a < b:
            ref[t, a - t * rpt:b - t * rpt, cols] = val[a - r0:b - r0]


def _attn_b_kernel(q_ref, kp_ref, kc_ref, kn_ref, vp_ref, vc_ref, vn_ref, o_ref, lse_ref, s_ref, *, l_len):
    lq = q_ref.shape[0] * q_ref.shape[1]
    sub = min(SUB_B, lq)
    base = pl.program_id(2) * lq
    lane = _lane_iota(sub)
    lo = lane < HEAD_DIM
    rq = lax.broadcasted_iota(jnp.int32, (sub, 2 * sub), 0)
    ck = lax.broadcasted_iota(jnp.int32, (sub, 2 * sub), 1)
    delta = ck - rq - sub // 2
    band = jnp.abs(delta) <= B_RADIUS
    items = [(sb, j) for sb in range(lq // sub) for j in range(4)]

    def scores(n):
        sb, j = items[n]
        r0 = sb * sub
        cols = slice(LANES * j, LANES * (j + 1))
        qs = _window((q_ref,), r0, sub, cols, lq).astype(F32)
        kw = _window((kp_ref, kc_ref, kn_ref), lq + r0 - sub // 2, 2 * sub, cols, lq)
        q2 = jnp.concatenate([jnp.where(lo, qs, 0.0), jnp.where(lo, 0.0, qs)], axis=0).astype(BF16)
        s_ref[n % 2] = lax.dot_general(q2, kw, (((1,), (1,)), ((), ())), preferred_element_type=F32)

    scores(0)
    for n, (sb, j) in enumerate(items):
        if n + 1 < len(items):
            scores(n + 1)
        r0 = sb * sub
        cols = slice(LANES * j, LANES * (j + 1))
        if j == 0:
            kpos = jnp.where(band, base + (r0 - sub // 2) + ck, -1)
            mask = jnp.where(jnp.logical_and(kpos >= 0, kpos < l_len), 0.0, NEG)
            mask2 = jnp.concatenate([mask, mask], axis=0)
            lse_tile = jnp.zeros((sub, LANES), F32)
        vw = _window((vp_ref, vc_ref, vn_ref), lq + r0 - sub // 2, 2 * sub, cols, lq)
        s = s_ref[n % 2] + mask2
        m = jnp.max(s, axis=1, keepdims=True)
        p = jnp.exp2(s - m)
        l = jnp.sum(p, axis=1, keepdims=True)
        o = jnp.dot(p.astype(BF16), vw, preferred_element_type=F32) * (1.0 / l)
        lse = m + jnp.log2(l)
        lse_tile = (lse_tile + jnp.where(lane == 2 * j, lse[0:sub], 0.0)
                    + jnp.where(lane == 2 * j + 1, lse[sub:2 * sub], 0.0))
        _store_rows(o_ref, r0, jnp.where(lo, o[0:sub], o[sub:2 * sub]).astype(BF16), cols)
        if j == 3:
            packed = jnp.zeros((sub, LANES), F32)
            rest = lse_tile
            for part in range(LSE_PARTS):
                term = rest.astype(BF16).astype(F32)
                rest = rest - term
                packed = packed + (pltpu.roll(term, 8 * part, 1) if part else term)
            _store_rows(lse_ref, r0, packed.astype(BF16), slice(None))


def _attn_b(q, k, v, dil):
    bsz, ntile, _, rpt, w = q.shape
    l_len = ntile * rpt
    lq = min(LQ_B, l_len)
    nt = lq // rpt
    nblk = l_len // lq

    def spec(width, shift):
        def index(b, r, i):
            return (b, jnp.clip(i + shift, 0, nblk - 1), r, 0, 0)
        return pl.BlockSpec((None, nt, None, rpt, width), index)

    cur, prv, nxt = spec(w, 0), spec(w, -1), spec(w, 1)
    return pl.pallas_call(
        functools.partial(_attn_b_kernel, l_len=l_len),
        grid=(bsz, dil, nblk),
        in_specs=[cur, prv, cur, nxt, prv, cur, nxt],
        out_specs=[cur, spec(LANES, 0)],
        out_shape=[jax.ShapeDtypeStruct(q.shape, BF16),
                   jax.ShapeDtypeStruct((bsz, ntile, dil, rpt, LANES), BF16)],
        scratch_shapes=[pltpu.VMEM((2, 2 * min(SUB_B, lq), 2 * min(SUB_B, lq)), F32)],
        compiler_params=_params("parallel", "parallel", "parallel"),
        name=f"attn_b_d{dil}",
    )(q, k, k, k, v, v, v)


def _attn_c_kernel(q_ref, kp_ref, kc_ref, kn_ref, vp_ref, vc_ref, vn_ref, bias_ref, o_ref, s_ref):
    tq = o_ref.shape[0]
    top = lax.broadcasted_iota(jnp.int32, (LANES, tq), 0) < HEAD_DIM
    npair = C_HEADS // 2

    def scores(pr):
        cols = slice(LANES * pr, LANES * (pr + 1))
        qs = q_ref[cols, :].astype(F32)
        q2 = jnp.concatenate([jnp.where(top, qs, 0.0), jnp.where(top, 0.0, qs)], axis=1).astype(BF16)
        k3 = jnp.concatenate([kp_ref[:, cols], kc_ref[:, cols], kn_ref[:, cols]], axis=0)
        s_ref[pr % 2] = jnp.dot(k3, q2, preferred_element_type=F32)

    scores(0)
    for pr in range(npair):
        if pr + 1 < npair:
            scores(pr + 1)
        cols = slice(LANES * pr, LANES * (pr + 1))
        v3 = jnp.concatenate([vp_ref[cols, :], vc_ref[cols, :], vn_ref[cols, :]], axis=1)
        s = s_ref[pr % 2] + bias_ref[pr]
        m = jnp.max(s, axis=0, keepdims=True)
        p = jnp.exp2(s - m)
        l = jnp.sum(p, axis=0, keepdims=True)
        o = jnp.dot(v3, p.astype(BF16), preferred_element_type=F32) * (1.0 / l)
        o_ref[:, cols] = jnp.where(top, o[:, 0:tq], o[:, tq:2 * tq]).T.astype(BF16)


def _bias_c(rpb):
    nq, nu = ROWS_C, 3 * ROWS_C
    assert nq == C_WIN_H // 2
    nrel = 2 * C_WIN_H - 1
    scaled = rpb.astype(F32) * LOG2E
    per_col = []
    for c in range(GRID_W):
        c0 = min(max(c - C_WIN_W // 2, 0), GRID_W - C_WIN_W)
        win = scaled[:, :, c0 - c + C_WIN_W - 1:c0 - c + 2 * C_WIN_W - 1]
        per_col.append(jnp.pad(win, ((0, 0), (0, 0), (c0, GRID_W - C_WIN_W - c0)), constant_values=NEG))
    tab = jnp.stack(per_col, axis=2)
    dead = jnp.full((C_HEADS, GRID_W, GRID_W), NEG, F32)
    kinds = []
    for kind in range(3):
        rows = []
        for j in range(nq):
            slots = []
            for u in range(nu):
                rel = (u - nq) - j + C_WIN_H - 1
                if kind == 0:
                    ok = nq <= u < nq + C_WIN_H
                elif kind == 2:
                    ok = 2 * nq - C_WIN_H <= u < 2 * nq
                else:
                    ok = nq - C_WIN_H // 2 <= u - j < nq + C_WIN_H // 2
                slots.append(tab[:, rel] if ok and 0 <= rel < nrel else dead)
            rows.append(jnp.concatenate(slots, axis=2))
        kinds.append(jnp.concatenate(rows, axis=1))
    tab = jnp.stack(kinds, axis=0)
    tq, tk = nq * GRID_W, nu * GRID_W
    tab = tab.reshape(3, C_HEADS // 2, 2, tq, tk).transpose(0, 1, 4, 2, 3)
    return tab.reshape(3, C_HEADS // 2, tk, 2 * tq)


def _attn_c(q, k, v, bias):
    bsz, s, w = k.shape
    tq = ROWS_C * GRID_W
    nblk = s // tq
    assert nblk >= 2
    kind = lambda i: jnp.where(i == 0, 0, jnp.where(i == nblk - 1, 2, 1))

    def specs(chan_major):
        def make(shift):
            blk = lambda i: jnp.clip(i + shift, 0, nblk - 1)
            if chan_major:
                return pl.BlockSpec((None, w, tq), lambda b, i: (b, 0, blk(i)))
            return pl.BlockSpec((None, tq, w), lambda b, i: (b, blk(i), 0))
        return make(-1), make(0), make(1)

    kprv, cur, knxt = specs(False)
    vprv, vcur, vnxt = specs(True)
    return pl.pallas_call(
        _attn_c_kernel,
        grid=(bsz, nblk),
        in_specs=[vcur, kprv, cur, knxt, vprv, vcur, vnxt,
                  pl.BlockSpec((None, C_HEADS // 2, 3 * tq, 2 * tq), lambda b, i: (kind(i), 0, 0, 0))],
        out_specs=cur,
        out_shape=jax.ShapeDtypeStruct((bsz, s, w), BF16),
        scratch_shapes=[pltpu.VMEM((2, 3 * tq, 2 * tq), F32)],
        compiler_params=_params("parallel", "arbitrary"),
        name="attn_c",
    )(q, k, k, k, v, v, v, bias)


S2_D = 64


def _dft_consts(s):
    s1 = s // S2_D
    th2 = 2 * np.pi * np.outer(np.arange(S2_D), np.arange(S2_D)) / S2_D
    c2, n2 = np.cos(th2) / 8.0, np.sin(th2) / 8.0
    w2big = np.block([[c2, n2], [-n2, c2]])
    th1 = 2 * np.pi * np.outer(np.arange(s1), np.arange(s1)) / s1
    w1cat = np.concatenate([np.cos(th1), np.sin(th1)], axis=1) / math.sqrt(s1)
    tht = 2 * np.pi * np.outer(np.arange(S2_D), np.arange(s1)) / s
    return (jnp.asarray(w2big, BF16), jnp.asarray(w1cat, BF16),
            jnp.asarray(np.cos(tht), F32), jnp.asarray(-np.sin(tht), F32))


def _chan_dft_const():
    th = 2 * np.pi * np.outer(np.arange(D_GROUP_DIM), np.arange(D_GROUP_DIM)) / D_GROUP_DIM
    eye = np.eye(D_GROUPS)
    wc = np.concatenate([np.kron(eye, np.cos(th)), np.kron(eye, -np.sin(th))], axis=1) / 8.0
    return jnp.asarray(wc, BF16)


def _dft1_kernel(v_ref, w_ref, tr_ref, ti_ref, o_ref):
    x = jnp.concatenate([v_ref[0], v_ref[1]], axis=0)
    b = jnp.dot(w_ref[...], x, preferred_element_type=F32)
    br, bi = b[0:S2_D], b[S2_D:2 * S2_D]
    tr, ti = tr_ref[...], ti_ref[...]
    o_ref[0] = (br * tr - bi * ti).astype(BF16)
    o_ref[1] = (br * ti + bi * tr).astype(BF16)


def _dft2_kernel(b_ref, w_ref, o_ref):
    for kk in range(b_ref.shape[1]):
        rhs = jnp.concatenate([b_ref[0, kk], b_ref[1, kk]], axis=0)
        x = jnp.dot(w_ref[...], rhs, preferred_element_type=F32)
        o_ref[:, D_WIDTH * kk:D_WIDTH * (kk + 1)] = x.astype(BF16)


def _fourier(vd, consts):
    bsz, _, s, w = vd.shape
    s1 = s // S2_D
    w2big, w1cat, tr, ti = consts
    ncol = s1 * w
    tn = min(TN_D1, ncol)
    twr = jnp.repeat(tr, w, axis=1)
    twi = jnp.repeat(ti, w, axis=1)
    b1 = pl.pallas_call(
        _dft1_kernel,
        grid=(ncol // tn, bsz),
        in_specs=[
            pl.BlockSpec((None, 2, S2_D, tn), lambda j, b: (b, 0, 0, j)),
            pl.BlockSpec((2 * S2_D, 2 * S2_D), lambda j, b: (0, 0)),
            pl.BlockSpec((S2_D, tn), lambda j, b: (0, j)),
            pl.BlockSpec((S2_D, tn), lambda j, b: (0, j)),
        ],
        out_specs=pl.BlockSpec((None, 2, S2_D, tn), lambda j, b: (b, 0, 0, j)),
        out_shape=jax.ShapeDtypeStruct((bsz, 2, S2_D, ncol), BF16),
        compiler_params=_params("parallel", "parallel"),
        name="dft_stage1",
    )(vd.reshape(bsz, 2, S2_D, ncol), w2big, twr, twi)
    tk2 = TK2_D
    f = pl.pallas_call(
        _dft2_kernel,
        grid=(bsz, S2_D // tk2),
        in_specs=[
            pl.BlockSpec((None, 2, tk2, s1, w), lambda b, j: (b, 0, j, 0, 0)),
            pl.BlockSpec((s1, 2 * s1), lambda b, j: (0, 0)),
        ],
        out_specs=pl.BlockSpec((None, s1, tk2 * w), lambda b, j: (b, 0, j)),
        out_shape=jax.ShapeDtypeStruct((bsz, s1, S2_D * w), BF16),
        compiler_params=_params("parallel", "parallel"),
        name="dft_stage2",
    )(b1.reshape(bsz, 2, S2_D, s1, w), w1cat)
    return f.reshape(bsz, s, w)


def _finish(y, x_ref, gate_ref, g_ref, o_ref):
    ms = jnp.mean(y * y, axis=-1, keepdims=True)
    yn = y * lax.rsqrt(ms + EPS) * g_ref[...]
    o_ref[...] = x_ref[...] + gate_ref[...] * yn


def _outproj_even_kernel(oa_ref, o1_ref, o4_ref, o16_ref, l1_ref, l4_ref, l16_ref, gates_ref,
                         ex_ref, p4_ref, p16_ref, w_ref, x_ref, gate_ref, g_ref, o_ref):
    tm = x_ref.shape[0]
    ex = ex_ref[...]

    def natural(o_res, l_res, perm_ref):
        tiles = []
        for t in range(tm // T_RES):
            both = jnp.concatenate([o_res[t].reshape(T_RES, 512), l_res[t].reshape(T_RES, LANES)], axis=1)
            tiles.append(jnp.dot(perm_ref[...], both, preferred_element_type=F32))
        nat = jnp.concatenate(tiles, axis=0)
        return nat[:, 0:512], nat[:, 512:512 + LANES]

    def lse(x):
        out = x
        for part in range(1, LSE_PARTS):
            out = out + pltpu.roll(x, LANES - 8 * part, 1)
        return out

    o2, l2 = natural(o4_ref, l4_ref, p4_ref)
    o3, l3 = natural(o16_ref, l16_ref, p16_ref)
    l1, l2, l3 = lse(l1_ref[...].astype(F32)), lse(l2), lse(l3)
    mx = jnp.maximum(jnp.maximum(l1, l2), l3)
    e1, e2, e3 = jnp.exp2(l1 - mx), jnp.exp2(l2 - mx), jnp.exp2(l3 - mx)
    inv = 1.0 / (e1 + e2 + e3)
    terms = []
    for wgt in (e1 * inv, e2 * inv, e3 * inv):
        hi = wgt.astype(BF16)
        terms += [hi, (wgt - hi.astype(F32)).astype(BF16)]
    wide = jnp.dot(jnp.concatenate(terms, axis=0), ex, preferred_element_type=F32)
    w1, w2, w3 = (wide[2 * n * tm:(2 * n + 1) * tm] + wide[(2 * n + 1) * tm:(2 * n + 2) * tm] for n in range(3))
    ob = w1 * o1_ref[...].astype(F32) + w2 * o2 + w3 * o3
    ma = (oa_ref[...].astype(F32) * gates_ref[:, 0:512].astype(F32)).astype(BF16)
    mb = (ob * gates_ref[:, 512:1024].astype(F32)).astype(BF16)
    y = jnp.dot(jnp.concatenate([ma, mb], axis=1), w_ref[...], preferred_element_type=F32)
    _finish(y, x_ref, gate_ref, g_ref, o_ref)


def _outproj_even(oa, obs, lses, gates, ex, w_out, x, gate, post_g):
    bsz, s, d = x.shape
    tm = TM_PROJ
    return pl.pallas_call(
        _outproj_even_kernel,
        grid=(bsz, s // tm),
        in_specs=[_tok_spec(tm, 512), _tok_spec(tm, 512), _res_spec(tm, 4, 512), _res_spec(tm, 16, 512),
                  _tok_spec(tm, LANES), _res_spec(tm, 4, LANES), _res_spec(tm, 16, LANES),
                  _tok_spec(tm, 1024), _const_spec((LANES, 512)),
                  _const_spec((T_RES, T_RES)), _const_spec((T_RES, T_RES)), _const_spec((d, d)),
                  _tok_spec(tm, d), _bcast_spec(d), _const_spec((1, d))],
        out_specs=_tok_spec(tm, d),
        out_shape=jax.ShapeDtypeStruct((bsz, s, d), F32),
        compiler_params=_params("parallel", "parallel"),
        name="outproj_even",
    )(oa, *obs, *lses, gates, ex,
      jnp.asarray(_residue_perm(T_RES, 4).T, BF16), jnp.asarray(_residue_perm(T_RES, 16).T, BF16),
      w_out, x, gate, post_g)


def _outproj_odd_kernel(oc_ref, f_ref, gates_ref, lin_ref, w_ref, x_ref, gate_ref, g_ref, o_ref):
    od = jnp.dot(f_ref[...], lin_ref[...], preferred_element_type=F32)
    mc = (oc_ref[...].astype(F32) * gates_ref[:, 0:768].astype(F32)).astype(BF16)
    md = (od * gates_ref[:, 768:1024].astype(F32)).astype(BF16)
    y = jnp.dot(jnp.concatenate([mc, md], axis=1), w_ref[...], preferred_element_type=F32)
    _finish(y, x_ref, gate_ref, g_ref, o_ref)


def _outproj_odd(oc, f, gates, lin, w_out, x, gate, post_g):
    bsz, s, d = x.shape
    tm = TM_PROJ
    return pl.pallas_call(
        _outproj_odd_kernel,
        grid=(bsz, s // tm),
        in_specs=[_tok_spec(tm, 768), _tok_spec(tm, D_WIDTH), _tok_spec(tm, 1024),
                  _const_spec((D_WIDTH, D_WIDTH)), _const_spec((d, d)),
                  _tok_spec(tm, d), _bcast_spec(d), _const_spec((1, d))],
        out_specs=_tok_spec(tm, d),
        out_shape=jax.ShapeDtypeStruct((bsz, s, d), F32),
        compiler_params=_params("parallel", "parallel"),
        name="outproj_odd",
    )(oc, f, gates, lin, w_out, x, gate, post_g)


def _rope_tables(s):
    t = jnp.arange(s)

    def tab(pos, dim, theta):
        inv = theta ** (-jnp.arange(0, dim, 2, dtype=F32) / dim)
        ang = pos[:, None] * inv[None, :]
        return jnp.cos(ang), jnp.sin(ang)

    cr, sr = tab((t // GRID_W).astype(F32), HEAD_DIM // 2, A_ROPE_THETA)
    cc, sc = tab((t % GRID_W).astype(F32), HEAD_DIM // 2, A_ROPE_THETA)
    cb, sb = tab(t.astype(F32), B_ROPE_DIMS, B_ROPE_THETA)
    rest = HEAD_DIM - B_ROPE_DIMS
    ca = jnp.concatenate([cr, cr, cc, cc] * 2, axis=-1)
    sa = jnp.concatenate([-sr, sr, -sc, sc] * 2, axis=-1)
    cb = jnp.concatenate([cb, cb, jnp.ones((s, rest), F32)] * 2, axis=-1)
    sb = jnp.concatenate([-sb, sb, jnp.zeros((s, rest), F32)] * 2, axis=-1)
    return ca, sa, cb, sb


def _even_layer(x, mod, pre_g, post_g, w_in, w_out, qn, kn, tabs, ex):
    bsz, s, _ = x.shape
    shift, scale, gate = mod
    qa, ka, va, gates, qkv_b = _inproj_even(x, scale, shift, pre_g, w_in, qn, kn, tabs)
    oa = _attn_a(qa, ka, va)
    obs, lses = [], []
    for n, dil in enumerate(B_DILATIONS):
        q, k, v = (t[n] for t in qkv_b)
        if dil == 1:
            lq = min(LQ_B, s)
            q, k, v = (t.reshape(bsz, s // lq, 1, lq, t.shape[-1]) for t in (q, k, v))
        o, lse = _attn_b(q, k, v, dil)
        if dil == 1:
            o, lse = o.reshape(bsz, s, o.shape[-1]), lse.reshape(bsz, s, LANES)
        obs.append(o)
        lses.append(lse)
    return _outproj_even(oa, obs, lses, gates, ex, w_out, x, gate, post_g)


def _odd_layer(x, mod, pre_g, post_g, w_in, w_out, bias, lin, wc, dft):
    shift, scale, gate = mod
    qc, kc, vc, gates, vd = _inproj_odd(x, scale, shift, pre_g, w_in, wc)
    oc = _attn_c(qc, kc, vc, bias)
    f = _fourier(vd, dft)
    return _outproj_odd(oc, f, gates, lin, w_out, x, gate, post_g)


def _trunk(x, mods, pre_g, post_g, w_in_ab, w_out_ab, qn_a, kn_a, w_in_cd, w_out_cd, biases, lin_d):
    bsz, s, d = x.shape
    assert s % TM_PROJ == 0 and s % (16 * SUB_B) == 0
    tabs = _rope_tables(s)
    dft = _dft_consts(s)
    wc = _chan_dft_const()
    lane = np.arange(LANES)
    ex = jnp.asarray(lane[:, None] == np.arange(512)[None, :] // HEAD_DIM, BF16)
    for i in range(DEPTH):
        j = i // 2
        mod = tuple(mods[i][:, None, k * d:(k + 1) * d] for k in range(3))
        pg, qg = pre_g[i][None, :], post_g[i][None, :]
        if i % 2 == 0:
            qn = jnp.tile(qn_a[j], 2)[None, :]
            kn = jnp.tile(kn_a[j], 2)[None, :]
            x = _even_layer(x, mod, pg, qg, w_in_ab[j], w_out_ab[j], qn, kn, tabs, ex)
        else:
            x = _odd_layer(x, mod, pg, qg, w_in_cd[j], w_out_cd[j], biases[j], lin_d[j], wc, dft)
    return x


def kernel(x_prompt, x_sample, c_prompt, c_sample, pre_g, post_g, ada_w, ada_b,
           w_in_ab, w_out_ab, qn_a, kn_a, w_in_cd, w_out_cd, rpb_c, lin_d):
    nb = x_prompt.shape[0]
    mods = _adaln(jnp.concatenate([c_prompt, c_sample], axis=0), ada_w, ada_b)
    biases = [_bias_c(rpb_c[j]) for j in range(rpb_c.shape[0])]
    args = (pre_g, post_g, w_in_ab.astype(BF16), w_out_ab.astype(BF16), qn_a, kn_a,
            w_in_cd.astype(BF16), w_out_cd.astype(BF16), biases, lin_d.astype(BF16))
    y_prompt = _trunk(x_prompt, mods[:, :nb], *args)
    y_sample = _trunk(x_sample, mods[:, nb:], *args)
    return (y_prompt, y_sample)
```

```python
import functools
import math

import numpy as np
import jax
import jax.numpy as jnp
from jax import lax
from jax.experimental import pallas as pl
from jax.experimental.pallas import tpu as pltpu

F32 = jnp.float32
BF16 = jnp.bfloat16

D_MODEL = 1024
DEPTH = 4
HEAD_DIM = 64
GRID_W = 64
A_ROPE_THETA = 10000.0
B_ROPE_THETA = 500000.0
B_ROPE_DIMS = 16
B_DILATIONS = (1, 4, 16)
B_RADIUS = 64
C_HEADS = 12
C_WIN_H = 8
C_WIN_W = 16
D_GROUPS = 4
D_GROUP_DIM = 64
D_WIDTH = D_GROUPS * D_GROUP_DIM
AB_IN = 3328
CD_IN = 3584
EPS = 1e-6
NEG = -1e30
LOG2E = 1.4426950408889634
QK_SCALE = LOG2E * HEAD_DIM ** -0.5

LANES = 128
VMEM_LIMIT = 56 * 1024 * 1024

TM_PROJ = 512
T_RES = 256
TQ_A = 256
TK_A = 2048
ITEMS_A = 8
LQ_B = 256
SUB_B = 128
ROWS_C = 4
TN_D1 = 4096
TK2_D = 8
LSE_PARTS = 3


def _params(*sem):
    return pltpu.CompilerParams(dimension_semantics=sem, vmem_limit_bytes=VMEM_LIMIT)


def _silu(x):
    return x / (1.0 + jnp.exp(-x))


def _lane_iota(rows):
    return lax.broadcasted_iota(jnp.int32, (rows, LANES), 1)


def _adaln_kernel(c_ref, w_ref, b_ref, o_ref):
    a = _silu(c_ref[...]).astype(BF16)
    o_ref[...] = jnp.dot(a, w_ref[...].astype(BF16), preferred_element_type=F32) + b_ref[...]


def _adaln(c_all, ada_w, ada_b):
    bt = c_all.shape[0]
    d = D_MODEL
    return pl.pallas_call(
        _adaln_kernel,
        grid=(DEPTH, 3),
        in_specs=[
            pl.BlockSpec((bt, d), lambda l, j: (0, 0)),
            pl.BlockSpec((None, d, d), lambda l, j: (l, 0, j)),
            pl.BlockSpec((None, 1, d), lambda l, j: (l, 0, j)),
        ],
        out_specs=pl.BlockSpec((None, bt, d), lambda l, j: (l, 0, j)),
        out_shape=jax.ShapeDtypeStruct((DEPTH, bt, 3 * d), F32),
        compiler_params=_params("parallel", "parallel"),
        name="adaln",
    )(c_all, ada_w, ada_b.reshape(DEPTH, 1, 3 * d))


def _modulated_norm(x, g, scale, shift):
    ms = jnp.mean(x * x, axis=-1, keepdims=True)
    y = x * lax.rsqrt(ms + EPS) * g
    return (y * (1.0 + scale) + shift).astype(BF16)


def _head_norm(xs, gain, lo):
    x2 = xs * xs
    s_lo = jnp.sum(jnp.where(lo, x2, 0.0), axis=1, keepdims=True)
    s_hi = jnp.sum(jnp.where(lo, 0.0, x2), axis=1, keepdims=True)
    ss = jnp.where(lo, s_lo, s_hi)
    return xs * lax.rsqrt(ss * (1.0 / HEAD_DIM) + EPS) * gain


def _rope(xs, cos, sin_signed, first_half, shift):
    up = pltpu.roll(xs, LANES - shift, 1)
    dn = pltpu.roll(xs, shift, 1)
    return xs * cos + jnp.where(first_half, up, dn) * sin_signed


def _inproj_even_kernel(x_ref, sc_ref, sh_ref, g_ref, w_ref, qn_ref, kn_ref,
                        ca_ref, sa_ref, cb_ref, sb_ref, p4_ref, p16_ref,
                        qa_ref, ka_ref, va_ref, gate_ref, *b_refs):
    tm = x_ref.shape[0]
    h = _modulated_norm(x_ref[...], g_ref[...], sc_ref[...], sh_ref[...])
    lane = _lane_iota(tm)
    half_a = (lane & 16) == 0
    half_b = (lane & 8) == 0
    lo = lane < HEAD_DIM
    ca, sa, cb, sb = ca_ref[...], sa_ref[...], cb_ref[...], sb_ref[...]
    full = jnp.dot(h, w_ref[...], preferred_element_type=F32)

    def proj(a, b):
        return full[:, a:b]

    def emit_b(vals):
        for n, val in enumerate(vals):
            b_refs[3 * n][...] = val
        cat = jnp.concatenate(vals, axis=1)
        for d, perm_ref in enumerate((p4_ref, p16_ref)):
            for t in range(tm // T_RES):
                pv = jnp.dot(perm_ref[...], cat[t * T_RES:(t + 1) * T_RES], preferred_element_type=F32)
                for n in range(3):
                    o_ref = b_refs[3 * n + 1 + d]
                    o_ref[t] = pv[:, 512 * n:512 * (n + 1)].astype(BF16).reshape(o_ref.shape[1:])

    p = proj(0, 512)
    lane_half = lane // HEAD_DIM
    for j in range(4):
        xs = _head_norm(p[:, LANES * j:LANES * (j + 1)], qn_ref[...], lo)
        xs = _rope(xs, ca, sa, half_a, 16) * QK_SCALE
        swapped = pltpu.roll(xs, HEAD_DIM, 1)
        for half in range(2):
            hd = 2 * j + half
            g, hh = hd // 4, hd % 4
            both = jnp.where(lane_half == half, xs, swapped)
            ht = jnp.where(lane_half == g, both, 0.0).T.astype(BF16)
            for t in range(tm // TQ_A):
                qa_ref[g, t, :, hh * TQ_A:(hh + 1) * TQ_A] = ht[:, t * TQ_A:(t + 1) * TQ_A]
    xs = _head_norm(proj(512, 640), kn_ref[...], lo)
    ka_ref[...] = _rope(xs, ca, sa, half_a, 16).astype(BF16)
    v = proj(640, 768)
    va_ref[0] = jnp.where(lo, v, 1.0).T.astype(BF16)
    va_ref[1] = jnp.where(lo, 1.0, v).T.astype(BF16)
    gate_ref[:, 0:512] = _silu(proj(768, 1280)).astype(BF16)
    p = proj(1280, 1792)
    qb = [(_rope(p[:, LANES * j:LANES * (j + 1)], cb, sb, half_b, 8) * QK_SCALE).astype(BF16)
          for j in range(4)]
    p = proj(1792, 2304)
    kb = [_rope(p[:, LANES * j:LANES * (j + 1)], cb, sb, half_b, 8).astype(BF16) for j in range(4)]
    emit_b([jnp.concatenate(qb, axis=1), jnp.concatenate(kb, axis=1), proj(2304, 2816).astype(BF16)])
    gate_ref[:, 512:1024] = _silu(proj(2816, 3328)).astype(BF16)


def _tok_spec(tm, width):
    return pl.BlockSpec((None, tm, width), lambda b, i: (b, i, 0))


def _res_spec(tm, dil, width):
    return pl.BlockSpec((None, tm // T_RES, dil, T_RES // dil, width), lambda b, i: (b, i, 0, 0, 0))


def _bcast_spec(width):
    return pl.BlockSpec((None, 1, width), lambda b, i: (b, 0, 0))


def _const_spec(shape):
    return pl.BlockSpec(shape, lambda b, i: (0,) * len(shape))


def _residue_perm(tm, dil):
    r = np.arange(tm)
    src = (r % (tm // dil)) * dil + r // (tm // dil)
    return np.asarray(r[None, :] == src[:, None], np.float32)


def _inproj_even(x, scale, shift, pre_g, w_in, qn, kn, tabs):
    bsz, s, d = x.shape
    tm = TM_PROJ
    ca, sa, cb, sb = tabs
    tab_spec = pl.BlockSpec((tm, LANES), lambda b, i: (i, 0))
    b_specs, b_shapes = [], []
    for _ in range(3):
        b_specs.append(_tok_spec(tm, 512))
        b_shapes.append(jax.ShapeDtypeStruct((bsz, s, 512), BF16))
        for dil in B_DILATIONS[1:]:
            b_specs.append(_res_spec(tm, dil, 512))
            b_shapes.append(jax.ShapeDtypeStruct((bsz, s // T_RES, dil, T_RES // dil, 512), BF16))
    outs = pl.pallas_call(
        _inproj_even_kernel,
        grid=(bsz, s // tm),
        in_specs=[
            _tok_spec(tm, d), _bcast_spec(d), _bcast_spec(d), _const_spec((1, d)),
            _const_spec((d, AB_IN)), _const_spec((1, LANES)), _const_spec((1, LANES)),
            tab_spec, tab_spec, tab_spec, tab_spec,
            _const_spec((T_RES, T_RES)), _const_spec((T_RES, T_RES)),
        ],
        out_specs=[pl.BlockSpec((None, 2, tm // TQ_A, LANES, 4 * TQ_A), lambda b, i: (b, 0, i, 0, 0)),
                   _tok_spec(tm, LANES),
                   pl.BlockSpec((None, 2, LANES, tm), lambda b, i: (b, 0, 0, i)),
                   _tok_spec(tm, 1024)] + b_specs,
        out_shape=[jax.ShapeDtypeStruct((bsz, 2, s // TQ_A, LANES, 4 * TQ_A), BF16),
                   jax.ShapeDtypeStruct((bsz, s, LANES), BF16),
                   jax.ShapeDtypeStruct((bsz, 2, LANES, s), BF16),
                   jax.ShapeDtypeStruct((bsz, s, 1024), BF16)] + b_shapes,
        compiler_params=_params("parallel", "parallel"),
        name="inproj_even",
    )(x, scale, shift, pre_g, w_in, qn, kn, ca, sa, cb, sb,
      jnp.asarray(_residue_perm(T_RES, 4), BF16), jnp.asarray(_residue_perm(T_RES, 16), BF16))
    qa, ka, va, gates = outs[:4]
    qkv_b = [outs[4 + 3 * n:7 + 3 * n] for n in range(3)]
    return qa, ka, va, gates, qkv_b


def _inproj_odd_kernel(x_ref, sc_ref, sh_ref, g_ref, w_ref, wc_ref,
                       q_ref, k_ref, v_ref, gate_ref, vd_ref):
    h = _modulated_norm(x_ref[...], g_ref[...], sc_ref[...], sh_ref[...])
    full = jnp.dot(h, w_ref[...], preferred_element_type=F32)

    def proj(a, b):
        return full[:, a:b]

    q_ref[...] = (proj(0, 768) * QK_SCALE).astype(BF16)
    k_ref[...] = proj(768, 1536).astype(BF16)
    v_ref[...] = proj(1536, 2304).astype(BF16)
    gate_ref[:, 0:768] = _silu(proj(2304, 3072)).astype(BF16)
    u = proj(3072, 3328).astype(BF16)
    vc = jnp.dot(u, wc_ref[...], preferred_element_type=F32)
    vd_ref[0] = vc[:, 0:D_WIDTH].astype(BF16)
    vd_ref[1] = vc[:, D_WIDTH:2 * D_WIDTH].astype(BF16)
    gate_ref[:, 768:1024] = _silu(proj(3328, 3584)).astype(BF16)


def _inproj_odd(x, scale, shift, pre_g, w_in, wc):
    bsz, s, d = x.shape
    tm = TM_PROJ
    widths = (768, 768, 768, 1024)
    return pl.pallas_call(
        _inproj_odd_kernel,
        grid=(bsz, s // tm),
        in_specs=[
            _tok_spec(tm, d), _bcast_spec(d), _bcast_spec(d), _const_spec((1, d)),
            _const_spec((d, CD_IN)), _const_spec((D_WIDTH, 2 * D_WIDTH)),
        ],
        out_specs=[_tok_spec(tm, w) for w in widths]
        + [pl.BlockSpec((None, 2, tm, D_WIDTH), lambda b, i: (b, 0, i, 0))],
        out_shape=[jax.ShapeDtypeStruct((bsz, s, w), BF16) for w in widths]
        + [jax.ShapeDtypeStruct((bsz, 2, s, D_WIDTH), BF16)],
        compiler_params=_params("parallel", "parallel"),
        name="inproj_odd",
    )(x, scale, shift, pre_g, w_in, wc)


def _attn_a_kernel(q_ref, k_ref, v_ref, o_ref, s_ref, *, tk):
    ntile = q_ref.shape[0]
    tq = o_ref.shape[0] // ntile
    nchunk = k_ref.shape[0] // tk
    first = pl.program_id(1) == 0
    items = [(t, j) for t in range(ntile) for j in range(nchunk)]

    def scores(n):
        t, j = items[n]
        s_ref[n % 2] = jnp.dot(k_ref[j * tk:(j + 1) * tk, :], q_ref[t], preferred_element_type=F32)

    scores(0)
    for n, (t, j) in enumerate(items):
        if n + 1 < len(items):
            scores(n + 1)
        if j == 0:
            m = jnp.full((1, 4 * tq), NEG, F32)
            acc = jnp.zeros((LANES, 4 * tq), F32)
        s = s_ref[n % 2]
        vc = v_ref[:, j * tk:(j + 1) * tk]
        m_new = jnp.maximum(m, jnp.max(s, axis=0, keepdims=True))
        p = jnp.exp2(s - m_new).astype(BF16)
        acc = jnp.exp2(m - m_new) * acc + jnp.dot(vc, p, preferred_element_type=F32)
        m = m_new
        if j == nchunk - 1:
            num = jnp.where(first, acc[0:HEAD_DIM], acc[HEAD_DIM:LANES])
            den = jnp.where(first, acc[HEAD_DIM:HEAD_DIM + 1], acc[0:1])
            o = num / den
            for sl in range(2):
                pair = jnp.concatenate([o[:, (2 * sl) * tq:(2 * sl + 1) * tq],
                                        o[:, (2 * sl + 1) * tq:(2 * sl + 2) * tq]], axis=0)
                o_ref[t * tq:(t + 1) * tq, LANES * sl:LANES * (sl + 1)] = pair.T.astype(BF16)


def _attn_a(q, k, v):
    bsz, s, _ = k.shape
    tq = TQ_A
    tk = min(TK_A, s)
    ntile = max(1, ITEMS_A * tk // s)
    return pl.pallas_call(
        functools.partial(_attn_a_kernel, tk=tk),
        grid=(bsz, 2, s // (ntile * tq)),
        in_specs=[
            pl.BlockSpec((None, None, ntile, LANES, 4 * tq), lambda b, g, i: (b, g, i, 0, 0)),
            pl.BlockSpec((None, s, LANES), lambda b, g, i: (b, 0, 0)),
            pl.BlockSpec((None, None, LANES, s), lambda b, g, i: (b, g, 0, 0)),
        ],
        out_specs=pl.BlockSpec((None, ntile * tq, 2 * LANES), lambda b, g, i: (b, i, g)),
        out_shape=jax.ShapeDtypeStruct((bsz, s, 512), BF16),
        scratch_shapes=[pltpu.VMEM((2, tk, 4 * tq), F32)],
        compiler_params=_params("parallel", "arbitrary", "arbitrary"),
        name="attn_a",
    )(q, k, v)


def _window(refs, start, size, cols, lq):
    rpt = refs[0].shape[1]
    pieces = []
    for n, r in enumerate(refs):
        for t in range(r.shape[0]):
            t0 = n * lq + t * rpt
            a, b = max(start, t0), min(start + size, t0 + rpt)
            if a < b:
                pieces.append(r[t, a - t0:b - t0, cols])
    return pieces[0] if len(pieces) == 1 else jnp.concatenate(pieces, axis=0)


def _store_rows(ref, r0, val, cols):
    rpt = ref.shape[1]
    n = val.shape[0]
    for t in range(ref.shape[0]):
        a, b = max(r0, t * rpt), min(r0 + n, (t + 1) * rpt)
        if a < b:
            ref[t, a - t * rpt:b - t * rpt, cols] = val[a - r0:b - r0]


def _attn_b_kernel(q_ref, kp_ref, kc_ref, kn_ref, vp_ref, vc_ref, vn_ref, o_ref, lse_ref, s_ref, *, l_len):
    lq = q_ref.shape[0] * q_ref.shape[1]
    sub = min(SUB_B, lq)
    base = pl.program_id(2) * lq
    lane = _lane_iota(sub)
    lo = lane < HEAD_DIM
    rq = lax.broadcasted_iota(jnp.int32, (sub, 2 * sub), 0)
    ck = lax.broadcasted_iota(jnp.int32, (sub, 2 * sub), 1)
    delta = ck - rq - sub // 2
    band = jnp.abs(delta) <= B_RADIUS
    items = [(sb, j) for sb in range(lq // sub) for j in range(4)]

    def scores(n):
        sb, j = items[n]
        r0 = sb * sub
        cols = slice(LANES * j, LANES * (j + 1))
        qs = _window((q_ref,), r0, sub, cols, lq).astype(F32)
        kw = _window((kp_ref, kc_ref, kn_ref), lq + r0 - sub // 2, 2 * sub, cols, lq)
        q2 = jnp.concatenate([jnp.where(lo, qs, 0.0), jnp.where(lo, 0.0, qs)], axis=0).astype(BF16)
        s_ref[n % 2] = lax.dot_general(q2, kw, (((1,), (1,)), ((), ())), preferred_element_type=F32)

    scores(0)
    for n, (sb, j) in enumerate(items):
        if n + 1 < len(items):
            scores(n + 1)
        r0 = sb * sub
        cols = slice(LANES * j, LANES * (j + 1))
        if j == 0:
            kpos = jnp.where(band, base + (r0 - sub // 2) + ck, -1)
            mask = jnp.where(jnp.logical_and(kpos >= 0, kpos < l_len), 0.0, NEG)
            mask2 = jnp.concatenate([mask, mask], axis=0)
            lse_tile = jnp.zeros((sub, LANES), F32)
        vw = _window((vp_ref, vc_ref, vn_ref), lq + r0 - sub // 2, 2 * sub, cols, lq)
        s = s_ref[n % 2] + mask2
        m = jnp.max(s, axis=1, keepdims=True)
        p = jnp.exp2(s - m)
        l = jnp.sum(p, axis=1, keepdims=True)
        o = jnp.dot(p.astype(BF16), vw, preferred_element_type=F32) * (1.0 / l)
        lse = m + jnp.log2(l)
        lse_tile = (lse_tile + jnp.where(lane == 2 * j, lse[0:sub], 0.0)
                    + jnp.where(lane == 2 * j + 1, lse[sub:2 * sub], 0.0))
        _store_rows(o_ref, r0, jnp.where(lo, o[0:sub], o[sub:2 * sub]).astype(BF16), cols)
        if j == 3:
            packed = jnp.zeros((sub, LANES), F32)
            rest = lse_tile
            for part in range(LSE_PARTS):
                term = rest.astype(BF16).astype(F32)
                rest = rest - term
                packed = packed + (pltpu.roll(term, 8 * part, 1) if part else term)
            _store_rows(lse_ref, r0, packed.astype(BF16), slice(None))


def _attn_b(q, k, v, dil):
    bsz, ntile, _, rpt, w = q.shape
    l_len = ntile * rpt
    lq = min(LQ_B, l_len)
    nt = lq // rpt
    nblk = l_len // lq

    def spec(width, shift):
        def index(b, r, i):
            return (b, jnp.clip(i + shift, 0, nblk - 1), r, 0, 0)
        return pl.BlockSpec((None, nt, None, rpt, width), index)

    cur, prv, nxt = spec(w, 0), spec(w, -1), spec(w, 1)
    return pl.pallas_call(
        functools.partial(_attn_b_kernel, l_len=l_len),
        grid=(bsz, dil, nblk),
        in_specs=[cur, prv, cur, nxt, prv, cur, nxt],
        out_specs=[cur, spec(LANES, 0)],
        out_shape=[jax.ShapeDtypeStruct(q.shape, BF16),
                   jax.ShapeDtypeStruct((bsz, ntile, dil, rpt, LANES), BF16)],
        scratch_shapes=[pltpu.VMEM((2, 2 * min(SUB_B, lq), 2 * min(SUB_B, lq)), F32)],
        compiler_params=_params("parallel", "parallel", "parallel"),
        name=f"attn_b_d{dil}",
    )(q, k, k, k, v, v, v)


def _attn_c_kernel(q_ref, kp_ref, kc_ref, kn_ref, vp_ref, vc_ref, vn_ref, bias_ref, o_ref, s_ref):
    tq = q_ref.shape[0]
    lane = _lane_iota(tq)
    lo = lane < HEAD_DIM
    npair = C_HEADS // 2

    def scores(pr):
        cols = slice(LANES * pr, LANES * (pr + 1))
        qs = q_ref[:, cols].astype(F32)
        k3 = jnp.concatenate([kp_ref[:, cols], kc_ref[:, cols], kn_ref[:, cols]], axis=0)
        q2 = jnp.concatenate([jnp.where(lo, qs, 0.0), jnp.where(lo, 0.0, qs)], axis=0).astype(BF16)
        s_ref[pr % 2] = lax.dot_general(q2, k3, (((1,), (1,)), ((), ())), preferred_element_type=F32)

    scores(0)
    for pr in range(npair):
        if pr + 1 < npair:
            scores(pr + 1)
        cols = slice(LANES * pr, LANES * (pr + 1))
        v3 = jnp.concatenate([vp_ref[:, cols], vc_ref[:, cols], vn_ref[:, cols]], axis=0)
        s = s_ref[pr % 2] + bias_ref[2 * pr:2 * pr + 2].reshape(2 * tq, 3 * tq)
        m = jnp.max(s, axis=1, keepdims=True)
        p = jnp.exp2(s - m)
        l = jnp.sum(p, axis=1, keepdims=True)
        o = jnp.dot(p.astype(BF16), v3, preferred_element_type=F32) * (1.0 / l)
        o_ref[:, cols] = jnp.where(lo, o[0:tq], o[tq:2 * tq]).astype(BF16)


def _bias_c(rpb):
    nq, nu = ROWS_C, 3 * ROWS_C
    assert nq == C_WIN_H // 2
    nrel = 2 * C_WIN_H - 1
    scaled = rpb.astype(F32) * LOG2E
    per_col = []
    for c in range(GRID_W):
        c0 = min(max(c - C_WIN_W // 2, 0), GRID_W - C_WIN_W)
        win = scaled[:, :, c0 - c + C_WIN_W - 1:c0 - c + 2 * C_WIN_W - 1]
        per_col.append(jnp.pad(win, ((0, 0), (0, 0), (c0, GRID_W - C_WIN_W - c0)), constant_values=NEG))
    tab = jnp.stack(per_col, axis=2)
    dead = jnp.full((C_HEADS, GRID_W, GRID_W), NEG, F32)
    kinds = []
    for kind in range(3):
        rows = []
        for j in range(nq):
            slots = []
            for u in range(nu):
                rel = (u - nq) - j + C_WIN_H - 1
                if kind == 0:
                    ok = nq <= u < nq + C_WIN_H
                elif kind == 2:
                    ok = 2 * nq - C_WIN_H <= u < 2 * nq
                else:
                    ok = nq - C_WIN_H // 2 <= u - j < nq + C_WIN_H // 2
                slots.append(tab[:, rel] if ok and 0 <= rel < nrel else dead)
            rows.append(jnp.concatenate(slots, axis=2))
        kinds.append(jnp.concatenate(rows, axis=1))
    return jnp.stack(kinds, axis=0)


def _attn_c(q, k, v, bias):
    bsz, s, w = q.shape
    tq = ROWS_C * GRID_W
    nblk = s // tq
    assert nblk >= 2
    kind = lambda i: jnp.where(i == 0, 0, jnp.where(i == nblk - 1, 2, 1))
    cur = pl.BlockSpec((None, tq, w), lambda b, i: (b, i, 0))
    prv = pl.BlockSpec((None, tq, w), lambda b, i: (b, jnp.maximum(i - 1, 0), 0))
    nxt = pl.BlockSpec((None, tq, w), lambda b, i: (b, jnp.minimum(i + 1, nblk - 1), 0))
    return pl.pallas_call(
        _attn_c_kernel,
        grid=(bsz, nblk),
        in_specs=[cur, prv, cur, nxt, prv, cur, nxt,
                  pl.BlockSpec((None, C_HEADS, tq, 3 * tq), lambda b, i: (kind(i), 0, 0, 0))],
        out_specs=cur,
        out_shape=jax.ShapeDtypeStruct((bsz, s, w), BF16),
        scratch_shapes=[pltpu.VMEM((2, 2 * tq, 3 * tq), F32)],
        compiler_params=_params("parallel", "arbitrary"),
        name="attn_c",
    )(q, k, k, k, v, v, v, bias)


S2_D = 64


def _dft_consts(s):
    s1 = s // S2_D
    th2 = 2 * np.pi * np.outer(np.arange(S2_D), np.arange(S2_D)) / S2_D
    c2, n2 = np.cos(th2) / 8.0, np.sin(th2) / 8.0
    w2big = np.block([[c2, n2], [-n2, c2]])
    th1 = 2 * np.pi * np.outer(np.arange(s1), np.arange(s1)) / s1
    w1cat = np.concatenate([np.cos(th1), np.sin(th1)], axis=1) / math.sqrt(s1)
    tht = 2 * np.pi * np.outer(np.arange(S2_D), np.arange(s1)) / s
    return (jnp.asarray(w2big, BF16), jnp.asarray(w1cat, BF16),
            jnp.asarray(np.cos(tht), F32), jnp.asarray(-np.sin(tht), F32))


def _chan_dft_const():
    th = 2 * np.pi * np.outer(np.arange(D_GROUP_DIM), np.arange(D_GROUP_DIM)) / D_GROUP_DIM
    eye = np.eye(D_GROUPS)
    wc = np.concatenate([np.kron(eye, np.cos(th)), np.kron(eye, -np.sin(th))], axis=1) / 8.0
    return jnp.asarray(wc, BF16)


def _dft1_kernel(v_ref, w_ref, tr_ref, ti_ref, o_ref):
    x = jnp.concatenate([v_ref[0], v_ref[1]], axis=0)
    b = jnp.dot(w_ref[...], x, preferred_element_type=F32)
    br, bi = b[0:S2_D], b[S2_D:2 * S2_D]
    tr, ti = tr_ref[...], ti_ref[...]
    o_ref[0] = (br * tr - bi * ti).astype(BF16)
    o_ref[1] = (br * ti + bi * tr).astype(BF16)


def _dft2_kernel(b_ref, w_ref, o_ref):
    for kk in range(b_ref.shape[1]):
        rhs = jnp.concatenate([b_ref[0, kk], b_ref[1, kk]], axis=0)
        x = jnp.dot(w_ref[...], rhs, preferred_element_type=F32)
        o_ref[:, D_WIDTH * kk:D_WIDTH * (kk + 1)] = x.astype(BF16)


def _fourier(vd, consts):
    bsz, _, s, w = vd.shape
    s1 = s // S2_D
    w2big, w1cat, tr, ti = consts
    ncol = s1 * w
    tn = min(TN_D1, ncol)
    twr = jnp.repeat(tr, w, axis=1)
    twi = jnp.repeat(ti, w, axis=1)
    b1 = pl.pallas_call(
        _dft1_kernel,
        grid=(ncol // tn, bsz),
        in_specs=[
            pl.BlockSpec((None, 2, S2_D, tn), lambda j, b: (b, 0, 0, j)),
            pl.BlockSpec((2 * S2_D, 2 * S2_D), lambda j, b: (0, 0)),
            pl.BlockSpec((S2_D, tn), lambda j, b: (0, j)),
            pl.BlockSpec((S2_D, tn), lambda j, b: (0, j)),
        ],
        out_specs=pl.BlockSpec((None, 2, S2_D, tn), lambda j, b: (b, 0, 0, j)),
        out_shape=jax.ShapeDtypeStruct((bsz, 2, S2_D, ncol), BF16),
        compiler_params=_params("parallel", "parallel"),
        name="dft_stage1",
    )(vd.reshape(bsz, 2, S2_D, ncol), w2big, twr, twi)
    tk2 = TK2_D
    f = pl.pallas_call(
        _dft2_kernel,
        grid=(bsz, S2_D // tk2),
        in_specs=[
            pl.BlockSpec((None, 2, tk2, s1, w), lambda b, j: (b, 0, j, 0, 0)),
            pl.BlockSpec((s1, 2 * s1), lambda b, j: (0, 0)),
        ],
        out_specs=pl.BlockSpec((None, s1, tk2 * w), lambda b, j: (b, 0, j)),
        out_shape=jax.ShapeDtypeStruct((bsz, s1, S2_D * w), BF16),
        compiler_params=_params("parallel", "parallel"),
        name="dft_stage2",
    )(b1.reshape(bsz, 2, S2_D, s1, w), w1cat)
    return f.reshape(bsz, s, w)


def _finish(y, x_ref, gate_ref, g_ref, o_ref):
    ms = jnp.mean(y * y, axis=-1, keepdims=True)
    yn = y * lax.rsqrt(ms + EPS) * g_ref[...]
    o_ref[...] = x_ref[...] + gate_ref[...] * yn


def _outproj_even_kernel(oa_ref, o1_ref, o4_ref, o16_ref, l1_ref, l4_ref, l16_ref, gates_ref,
                         ex_ref, p4_ref, p16_ref, w_ref, x_ref, gate_ref, g_ref, o_ref):
    tm = x_ref.shape[0]
    ex = ex_ref[...]

    def natural(o_res, l_res, perm_ref):
        tiles = []
        for t in range(tm // T_RES):
            both = jnp.concatenate([o_res[t].reshape(T_RES, 512), l_res[t].reshape(T_RES, LANES)], axis=1)
            tiles.append(jnp.dot(perm_ref[...], both, preferred_element_type=F32))
        nat = jnp.concatenate(tiles, axis=0)
        return nat[:, 0:512], nat[:, 512:512 + LANES]

    def lse(x):
        out = x
        for part in range(1, LSE_PARTS):
            out = out + pltpu.roll(x, LANES - 8 * part, 1)
        return out

    o2, l2 = natural(o4_ref, l4_ref, p4_ref)
    o3, l3 = natural(o16_ref, l16_ref, p16_ref)
    l1, l2, l3 = lse(l1_ref[...].astype(F32)), lse(l2), lse(l3)
    mx = jnp.maximum(jnp.maximum(l1, l2), l3)
    e1, e2, e3 = jnp.exp2(l1 - mx), jnp.exp2(l2 - mx), jnp.exp2(l3 - mx)
    inv = 1.0 / (e1 + e2 + e3)
    terms = []
    for wgt in (e1 * inv, e2 * inv, e3 * inv):
        hi = wgt.astype(BF16)
        terms += [hi, (wgt - hi.astype(F32)).astype(BF16)]
    wide = jnp.dot(jnp.concatenate(terms, axis=0), ex, preferred_element_type=F32)
    w1, w2, w3 = (wide[2 * n * tm:(2 * n + 1) * tm] + wide[(2 * n + 1) * tm:(2 * n + 2) * tm] for n in range(3))
    ob = w1 * o1_ref[...].astype(F32) + w2 * o2 + w3 * o3
    ma = (oa_ref[...].astype(F32) * gates_ref[:, 0:512].astype(F32)).astype(BF16)
    mb = (ob * gates_ref[:, 512:1024].astype(F32)).astype(BF16)
    y = jnp.dot(jnp.concatenate([ma, mb], axis=1), w_ref[...], preferred_element_type=F32)
    _finish(y, x_ref, gate_ref, g_ref, o_ref)


def _outproj_even(oa, obs, lses, gates, ex, w_out, x, gate, post_g):
    bsz, s, d = x.shape
    tm = TM_PROJ
    return pl.pallas_call(
        _outproj_even_kernel,
        grid=(bsz, s // tm),
        in_specs=[_tok_spec(tm, 512), _tok_spec(tm, 512), _res_spec(tm, 4, 512), _res_spec(tm, 16, 512),
                  _tok_spec(tm, LANES), _res_spec(tm, 4, LANES), _res_spec(tm, 16, LANES),
                  _tok_spec(tm, 1024), _const_spec((LANES, 512)),
                  _const_spec((T_RES, T_RES)), _const_spec((T_RES, T_RES)), _const_spec((d, d)),
                  _tok_spec(tm, d), _bcast_spec(d), _const_spec((1, d))],
        out_specs=_tok_spec(tm, d),
        out_shape=jax.ShapeDtypeStruct((bsz, s, d), F32),
        compiler_params=_params("parallel", "parallel"),
        name="outproj_even",
    )(oa, *obs, *lses, gates, ex,
      jnp.asarray(_residue_perm(T_RES, 4).T, BF16), jnp.asarray(_residue_perm(T_RES, 16).T, BF16),
      w_out, x, gate, post_g)


def _outproj_odd_kernel(oc_ref, f_ref, gates_ref, lin_ref, w_ref, x_ref, gate_ref, g_ref, o_ref):
    od = jnp.dot(f_ref[...], lin_ref[...], preferred_element_type=F32)
    mc = (oc_ref[...].astype(F32) * gates_ref[:, 0:768].astype(F32)).astype(BF16)
    md = (od * gates_ref[:, 768:1024].astype(F32)).astype(BF16)
    y = jnp.dot(jnp.concatenate([mc, md], axis=1), w_ref[...], preferred_element_type=F32)
    _finish(y, x_ref, gate_ref, g_ref, o_ref)


def _outproj_odd(oc, f, gates, lin, w_out, x, gate, post_g):
    bsz, s, d = x.shape
    tm = TM_PROJ
    return pl.pallas_call(
        _outproj_odd_kernel,
        grid=(bsz, s // tm),
        in_specs=[_tok_spec(tm, 768), _tok_spec(tm, D_WIDTH), _tok_spec(tm, 1024),
                  _const_spec((D_WIDTH, D_WIDTH)), _const_spec((d, d)),
                  _tok_spec(tm, d), _bcast_spec(d), _const_spec((1, d))],
        out_specs=_tok_spec(tm, d),
        out_shape=jax.ShapeDtypeStruct((bsz, s, d), F32),
        compiler_params=_params("parallel", "parallel"),
        name="outproj_odd",
    )(oc, f, gates, lin, w_out, x, gate, post_g)


def _rope_tables(s):
    t = jnp.arange(s)

    def tab(pos, dim, theta):
        inv = theta ** (-jnp.arange(0, dim, 2, dtype=F32) / dim)
        ang = pos[:, None] * inv[None, :]
        return jnp.cos(ang), jnp.sin(ang)

    cr, sr = tab((t // GRID_W).astype(F32), HEAD_DIM // 2, A_ROPE_THETA)
    cc, sc = tab((t % GRID_W).astype(F32), HEAD_DIM // 2, A_ROPE_THETA)
    cb, sb = tab(t.astype(F32), B_ROPE_DIMS, B_ROPE_THETA)
    rest = HEAD_DIM - B_ROPE_DIMS
    ca = jnp.concatenate([cr, cr, cc, cc] * 2, axis=-1)
    sa = jnp.concatenate([-sr, sr, -sc, sc] * 2, axis=-1)
    cb = jnp.concatenate([cb, cb, jnp.ones((s, rest), F32)] * 2, axis=-1)
    sb = jnp.concatenate([-sb, sb, jnp.zeros((s, rest), F32)] * 2, axis=-1)
    return ca, sa, cb, sb


def _even_layer(x, mod, pre_g, post_g, w_in, w_out, qn, kn, tabs, ex):
    bsz, s, _ = x.shape
    shift, scale, gate = mod
    qa, ka, va, gates, qkv_b = _inproj_even(x, scale, shift, pre_g, w_in, qn, kn, tabs)
    oa = _attn_a(qa, ka, va)
    obs, lses = [], []
    for n, dil in enumerate(B_DILATIONS):
        q, k, v = (t[n] for t in qkv_b)
        if dil == 1:
            lq = min(LQ_B, s)
            q, k, v = (t.reshape(bsz, s // lq, 1, lq, t.shape[-1]) for t in (q, k, v))
        o, lse = _attn_b(q, k, v, dil)
        if dil == 1:
            o, lse = o.reshape(bsz, s, o.shape[-1]), lse.reshape(bsz, s, LANES)
        obs.append(o)
        lses.append(lse)
    return _outproj_even(oa, obs, lses, gates, ex, w_out, x, gate, post_g)


def _odd_layer(x, mod, pre_g, post_g, w_in, w_out, bias, lin, wc, dft):
    shift, scale, gate = mod
    qc, kc, vc, gates, vd = _inproj_odd(x, scale, shift, pre_g, w_in, wc)
    oc = _attn_c(qc, kc, vc, bias)
    f = _fourier(vd, dft)
    return _outproj_odd(oc, f, gates, lin, w_out, x, gate, post_g)


def _trunk(x, mods, pre_g, post_g, w_in_ab, w_out_ab, qn_a, kn_a, w_in_cd, w_out_cd, biases, lin_d):
    bsz, s, d = x.shape
    assert s % TM_PROJ == 0 and s % (16 * SUB_B) == 0
    tabs = _rope_tables(s)
    dft = _dft_consts(s)
    wc = _chan_dft_const()
    lane = np.arange(LANES)
    ex = jnp.asarray(lane[:, None] == np.arange(512)[None, :] // HEAD_DIM, BF16)
    for i in range(DEPTH):
        j = i // 2
        mod = tuple(mods[i][:, None, k * d:(k + 1) * d] for k in range(3))
        pg, qg = pre_g[i][None, :], post_g[i][None, :]
        if i % 2 == 0:
            qn = jnp.tile(qn_a[j], 2)[None, :]
            kn = jnp.tile(kn_a[j], 2)[None, :]
            x = _even_layer(x, mod, pg, qg, w_in_ab[j], w_out_ab[j], qn, kn, tabs, ex)
        else:
            x = _odd_layer(x, mod, pg, qg, w_in_cd[j], w_out_cd[j], biases[j], lin_d[j], wc, dft)
    return x


def kernel(x_prompt, x_sample, c_prompt, c_sample, pre_g, post_g, ada_w, ada_b,
           w_in_ab, w_out_ab, qn_a, kn_a, w_in_cd, w_out_cd, rpb_c, lin_d):
    nb = x_prompt.shape[0]
    mods = _adaln(jnp.concatenate([c_prompt, c_sample], axis=0), ada_w, ada_b)
    biases = [_bias_c(rpb_c[j]) for j in range(rpb_c.shape[0])]
    args = (pre_g, post_g, w_in_ab.astype(BF16), w_out_ab.astype(BF16), qn_a, kn_a,
            w_in_cd.astype(BF16), w_out_cd.astype(BF16), biases, lin_d.astype(BF16))
    y_prompt = _trunk(x_prompt, mods[:, :nb], *args)
    y_sample = _trunk(x_sample, mods[:, nb:], *args)
    return (y_prompt, y_sample)
```

```python
import functools
import math

import numpy as np
import jax
import jax.numpy as jnp
from jax import lax
from jax.experimental import pallas as pl
from jax.experimental.pallas import tpu as pltpu

F32 = jnp.float32
BF16 = jnp.bfloat16

D_MODEL = 1024
DEPTH = 4
HEAD_DIM = 64
GRID_W = 64
A_ROPE_THETA = 10000.0
B_ROPE_THETA = 500000.0
B_ROPE_DIMS = 16
B_DILATIONS = (1, 4, 16)
B_RADIUS = 64
C_HEADS = 12
C_WIN_H = 8
C_WIN_W = 16
D_GROUPS = 4
D_GROUP_DIM = 64
D_WIDTH = D_GROUPS * D_GROUP_DIM
AB_IN = 3328
CD_IN = 3584
EPS = 1e-6
NEG = -1e30
LOG2E = 1.4426950408889634
QK_SCALE = LOG2E * HEAD_DIM ** -0.5

LANES = 128
VMEM_LIMIT = 56 * 1024 * 1024

TM_PROJ = 512
T_RES = 256
TQ_A = 256
TK_A = 2048
ITEMS_A = 16
LQ_B = 512
SUB_B = 128
ROWS_C = 4
TN_D1 = 4096
TK2_D = 8
LSE_PARTS = 3


def _params(*sem):
    return pltpu.CompilerParams(dimension_semantics=sem, vmem_limit_bytes=VMEM_LIMIT)


def _silu(x):
    return x / (1.0 + jnp.exp(-x))


def _lane_iota(rows):
    return lax.broadcasted_iota(jnp.int32, (rows, LANES), 1)


def _adaln_kernel(c_ref, w_ref, b_ref, o_ref):
    a = _silu(c_ref[...]).astype(BF16)
    o_ref[...] = jnp.dot(a, w_ref[...].astype(BF16), preferred_element_type=F32) + b_ref[...]


def _adaln(c_all, ada_w, ada_b):
    bt = c_all.shape[0]
    d = D_MODEL
    return pl.pallas_call(
        _adaln_kernel,
        grid=(DEPTH, 3),
        in_specs=[
            pl.BlockSpec((bt, d), lambda l, j: (0, 0)),
            pl.BlockSpec((None, d, d), lambda l, j: (l, 0, j)),
            pl.BlockSpec((None, 1, d), lambda l, j: (l, 0, j)),
        ],
        out_specs=pl.BlockSpec((None, bt, d), lambda l, j: (l, 0, j)),
        out_shape=jax.ShapeDtypeStruct((DEPTH, bt, 3 * d), F32),
        compiler_params=_params("parallel", "parallel"),
        name="adaln",
    )(c_all, ada_w, ada_b.reshape(DEPTH, 1, 3 * d))


def _modulated_norm(x, g, scale, shift):
    ms = jnp.mean(x * x, axis=-1, keepdims=True)
    y = x * lax.rsqrt(ms + EPS) * g
    return (y * (1.0 + scale) + shift).astype(BF16)


def _head_norm(xs, gain, lo):
    x2 = xs * xs
    s_lo = jnp.sum(jnp.where(lo, x2, 0.0), axis=1, keepdims=True)
    s_hi = jnp.sum(jnp.where(lo, 0.0, x2), axis=1, keepdims=True)
    ss = jnp.where(lo, s_lo, s_hi)
    return xs * lax.rsqrt(ss * (1.0 / HEAD_DIM) + EPS) * gain


def _rope(xs, cos, sin_signed, first_half, shift):
    up = pltpu.roll(xs, LANES - shift, 1)
    dn = pltpu.roll(xs, shift, 1)
    return xs * cos + jnp.where(first_half, up, dn) * sin_signed


def _inproj_even_kernel(x_ref, sc_ref, sh_ref, g_ref, w_ref, qn_ref, kn_ref,
                        ca_ref, sa_ref, cb_ref, sb_ref, p4_ref, p16_ref,
                        qa_ref, ka_ref, va_ref, gate_ref, *b_refs):
    tm = x_ref.shape[0]
    h = _modulated_norm(x_ref[...], g_ref[...], sc_ref[...], sh_ref[...])
    lane = _lane_iota(tm)
    half_a = (lane & 16) == 0
    half_b = (lane & 8) == 0
    lo = lane < HEAD_DIM
    ca, sa, cb, sb = ca_ref[...], sa_ref[...], cb_ref[...], sb_ref[...]
    full = jnp.dot(h, w_ref[...], preferred_element_type=F32)

    def proj(a, b):
        return full[:, a:b]

    def emit_b(vals):
        for n, val in enumerate(vals):
            b_refs[3 * n][...] = val
        cat = jnp.concatenate(vals, axis=1)
        for d, perm_ref in enumerate((p4_ref, p16_ref)):
            for t in range(tm // T_RES):
                pv = jnp.dot(perm_ref[...], cat[t * T_RES:(t + 1) * T_RES], preferred_element_type=F32)
                for n in range(3):
                    o_ref = b_refs[3 * n + 1 + d]
                    o_ref[t] = pv[:, 512 * n:512 * (n + 1)].astype(BF16).reshape(o_ref.shape[1:])

    p = proj(0, 512)
    lane_half = lane // HEAD_DIM
    for j in range(4):
        xs = _head_norm(p[:, LANES * j:LANES * (j + 1)], qn_ref[...], lo)
        xs = _rope(xs, ca, sa, half_a, 16) * QK_SCALE
        swapped = pltpu.roll(xs, HEAD_DIM, 1)
        for half in range(2):
            hd = 2 * j + half
            g, hh = hd // 4, hd % 4
            both = jnp.where(lane_half == half, xs, swapped)
            ht = jnp.where(lane_half == g, both, 0.0).T.astype(BF16)
            for t in range(tm // TQ_A):
                qa_ref[g, t, :, hh * TQ_A:(hh + 1) * TQ_A] = ht[:, t * TQ_A:(t + 1) * TQ_A]
    xs = _head_norm(proj(512, 640), kn_ref[...], lo)
    ka_ref[...] = _rope(xs, ca, sa, half_a, 16).astype(BF16)
    v = proj(640, 768)
    va_ref[0] = jnp.where(lo, v, 1.0).T.astype(BF16)
    va_ref[1] = jnp.where(lo, 1.0, v).T.astype(BF16)
    gate_ref[:, 0:512] = _silu(proj(768, 1280)).astype(BF16)
    p = proj(1280, 1792)
    qb = [(_rope(p[:, LANES * j:LANES * (j + 1)], cb, sb, half_b, 8) * QK_SCALE).astype(BF16)
          for j in range(4)]
    p = proj(1792, 2304)
    kb = [_rope(p[:, LANES * j:LANES * (j + 1)], cb, sb, half_b, 8).astype(BF16) for j in range(4)]
    emit_b([jnp.concatenate(qb, axis=1), jnp.concatenate(kb, axis=1), proj(2304, 2816).astype(BF16)])
    gate_ref[:, 512:1024] = _silu(proj(2816, 3328)).astype(BF16)


def _tok_spec(tm, width):
    return pl.BlockSpec((None, tm, width), lambda b, i: (b, i, 0))


def _res_spec(tm, dil, width):
    return pl.BlockSpec((None, tm // T_RES, dil, T_RES // dil, width), lambda b, i: (b, i, 0, 0, 0))


def _bcast_spec(width):
    return pl.BlockSpec((None, 1, width), lambda b, i: (b, 0, 0))


def _const_spec(shape):
    return pl.BlockSpec(shape, lambda b, i: (0,) * len(shape))


def _residue_perm(tm, dil):
    r = np.arange(tm)
    src = (r % (tm // dil)) * dil + r // (tm // dil)
    return np.asarray(r[None, :] == src[:, None], np.float32)


def _inproj_even(x, scale, shift, pre_g, w_in, qn, kn, tabs):
    bsz, s, d = x.shape
    tm = TM_PROJ
    ca, sa, cb, sb = tabs
    tab_spec = pl.BlockSpec((tm, LANES), lambda b, i: (i, 0))
    b_specs, b_shapes = [], []
    for _ in range(3):
        b_specs.append(_tok_spec(tm, 512))
        b_shapes.append(jax.ShapeDtypeStruct((bsz, s, 512), BF16))
        for dil in B_DILATIONS[1:]:
            b_specs.append(_res_spec(tm, dil, 512))
            b_shapes.append(jax.ShapeDtypeStruct((bsz, s // T_RES, dil, T_RES // dil, 512), BF16))
    outs = pl.pallas_call(
        _inproj_even_kernel,
        grid=(bsz, s // tm),
        in_specs=[
            _tok_spec(tm, d), _bcast_spec(d), _bcast_spec(d), _const_spec((1, d)),
            _const_spec((d, AB_IN)), _const_spec((1, LANES)), _const_spec((1, LANES)),
            tab_spec, tab_spec, tab_spec, tab_spec,
            _const_spec((T_RES, T_RES)), _const_spec((T_RES, T_RES)),
        ],
        out_specs=[pl.BlockSpec((None, 2, tm // TQ_A, LANES, 4 * TQ_A), lambda b, i: (b, 0, i, 0, 0)),
                   _tok_spec(tm, LANES),
                   pl.BlockSpec((None, 2, LANES, tm), lambda b, i: (b, 0, 0, i)),
                   _tok_spec(tm, 1024)] + b_specs,
        out_shape=[jax.ShapeDtypeStruct((bsz, 2, s // TQ_A, LANES, 4 * TQ_A), BF16),
                   jax.ShapeDtypeStruct((bsz, s, LANES), BF16),
                   jax.ShapeDtypeStruct((bsz, 2, LANES, s), BF16),
                   jax.ShapeDtypeStruct((bsz, s, 1024), BF16)] + b_shapes,
        compiler_params=_params("parallel", "parallel"),
        name="inproj_even",
    )(x, scale, shift, pre_g, w_in, qn, kn, ca, sa, cb, sb,
      jnp.asarray(_residue_perm(T_RES, 4), BF16), jnp.asarray(_residue_perm(T_RES, 16), BF16))
    qa, ka, va, gates = outs[:4]
    qkv_b = [outs[4 + 3 * n:7 + 3 * n] for n in range(3)]
    return qa, ka, va, gates, qkv_b


def _inproj_odd_kernel(x_ref, sc_ref, sh_ref, g_ref, w_ref, wc_ref,
                       q_ref, k_ref, v_ref, gate_ref, vd_ref):
    h = _modulated_norm(x_ref[...], g_ref[...], sc_ref[...], sh_ref[...])
    full = jnp.dot(h, w_ref[...], preferred_element_type=F32)

    def proj(a, b):
        return full[:, a:b]

    q_ref[...] = (proj(0, 768) * QK_SCALE).astype(BF16)
    k_ref[...] = proj(768, 1536).astype(BF16)
    v_ref[...] = proj(1536, 2304).astype(BF16)
    gate_ref[:, 0:768] = _silu(proj(2304, 3072)).astype(BF16)
    u = proj(3072, 3328).astype(BF16)
    vc = jnp.dot(u, wc_ref[...], preferred_element_type=F32)
    vd_ref[0] = vc[:, 0:D_WIDTH].astype(BF16)
    vd_ref[1] = vc[:, D_WIDTH:2 * D_WIDTH].astype(BF16)
    gate_ref[:, 768:1024] = _silu(proj(3328, 3584)).astype(BF16)


def _inproj_odd(x, scale, shift, pre_g, w_in, wc):
    bsz, s, d = x.shape
    tm = TM_PROJ
    widths = (768, 768, 768, 1024)
    return pl.pallas_call(
        _inproj_odd_kernel,
        grid=(bsz, s // tm),
        in_specs=[
            _tok_spec(tm, d), _bcast_spec(d), _bcast_spec(d), _const_spec((1, d)),
            _const_spec((d, CD_IN)), _const_spec((D_WIDTH, 2 * D_WIDTH)),
        ],
        out_specs=[_tok_spec(tm, w) for w in widths]
        + [pl.BlockSpec((None, 2, tm, D_WIDTH), lambda b, i: (b, 0, i, 0))],
        out_shape=[jax.ShapeDtypeStruct((bsz, s, w), BF16) for w in widths]
        + [jax.ShapeDtypeStruct((bsz, 2, s, D_WIDTH), BF16)],
        compiler_params=_params("parallel", "parallel"),
        name="inproj_odd",
    )(x, scale, shift, pre_g, w_in, wc)


def _attn_a_kernel(q_ref, k_ref, v_ref, o_ref, s_ref, *, tk):
    ntile = q_ref.shape[0]
    tq = o_ref.shape[0] // ntile
    nchunk = k_ref.shape[0] // tk
    first = pl.program_id(1) == 0
    items = [(t, j) for t in range(ntile) for j in range(nchunk)]

    def scores(n):
        t, j = items[n]
        s_ref[n % 2] = jnp.dot(k_ref[j * tk:(j + 1) * tk, :], q_ref[t], preferred_element_type=F32)

    scores(0)
    for n, (t, j) in enumerate(items):
        if n + 1 < len(items):
            scores(n + 1)
        if j == 0:
            m = jnp.full((1, 4 * tq), NEG, F32)
            acc = jnp.zeros((LANES, 4 * tq), F32)
        s = s_ref[n % 2]
        vc = v_ref[:, j * tk:(j + 1) * tk]
        m_new = jnp.maximum(m, jnp.max(s, axis=0, keepdims=True))
        p = jnp.exp2(s - m_new).astype(BF16)
        acc = jnp.exp2(m - m_new) * acc + jnp.dot(vc, p, preferred_element_type=F32)
        m = m_new
        if j == nchunk - 1:
            num = jnp.where(first, acc[0:HEAD_DIM], acc[HEAD_DIM:LANES])
            den = jnp.where(first, acc[HEAD_DIM:HEAD_DIM + 1], acc[0:1])
            o = num / den
            for sl in range(2):
                pair = jnp.concatenate([o[:, (2 * sl) * tq:(2 * sl + 1) * tq],
                                        o[:, (2 * sl + 1) * tq:(2 * sl + 2) * tq]], axis=0)
                o_ref[t * tq:(t + 1) * tq, LANES * sl:LANES * (sl + 1)] = pair.T.astype(BF16)


def _attn_a(q, k, v):
    bsz, s, _ = k.shape
    tq = TQ_A
    tk = min(TK_A, s)
    ntile = max(1, min(ITEMS_A * tk // s, s // tq))
    assert s % (ntile * tq) == 0 and s % tk == 0
    return pl.pallas_call(
        functools.partial(_attn_a_kernel, tk=tk),
        grid=(bsz, 2, s // (ntile * tq)),
        in_specs=[
            pl.BlockSpec((None, None, ntile, LANES, 4 * tq), lambda b, g, i: (b, g, i, 0, 0)),
            pl.BlockSpec((None, s, LANES), lambda b, g, i: (b, 0, 0)),
            pl.BlockSpec((None, None, LANES, s), lambda b, g, i: (b, g, 0, 0)),
        ],
        out_specs=pl.BlockSpec((None, ntile * tq, 2 * LANES), lambda b, g, i: (b, i, g)),
        out_shape=jax.ShapeDtypeStruct((bsz, s, 512), BF16),
        scratch_shapes=[pltpu.VMEM((2, tk, 4 * tq), F32)],
        compiler_params=_params("parallel", "arbitrary", "arbitrary"),
        name="attn_a",
    )(q, k, v)


def _window(refs, start, size, cols, lq):
    rpt = refs[0].shape[1]
    pieces = []
    for n, r in enumerate(refs):
        for t in range(r.shape[0]):
            t0 = n * lq + t * rpt
            a, b = max(start, t0), min(start + size, t0 + rpt)
            if a < b:
                pieces.append(r[t, a - t0:b - t0, cols])
    return pieces[0] if len(pieces) == 1 else jnp.concatenate(pieces, axis=0)


def _store_rows(ref, r0, val, cols):
    rpt = ref.shape[1]
    n = val.shape[0]
    for t in range(ref.shape[0]):
        a, b = max(r0, t * rpt), min(r0 + n, (t + 1) * rpt)
        if a < b:
            ref[t, a - t * rpt:b - t * rpt, cols] = val[a - r0:b - r0]


def _attn_b_kernel(q_ref, kp_ref, kc_ref, kn_ref, vp_ref, vc_ref, vn_ref, o_ref, lse_ref, s_ref, *, l_len):
    lq = q_ref.shape[0] * q_ref.shape[1]
    sub = min(SUB_B, lq)
    base = pl.program_id(2) * lq
    lane = _lane_iota(sub)
    lo = lane < HEAD_DIM
    rq = lax.broadcasted_iota(jnp.int32, (sub, 2 * sub), 0)
    ck = lax.broadcasted_iota(jnp.int32, (sub, 2 * sub), 1)
    delta = ck - rq - sub // 2
    band = jnp.abs(delta) <= B_RADIUS
    items = [(sb, j) for sb in range(lq // sub) for j in range(4)]

    def scores(n):
        sb, j = items[n]
        r0 = sb * sub
        cols = slice(LANES * j, LANES * (j + 1))
        qs = _window((q_ref,), r0, sub, cols, lq).astype(F32)
        kw = _window((kp_ref, kc_ref, kn_ref), lq + r0 - sub // 2, 2 * sub, cols, lq)
        q2 = jnp.concatenate([jnp.where(lo, qs, 0.0), jnp.where(lo, 0.0, qs)], axis=0).astype(BF16)
        s_ref[n % 2] = lax.dot_general(q2, kw, (((1,), (1,)), ((), ())), preferred_element_type=F32)

    scores(0)
    for n, (sb, j) in enumerate(items):
        if n + 1 < len(items):
            scores(n + 1)
        r0 = sb * sub
        cols = slice(LANES * j, LANES * (j + 1))
        if j == 0:
            kpos = jnp.where(band, base + (r0 - sub // 2) + ck, -1)
            mask = jnp.where(jnp.logical_and(kpos >= 0, kpos < l_len), 0.0, NEG)
            mask2 = jnp.concatenate([mask, mask], axis=0)
            lse_tile = jnp.zeros((sub, LANES), F32)
        vw = _window((vp_ref, vc_ref, vn_ref), lq + r0 - sub // 2, 2 * sub, cols, lq)
        s = s_ref[n % 2] + mask2
        m = jnp.max(s, axis=1, keepdims=True)
        p = jnp.exp2(s - m)
        l = jnp.sum(p, axis=1, keepdims=True)
        o = jnp.dot(p.astype(BF16), vw, preferred_element_type=F32) * (1.0 / l)
        lse = m + jnp.log2(l)
        lse_tile = (lse_tile + jnp.where(lane == 2 * j, lse[0:sub], 0.0)
                    + jnp.where(lane == 2 * j + 1, lse[sub:2 * sub], 0.0))
        _store_rows(o_ref, r0, jnp.where(lo, o[0:sub], o[sub:2 * sub]).astype(BF16), cols)
        if j == 3:
            packed = jnp.zeros((sub, LANES), F32)
            rest = lse_tile
            for part in range(LSE_PARTS):
                term = rest.astype(BF16).astype(F32)
                rest = rest - term
                packed = packed + (pltpu.roll(term, 8 * part, 1) if part else term)
            _store_rows(lse_ref, r0, packed.astype(BF16), slice(None))


def _attn_b(q, k, v, dil):
    bsz, ntile, _, rpt, w = q.shape
    l_len = ntile * rpt
    lq = min(LQ_B, l_len)
    nt = lq // rpt
    nblk = l_len // lq

    def spec(width, shift):
        def index(b, r, i):
            return (b, jnp.clip(i + shift, 0, nblk - 1), r, 0, 0)
        return pl.BlockSpec((None, nt, None, rpt, width), index)

    cur, prv, nxt = spec(w, 0), spec(w, -1), spec(w, 1)
    return pl.pallas_call(
        functools.partial(_attn_b_kernel, l_len=l_len),
        grid=(bsz, dil, nblk),
        in_specs=[cur, prv, cur, nxt, prv, cur, nxt],
        out_specs=[cur, spec(LANES, 0)],
        out_shape=[jax.ShapeDtypeStruct(q.shape, BF16),
                   jax.ShapeDtypeStruct((bsz, ntile, dil, rpt, LANES), BF16)],
        scratch_shapes=[pltpu.VMEM((2, 2 * min(SUB_B, lq), 2 * min(SUB_B, lq)), F32)],
        compiler_params=_params("parallel", "parallel", "parallel"),
        name=f"attn_b_d{dil}",
    )(q, k, k, k, v, v, v)


def _attn_c_kernel(q_ref, kp_ref, kc_ref, kn_ref, vp_ref, vc_ref, vn_ref, bias_ref, o_ref, s_ref):
    tq = q_ref.shape[0]
    lane = _lane_iota(tq)
    lo = lane < HEAD_DIM
    npair = C_HEADS // 2

    def scores(pr):
        cols = slice(LANES * pr, LANES * (pr + 1))
        qs = q_ref[:, cols].astype(F32)
        k3 = jnp.concatenate([kp_ref[:, cols], kc_ref[:, cols], kn_ref[:, cols]], axis=0)
        q2 = jnp.concatenate([jnp.where(lo, qs, 0.0), jnp.where(lo, 0.0, qs)], axis=0).astype(BF16)
        s_ref[pr % 2] = lax.dot_general(q2, k3, (((1,), (1,)), ((), ())), preferred_element_type=F32)

    scores(0)
    for pr in range(npair):
        if pr + 1 < npair:
            scores(pr + 1)
        cols = slice(LANES * pr, LANES * (pr + 1))
        v3 = jnp.concatenate([vp_ref[:, cols], vc_ref[:, cols], vn_ref[:, cols]], axis=0)
        s = s_ref[pr % 2] + bias_ref[2 * pr:2 * pr + 2].reshape(2 * tq, 3 * tq)
        m = jnp.max(s, axis=1, keepdims=True)
        p = jnp.exp2(s - m)
        l = jnp.sum(p, axis=1, keepdims=True)
        o = jnp.dot(p.astype(BF16), v3, preferred_element_type=F32) * (1.0 / l)
        o_ref[:, cols] = jnp.where(lo, o[0:tq], o[tq:2 * tq]).astype(BF16)


def _bias_c(rpb):
    nq, nu = ROWS_C, 3 * ROWS_C
    assert nq == C_WIN_H // 2
    nrel = 2 * C_WIN_H - 1
    scaled = rpb.astype(F32) * LOG2E
    per_col = []
    for c in range(GRID_W):
        c0 = min(max(c - C_WIN_W // 2, 0), GRID_W - C_WIN_W)
        win = scaled[:, :, c0 - c + C_WIN_W - 1:c0 - c + 2 * C_WIN_W - 1]
        per_col.append(jnp.pad(win, ((0, 0), (0, 0), (c0, GRID_W - C_WIN_W - c0)), constant_values=NEG))
    tab = jnp.stack(per_col, axis=2)
    dead = jnp.full((C_HEADS, GRID_W, GRID_W), NEG, F32)
    kinds = []
    for kind in range(3):
        rows = []
        for j in range(nq):
            slots = []
            for u in range(nu):
                rel = (u - nq) - j + C_WIN_H - 1
                if kind == 0:
                    ok = nq <= u < nq + C_WIN_H
                elif kind == 2:
                    ok = 2 * nq - C_WIN_H <= u < 2 * nq
                else:
                    ok = nq - C_WIN_H // 2 <= u - j < nq + C_WIN_H // 2
                slots.append(tab[:, rel] if ok and 0 <= rel < nrel else dead)
            rows.append(jnp.concatenate(slots, axis=2))
        kinds.append(jnp.concatenate(rows, axis=1))
    return jnp.stack(kinds, axis=0)


def _attn_c(q, k, v, bias):
    bsz, s, w = q.shape
    tq = ROWS_C * GRID_W
    nblk = s // tq
    assert nblk >= 2
    kind = lambda i: jnp.where(i == 0, 0, jnp.where(i == nblk - 1, 2, 1))
    cur = pl.BlockSpec((None, tq, w), lambda b, i: (b, i, 0))
    prv = pl.BlockSpec((None, tq, w), lambda b, i: (b, jnp.maximum(i - 1, 0), 0))
    nxt = pl.BlockSpec((None, tq, w), lambda b, i: (b, jnp.minimum(i + 1, nblk - 1), 0))
    return pl.pallas_call(
        _attn_c_kernel,
        grid=(bsz, nblk),
        in_specs=[cur, prv, cur, nxt, prv, cur, nxt,
                  pl.BlockSpec((None, C_HEADS, tq, 3 * tq), lambda b, i: (kind(i), 0, 0, 0))],
        out_specs=cur,
        out_shape=jax.ShapeDtypeStruct((bsz, s, w), BF16),
        scratch_shapes=[pltpu.VMEM((2, 2 * tq, 3 * tq), F32)],
        compiler_params=_params("parallel", "arbitrary"),
        name="attn_c",
    )(q, k, k, k, v, v, v, bias)


S2_D = 64


def _dft_consts(s):
    s1 = s // S2_D
    th2 = 2 * np.pi * np.outer(np.arange(S2_D), np.arange(S2_D)) / S2_D
    c2, n2 = np.cos(th2) / 8.0, np.sin(th2) / 8.0
    w2big = np.block([[c2, n2], [-n2, c2]])
    th1 = 2 * np.pi * np.outer(np.arange(s1), np.arange(s1)) / s1
    w1cat = np.concatenate([np.cos(th1), np.sin(th1)], axis=1) / math.sqrt(s1)
    tht = 2 * np.pi * np.outer(np.arange(S2_D), np.arange(s1)) / s
    return (jnp.asarray(w2big, BF16), jnp.asarray(w1cat, BF16),
            jnp.asarray(np.cos(tht), F32), jnp.asarray(-np.sin(tht), F32))


def _chan_dft_const():
    th = 2 * np.pi * np.outer(np.arange(D_GROUP_DIM), np.arange(D_GROUP_DIM)) / D_GROUP_DIM
    eye = np.eye(D_GROUPS)
    wc = np.concatenate([np.kron(eye, np.cos(th)), np.kron(eye, -np.sin(th))], axis=1) / 8.0
    return jnp.asarray(wc, BF16)


def _dft1_kernel(v_ref, w_ref, tr_ref, ti_ref, o_ref):
    x = jnp.concatenate([v_ref[0], v_ref[1]], axis=0)
    b = jnp.dot(w_ref[...], x, preferred_element_type=F32)
    br, bi = b[0:S2_D], b[S2_D:2 * S2_D]
    tr, ti = tr_ref[...], ti_ref[...]
    o_ref[0] = (br * tr - bi * ti).astype(BF16)
    o_ref[1] = (br * ti + bi * tr).astype(BF16)


def _dft2_kernel(b_ref, w_ref, o_ref):
    for kk in range(b_ref.shape[1]):
        rhs = jnp.concatenate([b_ref[0, kk], b_ref[1, kk]], axis=0)
        x = jnp.dot(w_ref[...], rhs, preferred_element_type=F32)
        o_ref[:, D_WIDTH * kk:D_WIDTH * (kk + 1)] = x.astype(BF16)


def _fourier(vd, consts):
    bsz, _, s, w = vd.shape
    s1 = s // S2_D
    w2big, w1cat, tr, ti = consts
    ncol = s1 * w
    tn = min(TN_D1, ncol)
    twr = jnp.repeat(tr, w, axis=1)
    twi = jnp.repeat(ti, w, axis=1)
    b1 = pl.pallas_call(
        _dft1_kernel,
        grid=(ncol // tn, bsz),
        in_specs=[
            pl.BlockSpec((None, 2, S2_D, tn), lambda j, b: (b, 0, 0, j)),
            pl.BlockSpec((2 * S2_D, 2 * S2_D), lambda j, b: (0, 0)),
            pl.BlockSpec((S2_D, tn), lambda j, b: (0, j)),
            pl.BlockSpec((S2_D, tn), lambda j, b: (0, j)),
        ],
        out_specs=pl.BlockSpec((None, 2, S2_D, tn), lambda j, b: (b, 0, 0, j)),
        out_shape=jax.ShapeDtypeStruct((bsz, 2, S2_D, ncol), BF16),
        compiler_params=_params("parallel", "parallel"),
        name="dft_stage1",
    )(vd.reshape(bsz, 2, S2_D, ncol), w2big, twr, twi)
    tk2 = TK2_D
    f = pl.pallas_call(
        _dft2_kernel,
        grid=(bsz, S2_D // tk2),
        in_specs=[
            pl.BlockSpec((None, 2, tk2, s1, w), lambda b, j: (b, 0, j, 0, 0)),
            pl.BlockSpec((s1, 2 * s1), lambda b, j: (0, 0)),
        ],
        out_specs=pl.BlockSpec((None, s1, tk2 * w), lambda b, j: (b, 0, j)),
        out_shape=jax.ShapeDtypeStruct((bsz, s1, S2_D * w), BF16),
        compiler_params=_params("parallel", "parallel"),
        name="dft_stage2",
    )(b1.reshape(bsz, 2, S2_D, s1, w), w1cat)
    return f.reshape(bsz, s, w)


def _finish(y, x_ref, gate_ref, g_ref, o_ref):
    ms = jnp.mean(y * y, axis=-1, keepdims=True)
    yn = y * lax.rsqrt(ms + EPS) * g_ref[...]
    o_ref[...] = x_ref[...] + gate_ref[...] * yn


def _outproj_even_kernel(oa_ref, o1_ref, o4_ref, o16_ref, l1_ref, l4_ref, l16_ref, gates_ref,
                         ex_ref, p4_ref, p16_ref, w_ref, x_ref, gate_ref, g_ref, o_ref):
    tm = x_ref.shape[0]
    ex = ex_ref[...]

    def natural(o_res, l_res, perm_ref):
        tiles = []
        for t in range(tm // T_RES):
            both = jnp.concatenate([o_res[t].reshape(T_RES, 512), l_res[t].reshape(T_RES, LANES)], axis=1)
            tiles.append(jnp.dot(perm_ref[...], both, preferred_element_type=F32))
        nat = jnp.concatenate(tiles, axis=0)
        return nat[:, 0:512], nat[:, 512:512 + LANES]

    def lse(x):
        out = x
        for part in range(1, LSE_PARTS):
            out = out + pltpu.roll(x, LANES - 8 * part, 1)
        return out

    o2, l2 = natural(o4_ref, l4_ref, p4_ref)
    o3, l3 = natural(o16_ref, l16_ref, p16_ref)
    l1, l2, l3 = lse(l1_ref[...].astype(F32)), lse(l2), lse(l3)
    mx = jnp.maximum(jnp.maximum(l1, l2), l3)
    e1, e2, e3 = jnp.exp2(l1 - mx), jnp.exp2(l2 - mx), jnp.exp2(l3 - mx)
    inv = 1.0 / (e1 + e2 + e3)
    terms = []
    for wgt in (e1 * inv, e2 * inv, e3 * inv):
        hi = wgt.astype(BF16)
        terms += [hi, (wgt - hi.astype(F32)).astype(BF16)]
    wide = jnp.dot(jnp.concatenate(terms, axis=0), ex, preferred_element_type=F32)
    w1, w2, w3 = (wide[2 * n * tm:(2 * n + 1) * tm] + wide[(2 * n + 1) * tm:(2 * n + 2) * tm] for n in range(3))
    ob = w1 * o1_ref[...].astype(F32) + w2 * o2 + w3 * o3
    ma = (oa_ref[...].astype(F32) * gates_ref[:, 0:512].astype(F32)).astype(BF16)
    mb = (ob * gates_ref[:, 512:1024].astype(F32)).astype(BF16)
    y = jnp.dot(jnp.concatenate([ma, mb], axis=1), w_ref[...], preferred_element_type=F32)
    _finish(y, x_ref, gate_ref, g_ref, o_ref)


def _outproj_even(oa, obs, lses, gates, ex, w_out, x, gate, post_g):
    bsz, s, d = x.shape
    tm = TM_PROJ
    return pl.pallas_call(
        _outproj_even_kernel,
        grid=(bsz, s // tm),
        in_specs=[_tok_spec(tm, 512), _tok_spec(tm, 512), _res_spec(tm, 4, 512), _res_spec(tm, 16, 512),
                  _tok_spec(tm, LANES), _res_spec(tm, 4, LANES), _res_spec(tm, 16, LANES),
                  _tok_spec(tm, 1024), _const_spec((LANES, 512)),
                  _const_spec((T_RES, T_RES)), _const_spec((T_RES, T_RES)), _const_spec((d, d)),
                  _tok_spec(tm, d), _bcast_spec(d), _const_spec((1, d))],
        out_specs=_tok_spec(tm, d),
        out_shape=jax.ShapeDtypeStruct((bsz, s, d), F32),
        compiler_params=_params("parallel", "parallel"),
        name="outproj_even",
    )(oa, *obs, *lses, gates, ex,
      jnp.asarray(_residue_perm(T_RES, 4).T, BF16), jnp.asarray(_residue_perm(T_RES, 16).T, BF16),
      w_out, x, gate, post_g)


def _outproj_odd_kernel(oc_ref, f_ref, gates_ref, lin_ref, w_ref, x_ref, gate_ref, g_ref, o_ref):
    od = jnp.dot(f_ref[...], lin_ref[...], preferred_element_type=F32)
    mc = (oc_ref[...].astype(F32) * gates_ref[:, 0:768].astype(F32)).astype(BF16)
    md = (od * gates_ref[:, 768:1024].astype(F32)).astype(BF16)
    y = jnp.dot(jnp.concatenate([mc, md], axis=1), w_ref[...], preferred_element_type=F32)
    _finish(y, x_ref, gate_ref, g_ref, o_ref)


def _outproj_odd(oc, f, gates, lin, w_out, x, gate, post_g):
    bsz, s, d = x.shape
    tm = TM_PROJ
    return pl.pallas_call(
        _outproj_odd_kernel,
        grid=(bsz, s // tm),
        in_specs=[_tok_spec(tm, 768), _tok_spec(tm, D_WIDTH), _tok_spec(tm, 1024),
                  _const_spec((D_WIDTH, D_WIDTH)), _const_spec((d, d)),
                  _tok_spec(tm, d), _bcast_spec(d), _const_spec((1, d))],
        out_specs=_tok_spec(tm, d),
        out_shape=jax.ShapeDtypeStruct((bsz, s, d), F32),
        compiler_params=_params("parallel", "parallel"),
        name="outproj_odd",
    )(oc, f, gates, lin, w_out, x, gate, post_g)


def _rope_tables(s):
    t = jnp.arange(s)

    def tab(pos, dim, theta):
        inv = theta ** (-jnp.arange(0, dim, 2, dtype=F32) / dim)
        ang = pos[:, None] * inv[None, :]
        return jnp.cos(ang), jnp.sin(ang)

    cr, sr = tab((t // GRID_W).astype(F32), HEAD_DIM // 2, A_ROPE_THETA)
    cc, sc = tab((t % GRID_W).astype(F32), HEAD_DIM // 2, A_ROPE_THETA)
    cb, sb = tab(t.astype(F32), B_ROPE_DIMS, B_ROPE_THETA)
    rest = HEAD_DIM - B_ROPE_DIMS
    ca = jnp.concatenate([cr, cr, cc, cc] * 2, axis=-1)
    sa = jnp.concatenate([-sr, sr, -sc, sc] * 2, axis=-1)
    cb = jnp.concatenate([cb, cb, jnp.ones((s, rest), F32)] * 2, axis=-1)
    sb = jnp.concatenate([-sb, sb, jnp.zeros((s, rest), F32)] * 2, axis=-1)
    return ca, sa, cb, sb


def _even_layer(x, mod, pre_g, post_g, w_in, w_out, qn, kn, tabs, ex):
    bsz, s, _ = x.shape
    shift, scale, gate = mod
    qa, ka, va, gates, qkv_b = _inproj_even(x, scale, shift, pre_g, w_in, qn, kn, tabs)
    oa = _attn_a(qa, ka, va)
    obs, lses = [], []
    for n, dil in enumerate(B_DILATIONS):
        q, k, v = (t[n] for t in qkv_b)
        if dil == 1:
            lq = min(LQ_B, s)
            q, k, v = (t.reshape(bsz, s // lq, 1, lq, t.shape[-1]) for t in (q, k, v))
        o, lse = _attn_b(q, k, v, dil)
        if dil == 1:
            o, lse = o.reshape(bsz, s, o.shape[-1]), lse.reshape(bsz, s, LANES)
        obs.append(o)
        lses.append(lse)
    return _outproj_even(oa, obs, lses, gates, ex, w_out, x, gate, post_g)


def _odd_layer(x, mod, pre_g, post_g, w_in, w_out, bias, lin, wc, dft):
    shift, scale, gate = mod
    qc, kc, vc, gates, vd = _inproj_odd(x, scale, shift, pre_g, w_in, wc)
    oc = _attn_c(qc, kc, vc, bias)
    f = _fourier(vd, dft)
    return _outproj_odd(oc, f, gates, lin, w_out, x, gate, post_g)


def _trunk(x, mods, pre_g, post_g, w_in_ab, w_out_ab, qn_a, kn_a, w_in_cd, w_out_cd, biases, lin_d):
    bsz, s, d = x.shape
    assert s % TM_PROJ == 0 and s % (16 * SUB_B) == 0
    tabs = _rope_tables(s)
    dft = _dft_consts(s)
    wc = _chan_dft_const()
    lane = np.arange(LANES)
    ex = jnp.asarray(lane[:, None] == np.arange(512)[None, :] // HEAD_DIM, BF16)
    for i in range(DEPTH):
        j = i // 2
        mod = tuple(mods[i][:, None, k * d:(k + 1) * d] for k in range(3))
        pg, qg = pre_g[i][None, :], post_g[i][None, :]
        if i % 2 == 0:
            qn = jnp.tile(qn_a[j], 2)[None, :]
            kn = jnp.tile(kn_a[j], 2)[None, :]
            x = _even_layer(x, mod, pg, qg, w_in_ab[j], w_out_ab[j], qn, kn, tabs, ex)
        else:
            x = _odd_layer(x, mod, pg, qg, w_in_cd[j], w_out_cd[j], biases[j], lin_d[j], wc, dft)
    return x


def kernel(x_prompt, x_sample, c_prompt, c_sample, pre_g, post_g, ada_w, ada_b,
           w_in_ab, w_out_ab, qn_a, kn_a, w_in_cd, w_out_cd, rpb_c, lin_d):
    nb = x_prompt.shape[0]
    mods = _adaln(jnp.concatenate([c_prompt, c_sample], axis=0), ada_w, ada_b)
    biases = [_bias_c(rpb_c[j]) for j in range(rpb_c.shape[0])]
    args = (pre_g, post_g, w_in_ab.astype(BF16), w_out_ab.astype(BF16), qn_a, kn_a,
            w_in_cd.astype(BF16), w_out_cd.astype(BF16), biases, lin_d.astype(BF16))
    y_prompt = _trunk(x_prompt, mods[:, :nb], *args)
    y_sample = _trunk(x_sample, mods[:, nb:], *args)
    return (y_prompt, y_sample)
```

```python
import functools
import math

import numpy as np
import jax
import jax.numpy as jnp
from jax import lax
from jax.experimental import pallas as pl
from jax.experimental.pallas import tpu as pltpu

F32 = jnp.float32
BF16 = jnp.bfloat16

D_MODEL = 1024
DEPTH = 4
HEAD_DIM = 64
GRID_W = 64
A_ROPE_THETA = 10000.0
B_ROPE_THETA = 500000.0
B_ROPE_DIMS = 16
B_DILATIONS = (1, 4, 16)
B_RADIUS = 64
C_HEADS = 12
C_WIN_H = 8
C_WIN_W = 16
D_GROUPS = 4
D_GROUP_DIM = 64
D_WIDTH = D_GROUPS * D_GROUP_DIM
AB_IN = 3328
CD_IN = 3584
EPS = 1e-6
NEG = -1e30
LOG2E = 1.4426950408889634
QK_SCALE = LOG2E * HEAD_DIM ** -0.5

LANES = 128
VMEM_LIMIT = 56 * 1024 * 1024

TM_PROJ = 512
T_RES = 256
TQ_A = 256
TK_A = 2048
ITEMS_A = 16
LQ_B = 1024
SUB_B = 128
ROWS_C = 4
TN_D1 = 4096
TK2_D = 8
LSE_PARTS = 3


def _params(*sem):
    return pltpu.CompilerParams(dimension_semantics=sem, vmem_limit_bytes=VMEM_LIMIT)


def _silu(x):
    return x / (1.0 + jnp.exp(-x))


def _lane_iota(rows):
    return lax.broadcasted_iota(jnp.int32, (rows, LANES), 1)


def _adaln_kernel(c_ref, w_ref, b_ref, o_ref):
    a = _silu(c_ref[...]).astype(BF16)
    o_ref[...] = jnp.dot(a, w_ref[...].astype(BF16), preferred_element_type=F32) + b_ref[...]


def _adaln(c_all, ada_w, ada_b):
    bt = c_all.shape[0]
    d = D_MODEL
    return pl.pallas_call(
        _adaln_kernel,
        grid=(DEPTH, 3),
        in_specs=[
            pl.BlockSpec((bt, d), lambda l, j: (0, 0)),
            pl.BlockSpec((None, d, d), lambda l, j: (l, 0, j)),
            pl.BlockSpec((None, 1, d), lambda l, j: (l, 0, j)),
        ],
        out_specs=pl.BlockSpec((None, bt, d), lambda l, j: (l, 0, j)),
        out_shape=jax.ShapeDtypeStruct((DEPTH, bt, 3 * d), F32),
        compiler_params=_params("parallel", "parallel"),
        name="adaln",
    )(c_all, ada_w, ada_b.reshape(DEPTH, 1, 3 * d))


def _modulated_norm(x, g, scale, shift):
    ms = jnp.mean(x * x, axis=-1, keepdims=True)
    y = x * lax.rsqrt(ms + EPS) * g
    return (y * (1.0 + scale) + shift).astype(BF16)


def _head_norm(xs, gain, lo):
    x2 = xs * xs
    s_lo = jnp.sum(jnp.where(lo, x2, 0.0), axis=1, keepdims=True)
    s_hi = jnp.sum(jnp.where(lo, 0.0, x2), axis=1, keepdims=True)
    ss = jnp.where(lo, s_lo, s_hi)
    return xs * lax.rsqrt(ss * (1.0 / HEAD_DIM) + EPS) * gain


def _rope(xs, cos, sin_signed, first_half, shift):
    up = pltpu.roll(xs, LANES - shift, 1)
    dn = pltpu.roll(xs, shift, 1)
    return xs * cos + jnp.where(first_half, up, dn) * sin_signed


def _inproj_even_kernel(x_ref, sc_ref, sh_ref, g_ref, w_ref, qn_ref, kn_ref,
                        ca_ref, sa_ref, cb_ref, sb_ref, p4_ref, p16_ref,
                        qa_ref, ka_ref, va_ref, gate_ref, *b_refs):
    tm = x_ref.shape[0]
    h = _modulated_norm(x_ref[...], g_ref[...], sc_ref[...], sh_ref[...])
    lane = _lane_iota(tm)
    half_a = (lane & 16) == 0
    half_b = (lane & 8) == 0
    lo = lane < HEAD_DIM
    ca, sa, cb, sb = ca_ref[...], sa_ref[...], cb_ref[...], sb_ref[...]
    full = jnp.dot(h, w_ref[...], preferred_element_type=F32)

    def proj(a, b):
        return full[:, a:b]

    def emit_b(vals):
        for n, val in enumerate(vals):
            b_refs[3 * n][...] = val
        cat = jnp.concatenate(vals, axis=1)
        for d, perm_ref in enumerate((p4_ref, p16_ref)):
            for t in range(tm // T_RES):
                pv = jnp.dot(perm_ref[...], cat[t * T_RES:(t + 1) * T_RES], preferred_element_type=F32)
                for n in range(3):
                    o_ref = b_refs[3 * n + 1 + d]
                    o_ref[t] = pv[:, 512 * n:512 * (n + 1)].astype(BF16).reshape(o_ref.shape[1:])

    p = proj(0, 512)
    lane_half = lane // HEAD_DIM
    for j in range(4):
        xs = _head_norm(p[:, LANES * j:LANES * (j + 1)], qn_ref[...], lo)
        xs = _rope(xs, ca, sa, half_a, 16) * QK_SCALE
        swapped = pltpu.roll(xs, HEAD_DIM, 1)
        for half in range(2):
            hd = 2 * j + half
            g, hh = hd // 4, hd % 4
            both = jnp.where(lane_half == half, xs, swapped)
            ht = jnp.where(lane_half == g, both, 0.0).T.astype(BF16)
            for t in range(tm // TQ_A):
                qa_ref[g, t, :, hh * TQ_A:(hh + 1) * TQ_A] = ht[:, t * TQ_A:(t + 1) * TQ_A]
    xs = _head_norm(proj(512, 640), kn_ref[...], lo)
    ka_ref[...] = _rope(xs, ca, sa, half_a, 16).astype(BF16)
    v = proj(640, 768)
    va_ref[0] = jnp.where(lo, v, 1.0).T.astype(BF16)
    va_ref[1] = jnp.where(lo, 1.0, v).T.astype(BF16)
    gate_ref[:, 0:512] = _silu(proj(768, 1280)).astype(BF16)
    p = proj(1280, 1792)
    qb = [(_rope(p[:, LANES * j:LANES * (j + 1)], cb, sb, half_b, 8) * QK_SCALE).astype(BF16)
          for j in range(4)]
    p = proj(1792, 2304)
    kb = [_rope(p[:, LANES * j:LANES * (j + 1)], cb, sb, half_b, 8).astype(BF16) for j in range(4)]
    emit_b([jnp.concatenate(qb, axis=1), jnp.concatenate(kb, axis=1), proj(2304, 2816).astype(BF16)])
    gate_ref[:, 512:1024] = _silu(proj(2816, 3328)).astype(BF16)


def _tok_spec(tm, width):
    return pl.BlockSpec((None, tm, width), lambda b, i: (b, i, 0))


def _res_spec(tm, dil, width):
    return pl.BlockSpec((None, tm // T_RES, dil, T_RES // dil, width), lambda b, i: (b, i, 0, 0, 0))


def _bcast_spec(width):
    return pl.BlockSpec((None, 1, width), lambda b, i: (b, 0, 0))


def _const_spec(shape):
    return pl.BlockSpec(shape, lambda b, i: (0,) * len(shape))


def _residue_perm(tm, dil):
    r = np.arange(tm)
    src = (r % (tm // dil)) * dil + r // (tm // dil)
    return np.asarray(r[None, :] == src[:, None], np.float32)


def _inproj_even(x, scale, shift, pre_g, w_in, qn, kn, tabs):
    bsz, s, d = x.shape
    tm = TM_PROJ
    ca, sa, cb, sb = tabs
    tab_spec = pl.BlockSpec((tm, LANES), lambda b, i: (i, 0))
    b_specs, b_shapes = [], []
    for _ in range(3):
        b_specs.append(_tok_spec(tm, 512))
        b_shapes.append(jax.ShapeDtypeStruct((bsz, s, 512), BF16))
        for dil in B_DILATIONS[1:]:
            b_specs.append(_res_spec(tm, dil, 512))
            b_shapes.append(jax.ShapeDtypeStruct((bsz, s // T_RES, dil, T_RES // dil, 512), BF16))
    outs = pl.pallas_call(
        _inproj_even_kernel,
        grid=(bsz, s // tm),
        in_specs=[
            _tok_spec(tm, d), _bcast_spec(d), _bcast_spec(d), _const_spec((1, d)),
            _const_spec((d, AB_IN)), _const_spec((1, LANES)), _const_spec((1, LANES)),
            tab_spec, tab_spec, tab_spec, tab_spec,
            _const_spec((T_RES, T_RES)), _const_spec((T_RES, T_RES)),
        ],
        out_specs=[pl.BlockSpec((None, 2, tm // TQ_A, LANES, 4 * TQ_A), lambda b, i: (b, 0, i, 0, 0)),
                   _tok_spec(tm, LANES),
                   pl.BlockSpec((None, 2, LANES, tm), lambda b, i: (b, 0, 0, i)),
                   _tok_spec(tm, 1024)] + b_specs,
        out_shape=[jax.ShapeDtypeStruct((bsz, 2, s // TQ_A, LANES, 4 * TQ_A), BF16),
                   jax.ShapeDtypeStruct((bsz, s, LANES), BF16),
                   jax.ShapeDtypeStruct((bsz, 2, LANES, s), BF16),
                   jax.ShapeDtypeStruct((bsz, s, 1024), BF16)] + b_shapes,
        compiler_params=_params("parallel", "parallel"),
        name="inproj_even",
    )(x, scale, shift, pre_g, w_in, qn, kn, ca, sa, cb, sb,
      jnp.asarray(_residue_perm(T_RES, 4), BF16), jnp.asarray(_residue_perm(T_RES, 16), BF16))
    qa, ka, va, gates = outs[:4]
    qkv_b = [outs[4 + 3 * n:7 + 3 * n] for n in range(3)]
    return qa, ka, va, gates, qkv_b


def _inproj_odd_kernel(x_ref, sc_ref, sh_ref, g_ref, w_ref, wc_ref,
                       q_ref, k_ref, v_ref, gate_ref, vd_ref):
    h = _modulated_norm(x_ref[...], g_ref[...], sc_ref[...], sh_ref[...])
    full = jnp.dot(h, w_ref[...], preferred_element_type=F32)

    def proj(a, b):
        return full[:, a:b]

    q_ref[...] = (proj(0, 768) * QK_SCALE).astype(BF16)
    k_ref[...] = proj(768, 1536).astype(BF16)
    v_ref[...] = proj(1536, 2304).astype(BF16)
    gate_ref[:, 0:768] = _silu(proj(2304, 3072)).astype(BF16)
    u = proj(3072, 3328).astype(BF16)
    vc = jnp.dot(u, wc_ref[...], preferred_element_type=F32)
    vd_ref[0] = vc[:, 0:D_WIDTH].astype(BF16)
    vd_ref[1] = vc[:, D_WIDTH:2 * D_WIDTH].astype(BF16)
    gate_ref[:, 768:1024] = _silu(proj(3328, 3584)).astype(BF16)


def _inproj_odd(x, scale, shift, pre_g, w_in, wc):
    bsz, s, d = x.shape
    tm = TM_PROJ
    widths = (768, 768, 768, 1024)
    return pl.pallas_call(
        _inproj_odd_kernel,
        grid=(bsz, s // tm),
        in_specs=[
            _tok_spec(tm, d), _bcast_spec(d), _bcast_spec(d), _const_spec((1, d)),
            _const_spec((d, CD_IN)), _const_spec((D_WIDTH, 2 * D_WIDTH)),
        ],
        out_specs=[_tok_spec(tm, w) for w in widths]
        + [pl.BlockSpec((None, 2, tm, D_WIDTH), lambda b, i: (b, 0, i, 0))],
        out_shape=[jax.ShapeDtypeStruct((bsz, s, w), BF16) for w in widths]
        + [jax.ShapeDtypeStruct((bsz, 2, s, D_WIDTH), BF16)],
        compiler_params=_params("parallel", "parallel"),
        name="inproj_odd",
    )(x, scale, shift, pre_g, w_in, wc)


def _attn_a_kernel(q_ref, k_ref, v_ref, o_ref, s_ref, *, tk):
    ntile = q_ref.shape[0]
    tq = o_ref.shape[0] // ntile
    nchunk = k_ref.shape[0] // tk
    first = pl.program_id(1) == 0
    items = [(t, j) for t in range(ntile) for j in range(nchunk)]

    def scores(n):
        t, j = items[n]
        s_ref[n % 2] = jnp.dot(k_ref[j * tk:(j + 1) * tk, :], q_ref[t], preferred_element_type=F32)

    scores(0)
    for n, (t, j) in enumerate(items):
        if n + 1 < len(items):
            scores(n + 1)
        if j == 0:
            m = jnp.full((1, 4 * tq), NEG, F32)
            acc = jnp.zeros((LANES, 4 * tq), F32)
        s = s_ref[n % 2]
        vc = v_ref[:, j * tk:(j + 1) * tk]
        m_new = jnp.maximum(m, jnp.max(s, axis=0, keepdims=True))
        p = jnp.exp2(s - m_new).astype(BF16)
        acc = jnp.exp2(m - m_new) * acc + jnp.dot(vc, p, preferred_element_type=F32)
        m = m_new
        if j == nchunk - 1:
            num = jnp.where(first, acc[0:HEAD_DIM], acc[HEAD_DIM:LANES])
            den = jnp.where(first, acc[HEAD_DIM:HEAD_DIM + 1], acc[0:1])
            o = num / den
            for sl in range(2):
                pair = jnp.concatenate([o[:, (2 * sl) * tq:(2 * sl + 1) * tq],
                                        o[:, (2 * sl + 1) * tq:(2 * sl + 2) * tq]], axis=0)
                o_ref[t * tq:(t + 1) * tq, LANES * sl:LANES * (sl + 1)] = pair.T.astype(BF16)


def _attn_a(q, k, v):
    bsz, s, _ = k.shape
    tq = TQ_A
    tk = min(TK_A, s)
    ntile = max(1, min(ITEMS_A * tk // s, s // tq))
    assert s % (ntile * tq) == 0 and s % tk == 0
    return pl.pallas_call(
        functools.partial(_attn_a_kernel, tk=tk),
        grid=(bsz, 2, s // (ntile * tq)),
        in_specs=[
            pl.BlockSpec((None, None, ntile, LANES, 4 * tq), lambda b, g, i: (b, g, i, 0, 0)),
            pl.BlockSpec((None, s, LANES), lambda b, g, i: (b, 0, 0)),
            pl.BlockSpec((None, None, LANES, s), lambda b, g, i: (b, g, 0, 0)),
        ],
        out_specs=pl.BlockSpec((None, ntile * tq, 2 * LANES), lambda b, g, i: (b, i, g)),
        out_shape=jax.ShapeDtypeStruct((bsz, s, 512), BF16),
        scratch_shapes=[pltpu.VMEM((2, tk, 4 * tq), F32)],
        compiler_params=_params("parallel", "arbitrary", "arbitrary"),
        name="attn_a",
    )(q, k, v)


def _window(refs, start, size, cols, lq):
    rpt = refs[0].shape[1]
    pieces = []
    for n, r in enumerate(refs):
        for t in range(r.shape[0]):
            t0 = n * lq + t * rpt
            a, b = max(start, t0), min(start + size, t0 + rpt)
            if a < b:
                pieces.append(r[t, a - t0:b - t0, cols])
    return pieces[0] if len(pieces) == 1 else jnp.concatenate(pieces, axis=0)


def _store_rows(ref, r0, val, cols):
    rpt = ref.shape[1]
    n = val.shape[0]
    for t in range(ref.shape[0]):
        a, b = max(r0, t * rpt), min(r0 + n, (t + 1) * rpt)
        if a < b:
            ref[t, a - t * rpt:b - t * rpt, cols] = val[a - r0:b - r0]


def _attn_b_kernel(q_ref, kp_ref, kc_ref, kn_ref, vp_ref, vc_ref, vn_ref, o_ref, lse_ref, s_ref, *, l_len):
    lq = q_ref.shape[0] * q_ref.shape[1]
    sub = min(SUB_B, lq)
    base = pl.program_id(2) * lq
    lane = _lane_iota(sub)
    lo = lane < HEAD_DIM
    rq = lax.broadcasted_iota(jnp.int32, (sub, 2 * sub), 0)
    ck = lax.broadcasted_iota(jnp.int32, (sub, 2 * sub), 1)
    delta = ck - rq - sub // 2
    band = jnp.abs(delta) <= B_RADIUS
    items = [(sb, j) for sb in range(lq // sub) for j in range(4)]

    def scores(n):
        sb, j = items[n]
        r0 = sb * sub
        cols = slice(LANES * j, LANES * (j + 1))
        qs = _window((q_ref,), r0, sub, cols, lq).astype(F32)
        kw = _window((kp_ref, kc_ref, kn_ref), lq + r0 - sub // 2, 2 * sub, cols, lq)
        q2 = jnp.concatenate([jnp.where(lo, qs, 0.0), jnp.where(lo, 0.0, qs)], axis=0).astype(BF16)
        s_ref[n % 2] = lax.dot_general(q2, kw, (((1,), (1,)), ((), ())), preferred_element_type=F32)

    scores(0)
    for n, (sb, j) in enumerate(items):
        if n + 1 < len(items):
            scores(n + 1)
        r0 = sb * sub
        cols = slice(LANES * j, LANES * (j + 1))
        if j == 0:
            kpos = jnp.where(band, base + (r0 - sub // 2) + ck, -1)
            mask = jnp.where(jnp.logical_and(kpos >= 0, kpos < l_len), 0.0, NEG)
            mask2 = jnp.concatenate([mask, mask], axis=0)
            lse_tile = jnp.zeros((sub, LANES), F32)
        vw = _window((vp_ref, vc_ref, vn_ref), lq + r0 - sub // 2, 2 * sub, cols, lq)
        s = s_ref[n % 2] + mask2
        m = jnp.max(s, axis=1, keepdims=True)
        p = jnp.exp2(s - m)
        l = jnp.sum(p, axis=1, keepdims=True)
        o = jnp.dot(p.astype(BF16), vw, preferred_element_type=F32) * (1.0 / l)
        lse = m + jnp.log2(l)
        lse_tile = (lse_tile + jnp.where(lane == 2 * j, lse[0:sub], 0.0)
                    + jnp.where(lane == 2 * j + 1, lse[sub:2 * sub], 0.0))
        _store_rows(o_ref, r0, jnp.where(lo, o[0:sub], o[sub:2 * sub]).astype(BF16), cols)
        if j == 3:
            packed = jnp.zeros((sub, LANES), F32)
            rest = lse_tile
            for part in range(LSE_PARTS):
                term = rest.astype(BF16).astype(F32)
                rest = rest - term
                packed = packed + (pltpu.roll(term, 8 * part, 1) if part else term)
            _store_rows(lse_ref, r0, packed.astype(BF16), slice(None))


def _attn_b(q, k, v, dil):
    bsz, ntile, _, rpt, w = q.shape
    l_len = ntile * rpt
    lq = min(LQ_B, l_len)
    nt = lq // rpt
    nblk = l_len // lq

    def spec(width, shift):
        def index(b, r, i):
            return (b, jnp.clip(i + shift, 0, nblk - 1), r, 0, 0)
        return pl.BlockSpec((None, nt, None, rpt, width), index)

    cur, prv, nxt = spec(w, 0), spec(w, -1), spec(w, 1)
    return pl.pallas_call(
        functools.partial(_attn_b_kernel, l_len=l_len),
        grid=(bsz, dil, nblk),
        in_specs=[cur, prv, cur, nxt, prv, cur, nxt],
        out_specs=[cur, spec(LANES, 0)],
        out_shape=[jax.ShapeDtypeStruct(q.shape, BF16),
                   jax.ShapeDtypeStruct((bsz, ntile, dil, rpt, LANES), BF16)],
        scratch_shapes=[pltpu.VMEM((2, 2 * min(SUB_B, lq), 2 * min(SUB_B, lq)), F32)],
        compiler_params=_params("parallel", "parallel", "parallel"),
        name=f"attn_b_d{dil}",
    )(q, k, k, k, v, v, v)


def _attn_c_kernel(q_ref, kp_ref, kc_ref, kn_ref, vp_ref, vc_ref, vn_ref, bias_ref, o_ref, s_ref):
    tq = q_ref.shape[0]
    lane = _lane_iota(tq)
    lo = lane < HEAD_DIM
    npair = C_HEADS // 2

    def scores(pr):
        cols = slice(LANES * pr, LANES * (pr + 1))
        qs = q_ref[:, cols].astype(F32)
        k3 = jnp.concatenate([kp_ref[:, cols], kc_ref[:, cols], kn_ref[:, cols]], axis=0)
        q2 = jnp.concatenate([jnp.where(lo, qs, 0.0), jnp.where(lo, 0.0, qs)], axis=0).astype(BF16)
        s_ref[pr % 2] = lax.dot_general(q2, k3, (((1,), (1,)), ((), ())), preferred_element_type=F32)

    scores(0)
    for pr in range(npair):
        if pr + 1 < npair:
            scores(pr + 1)
        cols = slice(LANES * pr, LANES * (pr + 1))
        v3 = jnp.concatenate([vp_ref[:, cols], vc_ref[:, cols], vn_ref[:, cols]], axis=0)
        s = s_ref[pr % 2] + bias_ref[2 * pr:2 * pr + 2].reshape(2 * tq, 3 * tq)
        m = jnp.max(s, axis=1, keepdims=True)
        p = jnp.exp2(s - m)
        l = jnp.sum(p, axis=1, keepdims=True)
        o = jnp.dot(p.astype(BF16), v3, preferred_element_type=F32) * (1.0 / l)
        o_ref[:, cols] = jnp.where(lo, o[0:tq], o[tq:2 * tq]).astype(BF16)


def _bias_c(rpb):
    nq, nu = ROWS_C, 3 * ROWS_C
    assert nq == C_WIN_H // 2
    nrel = 2 * C_WIN_H - 1
    scaled = rpb.astype(F32) * LOG2E
    per_col = []
    for c in range(GRID_W):
        c0 = min(max(c - C_WIN_W // 2, 0), GRID_W - C_WIN_W)
        win = scaled[:, :, c0 - c + C_WIN_W - 1:c0 - c + 2 * C_WIN_W - 1]
        per_col.append(jnp.pad(win, ((0, 0), (0, 0), (c0, GRID_W - C_WIN_W - c0)), constant_values=NEG))
    tab = jnp.stack(per_col, axis=2)
    dead = jnp.full((C_HEADS, GRID_W, GRID_W), NEG, F32)
    kinds = []
    for kind in range(3):
        rows = []
        for j in range(nq):
            slots = []
            for u in range(nu):
                rel = (u - nq) - j + C_WIN_H - 1
                if kind == 0:
                    ok = nq <= u < nq + C_WIN_H
                elif kind == 2:
                    ok = 2 * nq - C_WIN_H <= u < 2 * nq
                else:
                    ok = nq - C_WIN_H // 2 <= u - j < nq + C_WIN_H // 2
                slots.append(tab[:, rel] if ok and 0 <= rel < nrel else dead)
            rows.append(jnp.concatenate(slots, axis=2))
        kinds.append(jnp.concatenate(rows, axis=1))
    return jnp.stack(kinds, axis=0)


def _attn_c(q, k, v, bias):
    bsz, s, w = q.shape
    tq = ROWS_C * GRID_W
    nblk = s // tq
    assert nblk >= 2
    kind = lambda i: jnp.where(i == 0, 0, jnp.where(i == nblk - 1, 2, 1))
    cur = pl.BlockSpec((None, tq, w), lambda b, i: (b, i, 0))
    prv = pl.BlockSpec((None, tq, w), lambda b, i: (b, jnp.maximum(i - 1, 0), 0))
    nxt = pl.BlockSpec((None, tq, w), lambda b, i: (b, jnp.minimum(i + 1, nblk - 1), 0))
    return pl.pallas_call(
        _attn_c_kernel,
        grid=(bsz, nblk),
        in_specs=[cur, prv, cur, nxt, prv, cur, nxt,
                  pl.BlockSpec((None, C_HEADS, tq, 3 * tq), lambda b, i: (kind(i), 0, 0, 0))],
        out_specs=cur,
        out_shape=jax.ShapeDtypeStruct((bsz, s, w), BF16),
        scratch_shapes=[pltpu.VMEM((2, 2 * tq, 3 * tq), F32)],
        compiler_params=_params("parallel", "arbitrary"),
        name="attn_c",
    )(q, k, k, k, v, v, v, bias)


S2_D = 64


def _dft_consts(s):
    s1 = s // S2_D
    th2 = 2 * np.pi * np.outer(np.arange(S2_D), np.arange(S2_D)) / S2_D
    c2, n2 = np.cos(th2) / 8.0, np.sin(th2) / 8.0
    w2big = np.block([[c2, n2], [-n2, c2]])
    th1 = 2 * np.pi * np.outer(np.arange(s1), np.arange(s1)) / s1
    w1cat = np.concatenate([np.cos(th1), np.sin(th1)], axis=1) / math.sqrt(s1)
    tht = 2 * np.pi * np.outer(np.arange(S2_D), np.arange(s1)) / s
    return (jnp.asarray(w2big, BF16), jnp.asarray(w1cat, BF16),
            jnp.asarray(np.cos(tht), F32), jnp.asarray(-np.sin(tht), F32))


def _chan_dft_const():
    th = 2 * np.pi * np.outer(np.arange(D_GROUP_DIM), np.arange(D_GROUP_DIM)) / D_GROUP_DIM
    eye = np.eye(D_GROUPS)
    wc = np.concatenate([np.kron(eye, np.cos(th)), np.kron(eye, -np.sin(th))], axis=1) / 8.0
    return jnp.asarray(wc, BF16)


def _dft1_kernel(v_ref, w_ref, tr_ref, ti_ref, o_ref):
    x = jnp.concatenate([v_ref[0], v_ref[1]], axis=0)
    b = jnp.dot(w_ref[...], x, preferred_element_type=F32)
    br, bi = b[0:S2_D], b[S2_D:2 * S2_D]
    tr, ti = tr_ref[...], ti_ref[...]
    o_ref[0] = (br * tr - bi * ti).astype(BF16)
    o_ref[1] = (br * ti + bi * tr).astype(BF16)


def _dft2_kernel(b_ref, w_ref, o_ref):
    for kk in range(b_ref.shape[1]):
        rhs = jnp.concatenate([b_ref[0, kk], b_ref[1, kk]], axis=0)
        x = jnp.dot(w_ref[...], rhs, preferred_element_type=F32)
        o_ref[:, D_WIDTH * kk:D_WIDTH * (kk + 1)] = x.astype(BF16)


def _fourier(vd, consts):
    bsz, _, s, w = vd.shape
    s1 = s // S2_D
    w2big, w1cat, tr, ti = consts
    ncol = s1 * w
    tn = min(TN_D1, ncol)
    twr = jnp.repeat(tr, w, axis=1)
    twi = jnp.repeat(ti, w, axis=1)
    b1 = pl.pallas_call(
        _dft1_kernel,
        grid=(ncol // tn, bsz),
        in_specs=[
            pl.BlockSpec((None, 2, S2_D, tn), lambda j, b: (b, 0, 0, j)),
            pl.BlockSpec((2 * S2_D, 2 * S2_D), lambda j, b: (0, 0)),
            pl.BlockSpec((S2_D, tn), lambda j, b: (0, j)),
            pl.BlockSpec((S2_D, tn), lambda j, b: (0, j)),
        ],
        out_specs=pl.BlockSpec((None, 2, S2_D, tn), lambda j, b: (b, 0, 0, j)),
        out_shape=jax.ShapeDtypeStruct((bsz, 2, S2_D, ncol), BF16),
        compiler_params=_params("parallel", "parallel"),
        name="dft_stage1",
    )(vd.reshape(bsz, 2, S2_D, ncol), w2big, twr, twi)
    tk2 = TK2_D
    f = pl.pallas_call(
        _dft2_kernel,
        grid=(bsz, S2_D // tk2),
        in_specs=[
            pl.BlockSpec((None, 2, tk2, s1, w), lambda b, j: (b, 0, j, 0, 0)),
            pl.BlockSpec((s1, 2 * s1), lambda b, j: (0, 0)),
        ],
        out_specs=pl.BlockSpec((None, s1, tk2 * w), lambda b, j: (b, 0, j)),
        out_shape=jax.ShapeDtypeStruct((bsz, s1, S2_D * w), BF16),
        compiler_params=_params("parallel", "parallel"),
        name="dft_stage2",
    )(b1.reshape(bsz, 2, S2_D, s1, w), w1cat)
    return f.reshape(bsz, s, w)


def _finish(y, x_ref, gate_ref, g_ref, o_ref):
    ms = jnp.mean(y * y, axis=-1, keepdims=True)
    yn = y * lax.rsqrt(ms + EPS) * g_ref[...]
    o_ref[...] = x_ref[...] + gate_ref[...] * yn


def _outproj_even_kernel(oa_ref, o1_ref, o4_ref, o16_ref, l1_ref, l4_ref, l16_ref, gates_ref,
                         ex_ref, p4_ref, p16_ref, w_ref, x_ref, gate_ref, g_ref, o_ref):
    tm = x_ref.shape[0]
    ex = ex_ref[...]

    def natural(o_res, l_res, perm_ref):
        tiles = []
        for t in range(tm // T_RES):
            both = jnp.concatenate([o_res[t].reshape(T_RES, 512), l_res[t].reshape(T_RES, LANES)], axis=1)
            tiles.append(jnp.dot(perm_ref[...], both, preferred_element_type=F32))
        nat = jnp.concatenate(tiles, axis=0)
        return nat[:, 0:512], nat[:, 512:512 + LANES]

    def lse(x):
        out = x
        for part in range(1, LSE_PARTS):
            out = out + pltpu.roll(x, LANES - 8 * part, 1)
        return out

    o2, l2 = natural(o4_ref, l4_ref, p4_ref)
    o3, l3 = natural(o16_ref, l16_ref, p16_ref)
    l1, l2, l3 = lse(l1_ref[...].astype(F32)), lse(l2), lse(l3)
    mx = jnp.maximum(jnp.maximum(l1, l2), l3)
    e1, e2, e3 = jnp.exp2(l1 - mx), jnp.exp2(l2 - mx), jnp.exp2(l3 - mx)
    inv = 1.0 / (e1 + e2 + e3)
    terms = []
    for wgt in (e1 * inv, e2 * inv, e3 * inv):
        hi = wgt.astype(BF16)
        terms += [hi, (wgt - hi.astype(F32)).astype(BF16)]
    wide = jnp.dot(jnp.concatenate(terms, axis=0), ex, preferred_element_type=F32)
    w1, w2, w3 = (wide[2 * n * tm:(2 * n + 1) * tm] + wide[(2 * n + 1) * tm:(2 * n + 2) * tm] for n in range(3))
    ob = w1 * o1_ref[...].astype(F32) + w2 * o2 + w3 * o3
    ma = (oa_ref[...].astype(F32) * gates_ref[:, 0:512].astype(F32)).astype(BF16)
    mb = (ob * gates_ref[:, 512:1024].astype(F32)).astype(BF16)
    y = jnp.dot(jnp.concatenate([ma, mb], axis=1), w_ref[...], preferred_element_type=F32)
    _finish(y, x_ref, gate_ref, g_ref, o_ref)


def _outproj_even(oa, obs, lses, gates, ex, w_out, x, gate, post_g):
    bsz, s, d = x.shape
    tm = TM_PROJ
    return pl.pallas_call(
        _outproj_even_kernel,
        grid=(bsz, s // tm),
        in_specs=[_tok_spec(tm, 512), _tok_spec(tm, 512), _res_spec(tm, 4, 512), _res_spec(tm, 16, 512),
                  _tok_spec(tm, LANES), _res_spec(tm, 4, LANES), _res_spec(tm, 16, LANES),
                  _tok_spec(tm, 1024), _const_spec((LANES, 512)),
                  _const_spec((T_RES, T_RES)), _const_spec((T_RES, T_RES)), _const_spec((d, d)),
                  _tok_spec(tm, d), _bcast_spec(d), _const_spec((1, d))],
        out_specs=_tok_spec(tm, d),
        out_shape=jax.ShapeDtypeStruct((bsz, s, d), F32),
        compiler_params=_params("parallel", "parallel"),
        name="outproj_even",
    )(oa, *obs, *lses, gates, ex,
      jnp.asarray(_residue_perm(T_RES, 4).T, BF16), jnp.asarray(_residue_perm(T_RES, 16).T, BF16),
      w_out, x, gate, post_g)


def _outproj_odd_kernel(oc_ref, f_ref, gates_ref, lin_ref, w_ref, x_ref, gate_ref, g_ref, o_ref):
    od = jnp.dot(f_ref[...], lin_ref[...], preferred_element_type=F32)
    mc = (oc_ref[...].astype(F32) * gates_ref[:, 0:768].astype(F32)).astype(BF16)
    md = (od * gates_ref[:, 768:1024].astype(F32)).astype(BF16)
    y = jnp.dot(jnp.concatenate([mc, md], axis=1), w_ref[...], preferred_element_type=F32)
    _finish(y, x_ref, gate_ref, g_ref, o_ref)


def _outproj_odd(oc, f, gates, lin, w_out, x, gate, post_g):
    bsz, s, d = x.shape
    tm = TM_PROJ
    return pl.pallas_call(
        _outproj_odd_kernel,
        grid=(bsz, s // tm),
        in_specs=[_tok_spec(tm, 768), _tok_spec(tm, D_WIDTH), _tok_spec(tm, 1024),
                  _const_spec((D_WIDTH, D_WIDTH)), _const_spec((d, d)),
                  _tok_spec(tm, d), _bcast_spec(d), _const_spec((1, d))],
        out_specs=_tok_spec(tm, d),
        out_shape=jax.ShapeDtypeStruct((bsz, s, d), F32),
        compiler_params=_params("parallel", "parallel"),
        name="outproj_odd",
    )(oc, f, gates, lin, w_out, x, gate, post_g)


def _rope_tables(s):
    t = jnp.arange(s)

    def tab(pos, dim, theta):
        inv = theta ** (-jnp.arange(0, dim, 2, dtype=F32) / dim)
        ang = pos[:, None] * inv[None, :]
        return jnp.cos(ang), jnp.sin(ang)

    cr, sr = tab((t // GRID_W).astype(F32), HEAD_DIM // 2, A_ROPE_THETA)
    cc, sc = tab((t % GRID_W).astype(F32), HEAD_DIM // 2, A_ROPE_THETA)
    cb, sb = tab(t.astype(F32), B_ROPE_DIMS, B_ROPE_THETA)
    rest = HEAD_DIM - B_ROPE_DIMS
    ca = jnp.concatenate([cr, cr, cc, cc] * 2, axis=-1)
    sa = jnp.concatenate([-sr, sr, -sc, sc] * 2, axis=-1)
    cb = jnp.concatenate([cb, cb, jnp.ones((s, rest), F32)] * 2, axis=-1)
    sb = jnp.concatenate([-sb, sb, jnp.zeros((s, rest), F32)] * 2, axis=-1)
    return ca, sa, cb, sb


def _even_layer(x, mod, pre_g, post_g, w_in, w_out, qn, kn, tabs, ex):
    bsz, s, _ = x.shape
    shift, scale, gate = mod
    qa, ka, va, gates, qkv_b = _inproj_even(x, scale, shift, pre_g, w_in, qn, kn, tabs)
    oa = _attn_a(qa, ka, va)
    obs, lses = [], []
    for n, dil in enumerate(B_DILATIONS):
        q, k, v = (t[n] for t in qkv_b)
        if dil == 1:
            lq = min(LQ_B, s)
            q, k, v = (t.reshape(bsz, s // lq, 1, lq, t.shape[-1]) for t in (q, k, v))
        o, lse = _attn_b(q, k, v, dil)
        if dil == 1:
            o, lse = o.reshape(bsz, s, o.shape[-1]), lse.reshape(bsz, s, LANES)
        obs.append(o)
        lses.append(lse)
    return _outproj_even(oa, obs, lses, gates, ex, w_out, x, gate, post_g)


def _odd_layer(x, mod, pre_g, post_g, w_in, w_out, bias, lin, wc, dft):
    shift, scale, gate = mod
    qc, kc, vc, gates, vd = _inproj_odd(x, scale, shift, pre_g, w_in, wc)
    oc = _attn_c(qc, kc, vc, bias)
    f = _fourier(vd, dft)
    return _outproj_odd(oc, f, gates, lin, w_out, x, gate, post_g)


def _trunk(x, mods, pre_g, post_g, w_in_ab, w_out_ab, qn_a, kn_a, w_in_cd, w_out_cd, biases, lin_d):
    bsz, s, d = x.shape
    assert s % TM_PROJ == 0 and s % (16 * SUB_B) == 0
    tabs = _rope_tables(s)
    dft = _dft_consts(s)
    wc = _chan_dft_const()
    lane = np.arange(LANES)
    ex = jnp.asarray(lane[:, None] == np.arange(512)[None, :] // HEAD_DIM, BF16)
    for i in range(DEPTH):
        j = i // 2
        mod = tuple(mods[i][:, None, k * d:(k + 1) * d] for k in range(3))
        pg, qg = pre_g[i][None, :], post_g[i][None, :]
        if i % 2 == 0:
            qn = jnp.tile(qn_a[j], 2)[None, :]
            kn = jnp.tile(kn_a[j], 2)[None, :]
            x = _even_layer(x, mod, pg, qg, w_in_ab[j], w_out_ab[j], qn, kn, tabs, ex)
        else:
            x = _odd_layer(x, mod, pg, qg, w_in_cd[j], w_out_cd[j], biases[j], lin_d[j], wc, dft)
    return x


def kernel(x_prompt, x_sample, c_prompt, c_sample, pre_g, post_g, ada_w, ada_b,
           w_in_ab, w_out_ab, qn_a, kn_a, w_in_cd, w_out_cd, rpb_c, lin_d):
    nb = x_prompt.shape[0]
    mods = _adaln(jnp.concatenate([c_prompt, c_sample], axis=0), ada_w, ada_b)
    biases = [_bias_c(rpb_c[j]) for j in range(rpb_c.shape[0])]
    args = (pre_g, post_g, w_in_ab.astype(BF16), w_out_ab.astype(BF16), qn_a, kn_a,
            w_in_cd.astype(BF16), w_out_cd.astype(BF16), biases, lin_d.astype(BF16))
    y_prompt = _trunk(x_prompt, mods[:, :nb], *args)
    y_sample = _trunk(x_sample, mods[:, nb:], *args)
    return (y_prompt, y_sample)
```

```python
import functools
import math

import numpy as np
import jax
import jax.numpy as jnp
from jax import lax
from jax.experimental import pallas as pl
from jax.experimental.pallas import tpu as pltpu

F32 = jnp.float32
BF16 = jnp.bfloat16

D_MODEL = 1024
DEPTH = 4
HEAD_DIM = 64
GRID_W = 64
A_ROPE_THETA = 10000.0
B_ROPE_THETA = 500000.0
B_ROPE_DIMS = 16
B_DILATIONS = (1, 4, 16)
B_RADIUS = 64
C_HEADS = 12
C_WIN_H = 8
C_WIN_W = 16
D_GROUPS = 4
D_GROUP_DIM = 64
D_WIDTH = D_GROUPS * D_GROUP_DIM
AB_IN = 3328
CD_IN = 3584
EPS = 1e-6
NEG = -1e30
LOG2E = 1.4426950408889634
QK_SCALE = LOG2E * HEAD_DIM ** -0.5

LANES = 128
VMEM_LIMIT = 56 * 1024 * 1024

TM_PROJ = 512
T_RES = 256
TQ_A = 256
TK_A = 2048
ITEMS_A = 16
LQ_B = 512
SUB_B = 128
ROWS_C = 4
TN_D1 = 4096
TK2_D = 8
LSE_PARTS = 3


def _params(*sem):
    return pltpu.CompilerParams(dimension_semantics=sem, vmem_limit_bytes=VMEM_LIMIT)


def _silu(x):
    return x / (1.0 + jnp.exp(-x))


def _lane_iota(rows):
    return lax.broadcasted_iota(jnp.int32, (rows, LANES), 1)


def _adaln_kernel(c_ref, w_ref, b_ref, o_ref):
    a = _silu(c_ref[...]).astype(BF16)
    o_ref[...] = jnp.dot(a, w_ref[...].astype(BF16), preferred_element_type=F32) + b_ref[...]


def _adaln(c_all, ada_w, ada_b):
    bt = c_all.shape[0]
    d = D_MODEL
    return pl.pallas_call(
        _adaln_kernel,
        grid=(DEPTH, 3),
        in_specs=[
            pl.BlockSpec((bt, d), lambda l, j: (0, 0)),
            pl.BlockSpec((None, d, d), lambda l, j: (l, 0, j)),
            pl.BlockSpec((None, 1, d), lambda l, j: (l, 0, j)),
        ],
        out_specs=pl.BlockSpec((None, bt, d), lambda l, j: (l, 0, j)),
        out_shape=jax.ShapeDtypeStruct((DEPTH, bt, 3 * d), F32),
        compiler_params=_params("parallel", "parallel"),
        name="adaln",
    )(c_all, ada_w, ada_b.reshape(DEPTH, 1, 3 * d))


def _modulated_norm(x, g, scale, shift):
    ms = jnp.mean(x * x, axis=-1, keepdims=True)
    y = x * lax.rsqrt(ms + EPS) * g
    return (y * (1.0 + scale) + shift).astype(BF16)


def _head_norm(xs, gain, lo):
    x2 = xs * xs
    s_lo = jnp.sum(jnp.where(lo, x2, 0.0), axis=1, keepdims=True)
    s_hi = jnp.sum(jnp.where(lo, 0.0, x2), axis=1, keepdims=True)
    ss = jnp.where(lo, s_lo, s_hi)
    return xs * lax.rsqrt(ss * (1.0 / HEAD_DIM) + EPS) * gain


def _rope(xs, cos, sin_signed, first_half, shift):
    up = pltpu.roll(xs, LANES - shift, 1)
    dn = pltpu.roll(xs, shift, 1)
    return xs * cos + jnp.where(first_half, up, dn) * sin_signed


def _inproj_even_kernel(x_ref, sc_ref, sh_ref, g_ref, w_ref, qn_ref, kn_ref,
                        ca_ref, sa_ref, cb_ref, sb_ref, p4_ref, p16_ref,
                        qa_ref, ka_ref, va_ref, gate_ref, *b_refs):
    tm = x_ref.shape[0]
    h = _modulated_norm(x_ref[...], g_ref[...], sc_ref[...], sh_ref[...])
    lane = _lane_iota(tm)
    half_a = (lane & 16) == 0
    half_b = (lane & 8) == 0
    lo = lane < HEAD_DIM
    ca, sa, cb, sb = ca_ref[...], sa_ref[...], cb_ref[...], sb_ref[...]
    full = jnp.dot(h, w_ref[...], preferred_element_type=F32)

    def proj(a, b):
        return full[:, a:b]

    def emit_b(vals):
        for n, val in enumerate(vals):
            b_refs[3 * n][...] = val
        cat = jnp.concatenate(vals, axis=1)
        for d, perm_ref in enumerate((p4_ref, p16_ref)):
            for t in range(tm // T_RES):
                pv = jnp.dot(perm_ref[...], cat[t * T_RES:(t + 1) * T_RES], preferred_element_type=F32)
                for n in range(3):
                    o_ref = b_refs[3 * n + 1 + d]
                    o_ref[t] = pv[:, 512 * n:512 * (n + 1)].astype(BF16).reshape(o_ref.shape[1:])

    p = proj(0, 512)
    lane_half = lane // HEAD_DIM
    for j in range(4):
        xs = _head_norm(p[:, LANES * j:LANES * (j + 1)], qn_ref[...], lo)
        xs = _rope(xs, ca, sa, half_a, 16) * QK_SCALE
        swapped = pltpu.roll(xs, HEAD_DIM, 1)
        for half in range(2):
            hd = 2 * j + half
            g, hh = hd // 4, hd % 4
            both = jnp.where(lane_half == half, xs, swapped)
            ht = jnp.where(lane_half == g, both, 0.0).T.astype(BF16)
            for t in range(tm // TQ_A):
                qa_ref[g, t, :, hh * TQ_A:(hh + 1) * TQ_A] = ht[:, t * TQ_A:(t + 1) * TQ_A]
    xs = _head_norm(proj(512, 640), kn_ref[...], lo)
    ka_ref[...] = _rope(xs, ca, sa, half_a, 16).astype(BF16)
    v = proj(640, 768)
    va_ref[0] = jnp.where(lo, v, 1.0).T.astype(BF16)
    va_ref[1] = jnp.where(lo, 1.0, v).T.astype(BF16)
    gate_ref[:, 0:512] = _silu(proj(768, 1280)).astype(BF16)
    p = proj(1280, 1792)
    qb = [(_rope(p[:, LANES * j:LANES * (j + 1)], cb, sb, half_b, 8) * QK_SCALE).astype(BF16)
          for j in range(4)]
    p = proj(1792, 2304)
    kb = [_rope(p[:, LANES * j:LANES * (j + 1)], cb, sb, half_b, 8).astype(BF16) for j in range(4)]
    emit_b([jnp.concatenate(qb, axis=1), jnp.concatenate(kb, axis=1), proj(2304, 2816).astype(BF16)])
    gate_ref[:, 512:1024] = _silu(proj(2816, 3328)).astype(BF16)


def _tok_spec(tm, width):
    return pl.BlockSpec((None, tm, width), lambda b, i: (b, i, 0))


def _res_spec(tm, dil, width):
    return pl.BlockSpec((None, tm // T_RES, dil, T_RES // dil, width), lambda b, i: (b, i, 0, 0, 0))


def _bcast_spec(width):
    return pl.BlockSpec((None, 1, width), lambda b, i: (b, 0, 0))


def _const_spec(shape):
    return pl.BlockSpec(shape, lambda b, i: (0,) * len(shape))


def _residue_perm(tm, dil):
    r = np.arange(tm)
    src = (r % (tm // dil)) * dil + r // (tm // dil)
    return np.asarray(r[None, :] == src[:, None], np.float32)


def _inproj_even(x, scale, shift, pre_g, w_in, qn, kn, tabs):
    bsz, s, d = x.shape
    tm = TM_PROJ
    ca, sa, cb, sb = tabs
    tab_spec = pl.BlockSpec((tm, LANES), lambda b, i: (i, 0))
    b_specs, b_shapes = [], []
    for _ in range(3):
        b_specs.append(_tok_spec(tm, 512))
        b_shapes.append(jax.ShapeDtypeStruct((bsz, s, 512), BF16))
        for dil in B_DILATIONS[1:]:
            b_specs.append(_res_spec(tm, dil, 512))
            b_shapes.append(jax.ShapeDtypeStruct((bsz, s // T_RES, dil, T_RES // dil, 512), BF16))
    outs = pl.pallas_call(
        _inproj_even_kernel,
        grid=(bsz, s // tm),
        in_specs=[
            _tok_spec(tm, d), _bcast_spec(d), _bcast_spec(d), _const_spec((1, d)),
            _const_spec((d, AB_IN)), _const_spec((1, LANES)), _const_spec((1, LANES)),
            tab_spec, tab_spec, tab_spec, tab_spec,
            _const_spec((T_RES, T_RES)), _const_spec((T_RES, T_RES)),
        ],
        out_specs=[pl.BlockSpec((None, 2, tm // TQ_A, LANES, 4 * TQ_A), lambda b, i: (b, 0, i, 0, 0)),
                   _tok_spec(tm, LANES),
                   pl.BlockSpec((None, 2, LANES, tm), lambda b, i: (b, 0, 0, i)),
                   _tok_spec(tm, 1024)] + b_specs,
        out_shape=[jax.ShapeDtypeStruct((bsz, 2, s // TQ_A, LANES, 4 * TQ_A), BF16),
                   jax.ShapeDtypeStruct((bsz, s, LANES), BF16),
                   jax.ShapeDtypeStruct((bsz, 2, LANES, s), BF16),
                   jax.ShapeDtypeStruct((bsz, s, 1024), BF16)] + b_shapes,
        compiler_params=_params("parallel", "parallel"),
        name="inproj_even",
    )(x, scale, shift, pre_g, w_in, qn, kn, ca, sa, cb, sb,
      jnp.asarray(_residue_perm(T_RES, 4), BF16), jnp.asarray(_residue_perm(T_RES, 16), BF16))
    qa, ka, va, gates = outs[:4]
    qkv_b = [outs[4 + 3 * n:7 + 3 * n] for n in range(3)]
    return qa, ka, va, gates, qkv_b


def _inproj_odd_kernel(x_ref, sc_ref, sh_ref, g_ref, w_ref, wc_ref,
                       q_ref, k_ref, v_ref, gate_ref, vd_ref):
    h = _modulated_norm(x_ref[...], g_ref[...], sc_ref[...], sh_ref[...])
    full = jnp.dot(h, w_ref[...], preferred_element_type=F32)

    def proj(a, b):
        return full[:, a:b]

    q_ref[...] = (proj(0, 768) * QK_SCALE).astype(BF16)
    k_ref[...] = proj(768, 1536).astype(BF16)
    v_ref[...] = proj(1536, 2304).astype(BF16)
    gate_ref[:, 0:768] = _silu(proj(2304, 3072)).astype(BF16)
    u = proj(3072, 3328).astype(BF16)
    vc = jnp.dot(u, wc_ref[...], preferred_element_type=F32)
    vd_ref[0] = vc[:, 0:D_WIDTH].astype(BF16)
    vd_ref[1] = vc[:, D_WIDTH:2 * D_WIDTH].astype(BF16)
    gate_ref[:, 768:1024] = _silu(proj(3328, 3584)).astype(BF16)


def _inproj_odd(x, scale, shift, pre_g, w_in, wc):
    bsz, s, d = x.shape
    tm = TM_PROJ
    widths = (768, 768, 768, 1024)
    return pl.pallas_call(
        _inproj_odd_kernel,
        grid=(bsz, s // tm),
        in_specs=[
            _tok_spec(tm, d), _bcast_spec(d), _bcast_spec(d), _const_spec((1, d)),
            _const_spec((d, CD_IN)), _const_spec((D_WIDTH, 2 * D_WIDTH)),
        ],
        out_specs=[_tok_spec(tm, w) for w in widths]
        + [pl.BlockSpec((None, 2, tm, D_WIDTH), lambda b, i: (b, 0, i, 0))],
        out_shape=[jax.ShapeDtypeStruct((bsz, s, w), BF16) for w in widths]
        + [jax.ShapeDtypeStruct((bsz, 2, s, D_WIDTH), BF16)],
        compiler_params=_params("parallel", "parallel"),
        name="inproj_odd",
    )(x, scale, shift, pre_g, w_in, wc)


def _attn_a_kernel(q_ref, k_ref, v_ref, o_ref, s_ref, *, tk):
    ntile = q_ref.shape[0]
    tq = o_ref.shape[0] // ntile
    nchunk = k_ref.shape[0] // tk
    first = pl.program_id(1) == 0
    items = [(t, j) for t in range(ntile) for j in range(nchunk)]

    def scores(n):
        t, j = items[n]
        s_ref[n % 2] = jnp.dot(k_ref[j * tk:(j + 1) * tk, :], q_ref[t], preferred_element_type=F32)

    scores(0)
    for n, (t, j) in enumerate(items):
        if n + 1 < len(items):
            scores(n + 1)
        if j == 0:
            m = jnp.full((1, 4 * tq), NEG, F32)
            acc = jnp.zeros((LANES, 4 * tq), F32)
        s = s_ref[n % 2]
        vc = v_ref[:, j * tk:(j + 1) * tk]
        m_new = jnp.maximum(m, jnp.max(s, axis=0, keepdims=True))
        p = jnp.exp2(s - m_new).astype(BF16)
        acc = jnp.exp2(m - m_new) * acc + jnp.dot(vc, p, preferred_element_type=F32)
        m = m_new
        if j == nchunk - 1:
            num = jnp.where(first, acc[0:HEAD_DIM], acc[HEAD_DIM:LANES])
            den = jnp.where(first, acc[HEAD_DIM:HEAD_DIM + 1], acc[0:1])
            o = num / den
            for sl in range(2):
                pair = jnp.concatenate([o[:, (2 * sl) * tq:(2 * sl + 1) * tq],
                                        o[:, (2 * sl + 1) * tq:(2 * sl + 2) * tq]], axis=0)
                o_ref[t * tq:(t + 1) * tq, LANES * sl:LANES * (sl + 1)] = pair.T.astype(BF16)


def _attn_a(q, k, v):
    bsz, s, _ = k.shape
    tq = TQ_A
    tk = min(TK_A, s)
    ntile = max(1, min(ITEMS_A * tk // s, s // tq))
    assert s % (ntile * tq) == 0 and s % tk == 0
    return pl.pallas_call(
        functools.partial(_attn_a_kernel, tk=tk),
        grid=(bsz, 2, s // (ntile * tq)),
        in_specs=[
            pl.BlockSpec((None, None, ntile, LANES, 4 * tq), lambda b, g, i: (b, g, i, 0, 0)),
            pl.BlockSpec((None, s, LANES), lambda b, g, i: (b, 0, 0)),
            pl.BlockSpec((None, None, LANES, s), lambda b, g, i: (b, g, 0, 0)),
        ],
        out_specs=pl.BlockSpec((None, ntile * tq, 2 * LANES), lambda b, g, i: (b, i, g)),
        out_shape=jax.ShapeDtypeStruct((bsz, s, 512), BF16),
        scratch_shapes=[pltpu.VMEM((2, tk, 4 * tq), F32)],
        compiler_params=_params("parallel", "arbitrary", "arbitrary"),
        name="attn_a",
    )(q, k, v)


def _window(refs, start, size, cols, lq):
    rpt = refs[0].shape[1]
    pieces = []
    for n, r in enumerate(refs):
        for t in range(r.shape[0]):
            t0 = n * lq + t * rpt
            a, b = max(start, t0), min(start + size, t0 + rpt)
            if a < b:
                pieces.append(r[t, a - t0:b - t0, cols])
    return pieces[0] if len(pieces) == 1 else jnp.concatenate(pieces, axis=0)


def _store_rows(ref, r0, val, cols):
    rpt = ref.shape[1]
    n = val.shape[0]
    for t in range(ref.shape[0]):
        a, b = max(r0, t * rpt), min(r0 + n, (t + 1) * rpt)
        if a < b:
            ref[t, a - t * rpt:b - t * rpt, cols] = val[a - r0:b - r0]


def _attn_b_kernel(q_ref, kp_ref, kc_ref, kn_ref, vp_ref, vc_ref, vn_ref, o_ref, lse_ref, s_ref, *, l_len):
    lq = q_ref.shape[0] * q_ref.shape[1]
    sub = min(SUB_B, lq)
    base = pl.program_id(2) * lq
    lane = _lane_iota(sub)
    lo = lane < HEAD_DIM
    rq = lax.broadcasted_iota(jnp.int32, (sub, 2 * sub), 0)
    ck = lax.broadcasted_iota(jnp.int32, (sub, 2 * sub), 1)
    delta = ck - rq - sub // 2
    band = jnp.abs(delta) <= B_RADIUS
    items = [(sb, j) for sb in range(lq // sub) for j in range(4)]

    def scores(n):
        sb, j = items[n]
        r0 = sb * sub
        cols = slice(LANES * j, LANES * (j + 1))
        qs = _window((q_ref,), r0, sub, cols, lq).astype(F32)
        kw = _window((kp_ref, kc_ref, kn_ref), lq + r0 - sub // 2, 2 * sub, cols, lq)
        q2 = jnp.concatenate([jnp.where(lo, qs, 0.0), jnp.where(lo, 0.0, qs)], axis=0).astype(BF16)
        s_ref[n % 2] = lax.dot_general(q2, kw, (((1,), (1,)), ((), ())), preferred_element_type=F32)

    scores(0)
    for n, (sb, j) in enumerate(items):
        if n + 1 < len(items):
            scores(n + 1)
        r0 = sb * sub
        cols = slice(LANES * j, LANES * (j + 1))
        if j == 0:
            kpos = jnp.where(band, base + (r0 - sub // 2) + ck, -1)
            mask = jnp.where(jnp.logical_and(kpos >= 0, kpos < l_len), 0.0, NEG)
            mask2 = jnp.concatenate([mask, mask], axis=0)
            lse_tile = jnp.zeros((sub, LANES), F32)
        vw = _window((vp_ref, vc_ref, vn_ref), lq + r0 - sub // 2, 2 * sub, cols, lq)
        s = s_ref[n % 2] + mask2
        m = jnp.max(s, axis=1, keepdims=True)
        p = jnp.exp2(s - m)
        l = jnp.sum(p, axis=1, keepdims=True)
        o = jnp.dot(p.astype(BF16), vw, preferred_element_type=F32) * (1.0 / l)
        lse = m + jnp.log2(l)
        lse_tile = (lse_tile + jnp.where(lane == 2 * j, lse[0:sub], 0.0)
                    + jnp.where(lane == 2 * j + 1, lse[sub:2 * sub], 0.0))
        _store_rows(o_ref, r0, jnp.where(lo, o[0:sub], o[sub:2 * sub]).astype(BF16), cols)
        if j == 3:
            packed = jnp.zeros((sub, LANES), F32)
            rest = lse_tile
            for part in range(LSE_PARTS):
                term = rest.astype(BF16).astype(F32)
                rest = rest - term
                packed = packed + (pltpu.roll(term, 8 * part, 1) if part else term)
            _store_rows(lse_ref, r0, packed.astype(BF16), slice(None))


def _attn_b(q, k, v, dil):
    bsz, ntile, _, rpt, w = q.shape
    l_len = ntile * rpt
    lq = min(LQ_B, l_len)
    nt = lq // rpt
    nblk = l_len // lq

    def spec(width, shift):
        def index(b, r, i):
            return (b, jnp.clip(i + shift, 0, nblk - 1), r, 0, 0)
        return pl.BlockSpec((None, nt, None, rpt, width), index)

    cur, prv, nxt = spec(w, 0), spec(w, -1), spec(w, 1)
    return pl.pallas_call(
        functools.partial(_attn_b_kernel, l_len=l_len),
        grid=(bsz, dil, nblk),
        in_specs=[cur, prv, cur, nxt, prv, cur, nxt],
        out_specs=[cur, spec(LANES, 0)],
        out_shape=[jax.ShapeDtypeStruct(q.shape, BF16),
                   jax.ShapeDtypeStruct((bsz, ntile, dil, rpt, LANES), BF16)],
        scratch_shapes=[pltpu.VMEM((2, 2 * min(SUB_B, lq), 2 * min(SUB_B, lq)), F32)],
        compiler_params=_params("parallel", "parallel", "parallel"),
        name=f"attn_b_d{dil}",
    )(q, k, k, k, v, v, v)


def _attn_c_kernel(q_ref, kp_ref, kc_ref, kn_ref, vp_ref, vc_ref, vn_ref, bias0_ref, bias1_ref, o_ref, s_ref):
    tq = q_ref.shape[0] // 2
    lane = _lane_iota(tq)
    lo = lane < HEAD_DIM
    items = [(qb, pr) for qb in range(2) for pr in range(C_HEADS // 2)]

    def window(side_refs, mid_ref, qb, cols):
        first = side_refs[0][:, cols] if qb == 0 else mid_ref[0:tq, cols]
        last = mid_ref[tq:2 * tq, cols] if qb == 0 else side_refs[1][:, cols]
        mid = mid_ref[tq:2 * tq, cols] if qb == 1 else mid_ref[0:tq, cols]
        return jnp.concatenate([first, mid, last], axis=0)

    def scores(n):
        qb, pr = items[n]
        cols = slice(LANES * pr, LANES * (pr + 1))
        qs = q_ref[qb * tq:(qb + 1) * tq, cols].astype(F32)
        k3 = window((kp_ref, kn_ref), kc_ref, qb, cols)
        q2 = jnp.concatenate([jnp.where(lo, qs, 0.0), jnp.where(lo, 0.0, qs)], axis=0).astype(BF16)
        s_ref[n % 2] = lax.dot_general(q2, k3, (((1,), (1,)), ((), ())), preferred_element_type=F32)

    scores(0)
    for n, (qb, pr) in enumerate(items):
        if n + 1 < len(items):
            scores(n + 1)
        cols = slice(LANES * pr, LANES * (pr + 1))
        v3 = window((vp_ref, vn_ref), vc_ref, qb, cols)
        bias_ref = bias0_ref if qb == 0 else bias1_ref
        s = s_ref[n % 2] + bias_ref[2 * pr:2 * pr + 2].reshape(2 * tq, 3 * tq)
        m = jnp.max(s, axis=1, keepdims=True)
        p = jnp.exp2(s - m)
        l = jnp.sum(p, axis=1, keepdims=True)
        o = jnp.dot(p.astype(BF16), v3, preferred_element_type=F32) * (1.0 / l)
        o_ref[qb * tq:(qb + 1) * tq, cols] = jnp.where(lo, o[0:tq], o[tq:2 * tq]).astype(BF16)


def _bias_c(rpb):
    nq, nu = ROWS_C, 3 * ROWS_C
    assert nq == C_WIN_H // 2
    nrel = 2 * C_WIN_H - 1
    scaled = rpb.astype(F32) * LOG2E
    per_col = []
    for c in range(GRID_W):
        c0 = min(max(c - C_WIN_W // 2, 0), GRID_W - C_WIN_W)
        win = scaled[:, :, c0 - c + C_WIN_W - 1:c0 - c + 2 * C_WIN_W - 1]
        per_col.append(jnp.pad(win, ((0, 0), (0, 0), (c0, GRID_W - C_WIN_W - c0)), constant_values=NEG))
    tab = jnp.stack(per_col, axis=2)
    dead = jnp.full((C_HEADS, GRID_W, GRID_W), NEG, F32)
    kinds = []
    for kind in range(3):
        rows = []
        for j in range(nq):
            slots = []
            for u in range(nu):
                rel = (u - nq) - j + C_WIN_H - 1
                if kind == 0:
                    ok = nq <= u < nq + C_WIN_H
                elif kind == 2:
                    ok = 2 * nq - C_WIN_H <= u < 2 * nq
                else:
                    ok = nq - C_WIN_H // 2 <= u - j < nq + C_WIN_H // 2
                slots.append(tab[:, rel] if ok and 0 <= rel < nrel else dead)
            rows.append(jnp.concatenate(slots, axis=2))
        kinds.append(jnp.concatenate(rows, axis=1))
    return jnp.stack(kinds, axis=0)


def _attn_c(q, k, v, bias):
    bsz, s, w = q.shape
    tq = ROWS_C * GRID_W
    nblk = s // tq
    assert nblk >= 2 and nblk % 2 == 0
    npair = nblk // 2
    cur = pl.BlockSpec((None, 2 * tq, w), lambda b, i: (b, i, 0))
    prv = pl.BlockSpec((None, tq, w), lambda b, i: (b, jnp.maximum(2 * i - 1, 0), 0))
    nxt = pl.BlockSpec((None, tq, w), lambda b, i: (b, jnp.minimum(2 * i + 2, nblk - 1), 0))
    bias_shape = (None, C_HEADS, tq, 3 * tq)
    bias0 = pl.BlockSpec(bias_shape, lambda b, i: (jnp.where(i == 0, 0, 1), 0, 0, 0),
                         pipeline_mode=pl.Buffered(1))
    bias1 = pl.BlockSpec(bias_shape, lambda b, i: (jnp.where(i == npair - 1, 2, 1), 0, 0, 0),
                         pipeline_mode=pl.Buffered(1))
    return pl.pallas_call(
        _attn_c_kernel,
        grid=(bsz, npair),
        in_specs=[cur, prv, cur, nxt, prv, cur, nxt, bias0, bias1],
        out_specs=cur,
        out_shape=jax.ShapeDtypeStruct((bsz, s, w), BF16),
        scratch_shapes=[pltpu.VMEM((2, 2 * tq, 3 * tq), F32)],
        compiler_params=_params("parallel", "arbitrary"),
        name="attn_c",
    )(q, k, k, k, v, v, v, bias, bias)


S2_D = 64


def _dft_consts(s):
    s1 = s // S2_D
    th2 = 2 * np.pi * np.outer(np.arange(S2_D), np.arange(S2_D)) / S2_D
    c2, n2 = np.cos(th2) / 8.0, np.sin(th2) / 8.0
    w2big = np.block([[c2, n2], [-n2, c2]])
    th1 = 2 * np.pi * np.outer(np.arange(s1), np.arange(s1)) / s1
    w1cat = np.concatenate([np.cos(th1), np.sin(th1)], axis=1) / math.sqrt(s1)
    tht = 2 * np.pi * np.outer(np.arange(S2_D), np.arange(s1)) / s
    return (jnp.asarray(w2big, BF16), jnp.asarray(w1cat, BF16),
            jnp.asarray(np.cos(tht), F32), jnp.asarray(-np.sin(tht), F32))


def _chan_dft_const():
    th = 2 * np.pi * np.outer(np.arange(D_GROUP_DIM), np.arange(D_GROUP_DIM)) / D_GROUP_DIM
    eye = np.eye(D_GROUPS)
    wc = np.concatenate([np.kron(eye, np.cos(th)), np.kron(eye, -np.sin(th))], axis=1) / 8.0
    return jnp.asarray(wc, BF16)


def _dft1_kernel(v_ref, w_ref, tr_ref, ti_ref, o_ref):
    x = jnp.concatenate([v_ref[0], v_ref[1]], axis=0)
    b = jnp.dot(w_ref[...], x, preferred_element_type=F32)
    br, bi = b[0:S2_D], b[S2_D:2 * S2_D]
    tr, ti = tr_ref[...], ti_ref[...]
    o_ref[0] = (br * tr - bi * ti).astype(BF16)
    o_ref[1] = (br * ti + bi * tr).astype(BF16)


def _dft2_kernel(b_ref, w_ref, o_ref):
    for kk in range(b_ref.shape[1]):
        rhs = jnp.concatenate([b_ref[0, kk], b_ref[1, kk]], axis=0)
        x = jnp.dot(w_ref[...], rhs, preferred_element_type=F32)
        o_ref[:, D_WIDTH * kk:D_WIDTH * (kk + 1)] = x.astype(BF16)


def _fourier(vd, consts):
    bsz, _, s, w = vd.shape
    s1 = s // S2_D
    w2big, w1cat, tr, ti = consts
    ncol = s1 * w
    tn = min(TN_D1, ncol)
    twr = jnp.repeat(tr, w, axis=1)
    twi = jnp.repeat(ti, w, axis=1)
    b1 = pl.pallas_call(
        _dft1_kernel,
        grid=(ncol // tn, bsz),
        in_specs=[
            pl.BlockSpec((None, 2, S2_D, tn), lambda j, b: (b, 0, 0, j)),
            pl.BlockSpec((2 * S2_D, 2 * S2_D), lambda j, b: (0, 0)),
            pl.BlockSpec((S2_D, tn), lambda j, b: (0, j)),
            pl.BlockSpec((S2_D, tn), lambda j, b: (0, j)),
        ],
        out_specs=pl.BlockSpec((None, 2, S2_D, tn), lambda j, b: (b, 0, 0, j)),
        out_shape=jax.ShapeDtypeStruct((bsz, 2, S2_D, ncol), BF16),
        compiler_params=_params("parallel", "parallel"),
        name="dft_stage1",
    )(vd.reshape(bsz, 2, S2_D, ncol), w2big, twr, twi)
    tk2 = TK2_D
    f = pl.pallas_call(
        _dft2_kernel,
        grid=(bsz, S2_D // tk2),
        in_specs=[
            pl.BlockSpec((None, 2, tk2, s1, w), lambda b, j: (b, 0, j, 0, 0)),
            pl.BlockSpec((s1, 2 * s1), lambda b, j: (0, 0)),
        ],
        out_specs=pl.BlockSpec((None, s1, tk2 * w), lambda b, j: (b, 0, j)),
        out_shape=jax.ShapeDtypeStruct((bsz, s1, S2_D * w), BF16),
        compiler_params=_params("parallel", "parallel"),
        name="dft_stage2",
    )(b1.reshape(bsz, 2, S2_D, s1, w), w1cat)
    return f.reshape(bsz, s, w)


def _finish(y, x_ref, gate_ref, g_ref, o_ref):
    ms = jnp.mean(y * y, axis=-1, keepdims=True)
    yn = y * lax.rsqrt(ms + EPS) * g_ref[...]
    o_ref[...] = x_ref[...] + gate_ref[...] * yn


def _outproj_even_kernel(oa_ref, o1_ref, o4_ref, o16_ref, l1_ref, l4_ref, l16_ref, gates_ref,
                         ex_ref, p4_ref, p16_ref, w_ref, x_ref, gate_ref, g_ref, o_ref):
    tm = x_ref.shape[0]
    ex = ex_ref[...]

    def natural(o_res, l_res, perm_ref):
        tiles = []
        for t in range(tm // T_RES):
            both = jnp.concatenate([o_res[t].reshape(T_RES, 512), l_res[t].reshape(T_RES, LANES)], axis=1)
            tiles.append(jnp.dot(perm_ref[...], both, preferred_element_type=F32))
        nat = jnp.concatenate(tiles, axis=0)
        return nat[:, 0:512], nat[:, 512:512 + LANES]

    def lse(x):
        out = x
        for part in range(1, LSE_PARTS):
            out = out + pltpu.roll(x, LANES - 8 * part, 1)
        return out

    o2, l2 = natural(o4_ref, l4_ref, p4_ref)
    o3, l3 = natural(o16_ref, l16_ref, p16_ref)
    l1, l2, l3 = lse(l1_ref[...].astype(F32)), lse(l2), lse(l3)
    mx = jnp.maximum(jnp.maximum(l1, l2), l3)
    e1, e2, e3 = jnp.exp2(l1 - mx), jnp.exp2(l2 - mx), jnp.exp2(l3 - mx)
    inv = 1.0 / (e1 + e2 + e3)
    terms = []
    for wgt in (e1 * inv, e2 * inv, e3 * inv):
        hi = wgt.astype(BF16)
        terms += [hi, (wgt - hi.astype(F32)).astype(BF16)]
    wide = jnp.dot(jnp.concatenate(terms, axis=0), ex, preferred_element_type=F32)
    w1, w2, w3 = (wide[2 * n * tm:(2 * n + 1) * tm] + wide[(2 * n + 1) * tm:(2 * n + 2) * tm] for n in range(3))
    ob = w1 * o1_ref[...].astype(F32) + w2 * o2 + w3 * o3
    ma = (oa_ref[...].astype(F32) * gates_ref[:, 0:512].astype(F32)).astype(BF16)
    mb = (ob * gates_ref[:, 512:1024].astype(F32)).astype(BF16)
    y = jnp.dot(jnp.concatenate([ma, mb], axis=1), w_ref[...], preferred_element_type=F32)
    _finish(y, x_ref, gate_ref, g_ref, o_ref)


def _outproj_even(oa, obs, lses, gates, ex, w_out, x, gate, post_g):
    bsz, s, d = x.shape
    tm = TM_PROJ
    return pl.pallas_call(
        _outproj_even_kernel,
        grid=(bsz, s // tm),
        in_specs=[_tok_spec(tm, 512), _tok_spec(tm, 512), _res_spec(tm, 4, 512), _res_spec(tm, 16, 512),
                  _tok_spec(tm, LANES), _res_spec(tm, 4, LANES), _res_spec(tm, 16, LANES),
                  _tok_spec(tm, 1024), _const_spec((LANES, 512)),
                  _const_spec((T_RES, T_RES)), _const_spec((T_RES, T_RES)), _const_spec((d, d)),
                  _tok_spec(tm, d), _bcast_spec(d), _const_spec((1, d))],
        out_specs=_tok_spec(tm, d),
        out_shape=jax.ShapeDtypeStruct((bsz, s, d), F32),
        compiler_params=_params("parallel", "parallel"),
        name="outproj_even",
    )(oa, *obs, *lses, gates, ex,
      jnp.asarray(_residue_perm(T_RES, 4).T, BF16), jnp.asarray(_residue_perm(T_RES, 16).T, BF16),
      w_out, x, gate, post_g)


def _outproj_odd_kernel(oc_ref, f_ref, gates_ref, lin_ref, w_ref, x_ref, gate_ref, g_ref, o_ref):
    od = jnp.dot(f_ref[...], lin_ref[...], preferred_element_type=F32)
    mc = (oc_ref[...].astype(F32) * gates_ref[:, 0:768].astype(F32)).astype(BF16)
    md = (od * gates_ref[:, 768:1024].astype(F32)).astype(BF16)
    y = jnp.dot(jnp.concatenate([mc, md], axis=1), w_ref[...], preferred_element_type=F32)
    _finish(y, x_ref, gate_ref, g_ref, o_ref)


def _outproj_odd(oc, f, gates, lin, w_out, x, gate, post_g):
    bsz, s, d = x.shape
    tm = TM_PROJ
    return pl.pallas_call(
        _outproj_odd_kernel,
        grid=(bsz, s // tm),
        in_specs=[_tok_spec(tm, 768), _tok_spec(tm, D_WIDTH), _tok_spec(tm, 1024),
                  _const_spec((D_WIDTH, D_WIDTH)), _const_spec((d, d)),
                  _tok_spec(tm, d), _bcast_spec(d), _const_spec((1, d))],
        out_specs=_tok_spec(tm, d),
        out_shape=jax.ShapeDtypeStruct((bsz, s, d), F32),
        compiler_params=_params("parallel", "parallel"),
        name="outproj_odd",
    )(oc, f, gates, lin, w_out, x, gate, post_g)


def _rope_tables(s):
    t = jnp.arange(s)

    def tab(pos, dim, theta):
        inv = theta ** (-jnp.arange(0, dim, 2, dtype=F32) / dim)
        ang = pos[:, None] * inv[None, :]
        return jnp.cos(ang), jnp.sin(ang)

    cr, sr = tab((t // GRID_W).astype(F32), HEAD_DIM // 2, A_ROPE_THETA)
    cc, sc = tab((t % GRID_W).astype(F32), HEAD_DIM // 2, A_ROPE_THETA)
    cb, sb = tab(t.astype(F32), B_ROPE_DIMS, B_ROPE_THETA)
    rest = HEAD_DIM - B_ROPE_DIMS
    ca = jnp.concatenate([cr, cr, cc, cc] * 2, axis=-1)
    sa = jnp.concatenate([-sr, sr, -sc, sc] * 2, axis=-1)
    cb = jnp.concatenate([cb, cb, jnp.ones((s, rest), F32)] * 2, axis=-1)
    sb = jnp.concatenate([-sb, sb, jnp.zeros((s, rest), F32)] * 2, axis=-1)
    return ca, sa, cb, sb


def _even_layer(x, mod, pre_g, post_g, w_in, w_out, qn, kn, tabs, ex):
    bsz, s, _ = x.shape
    shift, scale, gate = mod
    qa, ka, va, gates, qkv_b = _inproj_even(x, scale, shift, pre_g, w_in, qn, kn, tabs)
    oa = _attn_a(qa, ka, va)
    obs, lses = [], []
    for n, dil in enumerate(B_DILATIONS):
        q, k, v = (t[n] for t in qkv_b)
        if dil == 1:
            lq = min(LQ_B, s)
            q, k, v = (t.reshape(bsz, s // lq, 1, lq, t.shape[-1]) for t in (q, k, v))
        o, lse = _attn_b(q, k, v, dil)
        if dil == 1:
            o, lse = o.reshape(bsz, s, o.shape[-1]), lse.reshape(bsz, s, LANES)
        obs.append(o)
        lses.append(lse)
    return _outproj_even(oa, obs, lses, gates, ex, w_out, x, gate, post_g)


def _odd_layer(x, mod, pre_g, post_g, w_in, w_out, bias, lin, wc, dft):
    shift, scale, gate = mod
    qc, kc, vc, gates, vd = _inproj_odd(x, scale, shift, pre_g, w_in, wc)
    oc = _attn_c(qc, kc, vc, bias)
    f = _fourier(vd, dft)
    return _outproj_odd(oc, f, gates, lin, w_out, x, gate, post_g)


def _trunk(x, mods, pre_g, post_g, w_in_ab, w_out_ab, qn_a, kn_a, w_in_cd, w_out_cd, biases, lin_d):
    bsz, s, d = x.shape
    assert s % TM_PROJ == 0 and s % (16 * SUB_B) == 0
    tabs = _rope_tables(s)
    dft = _dft_consts(s)
    wc = _chan_dft_const()
    lane = np.arange(LANES)
    ex = jnp.asarray(lane[:, None] == np.arange(512)[None, :] // HEAD_DIM, BF16)
    for i in range(DEPTH):
        j = i // 2
        mod = tuple(mods[i][:, None, k * d:(k + 1) * d] for k in range(3))
        pg, qg = pre_g[i][None, :], post_g[i][None, :]
        if i % 2 == 0:
            qn = jnp.tile(qn_a[j], 2)[None, :]
            kn = jnp.tile(kn_a[j], 2)[None, :]
            x = _even_layer(x, mod, pg, qg, w_in_ab[j], w_out_ab[j], qn, kn, tabs, ex)
        else:
            x = _odd_layer(x, mod, pg, qg, w_in_cd[j], w_out_cd[j], biases[j], lin_d[j], wc, dft)
    return x


def kernel(x_prompt, x_sample, c_prompt, c_sample, pre_g, post_g, ada_w, ada_b,
           w_in_ab, w_out_ab, qn_a, kn_a, w_in_cd, w_out_cd, rpb_c, lin_d):
    nb = x_prompt.shape[0]
    mods = _adaln(jnp.concatenate([c_prompt, c_sample], axis=0), ada_w, ada_b)
    biases = [_bias_c(rpb_c[j]) for j in range(rpb_c.shape[0])]
    args = (pre_g, post_g, w_in_ab.astype(BF16), w_out_ab.astype(BF16), qn_a, kn_a,
            w_in_cd.astype(BF16), w_out_cd.astype(BF16), biases, lin_d.astype(BF16))
    y_prompt = _trunk(x_prompt, mods[:, :nb], *args)
    y_sample = _trunk(x_sample, mods[:, nb:], *args)
    return (y_prompt, y_sample)
```

```python
import functools
import math

import numpy as np
import jax
import jax.numpy as jnp
from jax import lax
from jax.experimental import pallas as pl
from jax.experimental.pallas import tpu as pltpu

F32 = jnp.float32
BF16 = jnp.bfloat16

D_MODEL = 1024
DEPTH = 4
HEAD_DIM = 64
GRID_W = 64
A_ROPE_THETA = 10000.0
B_ROPE_THETA = 500000.0
B_ROPE_DIMS = 16
B_DILATIONS = (1, 4, 16)
B_RADIUS = 64
C_HEADS = 12
C_WIN_H = 8
C_WIN_W = 16
D_GROUPS = 4
D_GROUP_DIM = 64
D_WIDTH = D_GROUPS * D_GROUP_DIM
AB_IN = 3328
CD_IN = 3584
EPS = 1e-6
NEG = -1e30
LOG2E = 1.4426950408889634
QK_SCALE = LOG2E * HEAD_DIM ** -0.5

LANES = 128
VMEM_LIMIT = 56 * 1024 * 1024

TM_PROJ = 512
T_RES = 256
TQ_A = 256
TK_A = 2048
ITEMS_A = 16
LQ_B = 512
SUB_B = 128
ROWS_C = 4
TN_D1 = 4096
TK2_D = 8
LSE_PARTS = 3


def _params(*sem):
    return pltpu.CompilerParams(dimension_semantics=sem, vmem_limit_bytes=VMEM_LIMIT)


def _silu(x):
    return x / (1.0 + jnp.exp(-x))


def _lane_iota(rows):
    return lax.broadcasted_iota(jnp.int32, (rows, LANES), 1)


def _adaln_kernel(c_ref, w_ref, b_ref, o_ref):
    a = _silu(c_ref[...]).astype(BF16)
    o_ref[...] = jnp.dot(a, w_ref[...].astype(BF16), preferred_element_type=F32) + b_ref[...]


def _adaln(c_all, ada_w, ada_b):
    bt = c_all.shape[0]
    d = D_MODEL
    return pl.pallas_call(
        _adaln_kernel,
        grid=(DEPTH, 3),
        in_specs=[
            pl.BlockSpec((bt, d), lambda l, j: (0, 0)),
            pl.BlockSpec((None, d, d), lambda l, j: (l, 0, j)),
            pl.BlockSpec((None, 1, d), lambda l, j: (l, 0, j)),
        ],
        out_specs=pl.BlockSpec((None, bt, d), lambda l, j: (l, 0, j)),
        out_shape=jax.ShapeDtypeStruct((DEPTH, bt, 3 * d), F32),
        compiler_params=_params("parallel", "parallel"),
        name="adaln",
    )(c_all, ada_w, ada_b.reshape(DEPTH, 1, 3 * d))


def _modulated_norm(x, g, scale, shift):
    ms = jnp.mean(x * x, axis=-1, keepdims=True)
    y = x * lax.rsqrt(ms + EPS) * g
    return (y * (1.0 + scale) + shift).astype(BF16)


def _head_norm(xs, gain, lo):
    x2 = xs * xs
    s_lo = jnp.sum(jnp.where(lo, x2, 0.0), axis=1, keepdims=True)
    s_hi = jnp.sum(jnp.where(lo, 0.0, x2), axis=1, keepdims=True)
    ss = jnp.where(lo, s_lo, s_hi)
    return xs * lax.rsqrt(ss * (1.0 / HEAD_DIM) + EPS) * gain


def _rope(xs, cos, sin_signed, first_half, shift):
    up = pltpu.roll(xs, LANES - shift, 1)
    dn = pltpu.roll(xs, shift, 1)
    return xs * cos + jnp.where(first_half, up, dn) * sin_signed


def _inproj_even_kernel(x_ref, sc_ref, sh_ref, g_ref, w_ref, qn_ref, kn_ref,
                        ca_ref, sa_ref, cb_ref, sb_ref, p4_ref, p16_ref,
                        qa_ref, ka_ref, va_ref, gate_ref, *b_refs):
    tm = x_ref.shape[0]
    h = _modulated_norm(x_ref[...], g_ref[...], sc_ref[...], sh_ref[...])
    lane = _lane_iota(tm)
    half_a = (lane & 16) == 0
    half_b = (lane & 8) == 0
    lo = lane < HEAD_DIM
    ca, sa, cb, sb = ca_ref[...], sa_ref[...], cb_ref[...], sb_ref[...]
    full = jnp.dot(h, w_ref[...], preferred_element_type=F32)

    def proj(a, b):
        return full[:, a:b]

    def emit_b(vals):
        for n, val in enumerate(vals):
            b_refs[3 * n][...] = val
        cat = jnp.concatenate(vals, axis=1)
        for d, perm_ref in enumerate((p4_ref, p16_ref)):
            for t in range(tm // T_RES):
                pv = jnp.dot(perm_ref[...], cat[t * T_RES:(t + 1) * T_RES], preferred_element_type=F32)
                for n in range(3):
                    o_ref = b_refs[3 * n + 1 + d]
                    o_ref[t] = pv[:, 512 * n:512 * (n + 1)].astype(BF16).reshape(o_ref.shape[1:])

    p = proj(0, 512)
    lane_half = lane // HEAD_DIM
    for j in range(4):
        xs = _head_norm(p[:, LANES * j:LANES * (j + 1)], qn_ref[...], lo)
        xs = _rope(xs, ca, sa, half_a, 16) * QK_SCALE
        swapped = pltpu.roll(xs, HEAD_DIM, 1)
        for half in range(2):
            hd = 2 * j + half
            g, hh = hd // 4, hd % 4
            both = jnp.where(lane_half == half, xs, swapped)
            ht = jnp.where(lane_half == g, both, 0.0).T.astype(BF16)
            for t in range(tm // TQ_A):
                qa_ref[g, t, :, hh * TQ_A:(hh + 1) * TQ_A] = ht[:, t * TQ_A:(t + 1) * TQ_A]
    xs = _head_norm(proj(512, 640), kn_ref[...], lo)
    ka_ref[...] = _rope(xs, ca, sa, half_a, 16).astype(BF16)
    v = proj(640, 768)
    va_ref[0] = jnp.where(lo, v, 1.0).T.astype(BF16)
    va_ref[1] = jnp.where(lo, 1.0, v).T.astype(BF16)
    gate_ref[:, 0:512] = _silu(proj(768, 1280)).astype(BF16)
    p = proj(1280, 1792)
    qb = [(_rope(p[:, LANES * j:LANES * (j + 1)], cb, sb, half_b, 8) * QK_SCALE).astype(BF16)
          for j in range(4)]
    p = proj(1792, 2304)
    kb = [_rope(p[:, LANES * j:LANES * (j + 1)], cb, sb, half_b, 8).astype(BF16) for j in range(4)]
    emit_b([jnp.concatenate(qb, axis=1), jnp.concatenate(kb, axis=1), proj(2304, 2816).astype(BF16)])
    gate_ref[:, 512:1024] = _silu(proj(2816, 3328)).astype(BF16)


def _tok_spec(tm, width):
    return pl.BlockSpec((None, tm, width), lambda b, i: (b, i, 0))


def _res_spec(tm, dil, width):
    return pl.BlockSpec((None, tm // T_RES, dil, T_RES // dil, width), lambda b, i: (b, i, 0, 0, 0))


def _bcast_spec(width):
    return pl.BlockSpec((None, 1, width), lambda b, i: (b, 0, 0))


def _const_spec(shape):
    return pl.BlockSpec(shape, lambda b, i: (0,) * len(shape))


def _residue_perm(tm, dil):
    r = np.arange(tm)
    src = (r % (tm // dil)) * dil + r // (tm // dil)
    return np.asarray(r[None, :] == src[:, None], np.float32)


def _inproj_even(x, scale, shift, pre_g, w_in, qn, kn, tabs):
    bsz, s, d = x.shape
    tm = TM_PROJ
    ca, sa, cb, sb = tabs
    tab_spec = pl.BlockSpec((tm, LANES), lambda b, i: (i, 0))
    b_specs, b_shapes = [], []
    for _ in range(3):
        b_specs.append(_tok_spec(tm, 512))
        b_shapes.append(jax.ShapeDtypeStruct((bsz, s, 512), BF16))
        for dil in B_DILATIONS[1:]:
            b_specs.append(_res_spec(tm, dil, 512))
            b_shapes.append(jax.ShapeDtypeStruct((bsz, s // T_RES, dil, T_RES // dil, 512), BF16))
    outs = pl.pallas_call(
        _inproj_even_kernel,
        grid=(bsz, s // tm),
        in_specs=[
            _tok_spec(tm, d), _bcast_spec(d), _bcast_spec(d), _const_spec((1, d)),
            _const_spec((d, AB_IN)), _const_spec((1, LANES)), _const_spec((1, LANES)),
            tab_spec, tab_spec, tab_spec, tab_spec,
            _const_spec((T_RES, T_RES)), _const_spec((T_RES, T_RES)),
        ],
        out_specs=[pl.BlockSpec((None, 2, tm // TQ_A, LANES, 4 * TQ_A), lambda b, i: (b, 0, i, 0, 0)),
                   _tok_spec(tm, LANES),
                   pl.BlockSpec((None, 2, LANES, tm), lambda b, i: (b, 0, 0, i)),
                   _tok_spec(tm, 1024)] + b_specs,
        out_shape=[jax.ShapeDtypeStruct((bsz, 2, s // TQ_A, LANES, 4 * TQ_A), BF16),
                   jax.ShapeDtypeStruct((bsz, s, LANES), BF16),
                   jax.ShapeDtypeStruct((bsz, 2, LANES, s), BF16),
                   jax.ShapeDtypeStruct((bsz, s, 1024), BF16)] + b_shapes,
        compiler_params=_params("parallel", "parallel"),
        name="inproj_even",
    )(x, scale, shift, pre_g, w_in, qn, kn, ca, sa, cb, sb,
      jnp.asarray(_residue_perm(T_RES, 4), BF16), jnp.asarray(_residue_perm(T_RES, 16), BF16))
    qa, ka, va, gates = outs[:4]
    qkv_b = [outs[4 + 3 * n:7 + 3 * n] for n in range(3)]
    return qa, ka, va, gates, qkv_b


def _inproj_odd_kernel(x_ref, sc_ref, sh_ref, g_ref, w_ref, wc_ref,
                       q_ref, k_ref, v_ref, gate_ref, vd_ref):
    h = _modulated_norm(x_ref[...], g_ref[...], sc_ref[...], sh_ref[...])
    full = jnp.dot(h, w_ref[...], preferred_element_type=F32)

    def proj(a, b):
        return full[:, a:b]

    q_ref[...] = (proj(0, 768) * QK_SCALE).astype(BF16)
    k_ref[...] = proj(768, 1536).astype(BF16)
    v_ref[...] = proj(1536, 2304).astype(BF16)
    gate_ref[:, 0:768] = _silu(proj(2304, 3072)).astype(BF16)
    u = proj(3072, 3328).astype(BF16)
    vc = jnp.dot(u, wc_ref[...], preferred_element_type=F32)
    vd_ref[0] = vc[:, 0:D_WIDTH].astype(BF16)
    vd_ref[1] = vc[:, D_WIDTH:2 * D_WIDTH].astype(BF16)
    gate_ref[:, 768:1024] = _silu(proj(3328, 3584)).astype(BF16)


def _inproj_odd(x, scale, shift, pre_g, w_in, wc):
    bsz, s, d = x.shape
    tm = TM_PROJ
    widths = (768, 768, 768, 1024)
    return pl.pallas_call(
        _inproj_odd_kernel,
        grid=(bsz, s // tm),
        in_specs=[
            _tok_spec(tm, d), _bcast_spec(d), _bcast_spec(d), _const_spec((1, d)),
            _const_spec((d, CD_IN)), _const_spec((D_WIDTH, 2 * D_WIDTH)),
        ],
        out_specs=[_tok_spec(tm, w) for w in widths]
        + [pl.BlockSpec((None, 2, tm, D_WIDTH), lambda b, i: (b, 0, i, 0))],
        out_shape=[jax.ShapeDtypeStruct((bsz, s, w), BF16) for w in widths]
        + [jax.ShapeDtypeStruct((bsz, 2, s, D_WIDTH), BF16)],
        compiler_params=_params("parallel", "parallel"),
        name="inproj_odd",
    )(x, scale, shift, pre_g, w_in, wc)


def _attn_a_kernel(q_ref, k_ref, v_ref, o_ref, s_ref, *, tk):
    ntile = q_ref.shape[0]
    tq = o_ref.shape[0] // ntile
    nchunk = k_ref.shape[0] // tk
    first = pl.program_id(1) == 0
    items = [(t, j) for t in range(ntile) for j in range(nchunk)]

    def scores(n):
        t, j = items[n]
        s_ref[n % 2] = jnp.dot(k_ref[j * tk:(j + 1) * tk, :], q_ref[t], preferred_element_type=F32)

    scores(0)
    for n, (t, j) in enumerate(items):
        if n + 1 < len(items):
            scores(n + 1)
        if j == 0:
            m = jnp.full((1, 4 * tq), NEG, F32)
            acc = jnp.zeros((LANES, 4 * tq), F32)
        s = s_ref[n % 2]
        vc = v_ref[:, j * tk:(j + 1) * tk]
        m_new = jnp.maximum(m, jnp.max(s, axis=0, keepdims=True))
        p = jnp.exp2(s - m_new).astype(BF16)
        acc = jnp.exp2(m - m_new) * acc + jnp.dot(vc, p, preferred_element_type=F32)
        m = m_new
        if j == nchunk - 1:
            num = jnp.where(first, acc[0:HEAD_DIM], acc[HEAD_DIM:LANES])
            den = jnp.where(first, acc[HEAD_DIM:HEAD_DIM + 1], acc[0:1])
            o = num / den
            for sl in range(2):
                pair = jnp.concatenate([o[:, (2 * sl) * tq:(2 * sl + 1) * tq],
                                        o[:, (2 * sl + 1) * tq:(2 * sl + 2) * tq]], axis=0)
                o_ref[t * tq:(t + 1) * tq, LANES * sl:LANES * (sl + 1)] = pair.T.astype(BF16)


def _attn_a(q, k, v):
    bsz, s, _ = k.shape
    tq = TQ_A
    tk = min(TK_A, s)
    ntile = max(1, min(ITEMS_A * tk // s, s // tq))
    assert s % (ntile * tq) == 0 and s % tk == 0
    return pl.pallas_call(
        functools.partial(_attn_a_kernel, tk=tk),
        grid=(bsz, 2, s // (ntile * tq)),
        in_specs=[
            pl.BlockSpec((None, None, ntile, LANES, 4 * tq), lambda b, g, i: (b, g, i, 0, 0)),
            pl.BlockSpec((None, s, LANES), lambda b, g, i: (b, 0, 0)),
            pl.BlockSpec((None, None, LANES, s), lambda b, g, i: (b, g, 0, 0)),
        ],
        out_specs=pl.BlockSpec((None, ntile * tq, 2 * LANES), lambda b, g, i: (b, i, g)),
        out_shape=jax.ShapeDtypeStruct((bsz, s, 512), BF16),
        scratch_shapes=[pltpu.VMEM((2, tk, 4 * tq), F32)],
        compiler_params=_params("parallel", "arbitrary", "arbitrary"),
        name="attn_a",
    )(q, k, v)


def _window(refs, start, size, cols, lq):
    rpt = refs[0].shape[1]
    pieces = []
    for n, r in enumerate(refs):
        for t in range(r.shape[0]):
            t0 = n * lq + t * rpt
            a, b = max(start, t0), min(start + size, t0 + rpt)
            if a < b:
                pieces.append(r[t, a - t0:b - t0, cols])
    return pieces[0] if len(pieces) == 1 else jnp.concatenate(pieces, axis=0)


def _store_rows(ref, r0, val, cols):
    rpt = ref.shape[1]
    n = val.shape[0]
    for t in range(ref.shape[0]):
        a, b = max(r0, t * rpt), min(r0 + n, (t + 1) * rpt)
        if a < b:
            ref[t, a - t * rpt:b - t * rpt, cols] = val[a - r0:b - r0]


def _attn_b_kernel(q_ref, kp_ref, kc_ref, kn_ref, vp_ref, vc_ref, vn_ref, o_ref, lse_ref, s_ref, *, l_len):
    lq = q_ref.shape[0] * q_ref.shape[1]
    sub = min(SUB_B, lq)
    base = pl.program_id(2) * lq
    lane = _lane_iota(sub)
    lo = lane < HEAD_DIM
    rq = lax.broadcasted_iota(jnp.int32, (sub, 2 * sub), 0)
    ck = lax.broadcasted_iota(jnp.int32, (sub, 2 * sub), 1)
    delta = ck - rq - sub // 2
    band = jnp.abs(delta) <= B_RADIUS
    items = [(sb, j) for sb in range(lq // sub) for j in range(4)]

    def scores(n):
        sb, j = items[n]
        r0 = sb * sub
        cols = slice(LANES * j, LANES * (j + 1))
        qs = _window((q_ref,), r0, sub, cols, lq).astype(F32)
        kw = _window((kp_ref, kc_ref, kn_ref), lq + r0 - sub // 2, 2 * sub, cols, lq)
        q2 = jnp.concatenate([jnp.where(lo, qs, 0.0), jnp.where(lo, 0.0, qs)], axis=0).astype(BF16)
        s_ref[n % 2] = lax.dot_general(q2, kw, (((1,), (1,)), ((), ())), preferred_element_type=F32)

    scores(0)
    for n, (sb, j) in enumerate(items):
        if n + 1 < len(items):
            scores(n + 1)
        r0 = sb * sub
        cols = slice(LANES * j, LANES * (j + 1))
        if j == 0:
            kpos = jnp.where(band, base + (r0 - sub // 2) + ck, -1)
            mask = jnp.where(jnp.logical_and(kpos >= 0, kpos < l_len), 0.0, NEG)
            mask2 = jnp.concatenate([mask, mask], axis=0)
            lse_tile = jnp.zeros((sub, LANES), F32)
        vw = _window((vp_ref, vc_ref, vn_ref), lq + r0 - sub // 2, 2 * sub, cols, lq)
        s = s_ref[n % 2] + mask2
        m = jnp.max(s, axis=1, keepdims=True)
        p = jnp.exp2(s - m)
        l = jnp.sum(p, axis=1, keepdims=True)
        o = jnp.dot(p.astype(BF16), vw, preferred_element_type=F32) * (1.0 / l)
        lse = m + jnp.log2(l)
        lse_tile = (lse_tile + jnp.where(lane == 2 * j, lse[0:sub], 0.0)
                    + jnp.where(lane == 2 * j + 1, lse[sub:2 * sub], 0.0))
        _store_rows(o_ref, r0, jnp.where(lo, o[0:sub], o[sub:2 * sub]).astype(BF16), cols)
        if j == 3:
            packed = jnp.zeros((sub, LANES), F32)
            rest = lse_tile
            for part in range(LSE_PARTS):
                term = rest.astype(BF16).astype(F32)
                rest = rest - term
                packed = packed + (pltpu.roll(term, 8 * part, 1) if part else term)
            _store_rows(lse_ref, r0, packed.astype(BF16), slice(None))


def _attn_b(q, k, v, dil):
    bsz, ntile, _, rpt, w = q.shape
    l_len = ntile * rpt
    lq = min(LQ_B, l_len)
    nt = lq // rpt
    nblk = l_len // lq

    def spec(width, shift):
        def index(b, r, i):
            return (b, jnp.clip(i + shift, 0, nblk - 1), r, 0, 0)
        return pl.BlockSpec((None, nt, None, rpt, width), index)

    cur, prv, nxt = spec(w, 0), spec(w, -1), spec(w, 1)
    return pl.pallas_call(
        functools.partial(_attn_b_kernel, l_len=l_len),
        grid=(bsz, dil, nblk),
        in_specs=[cur, prv, cur, nxt, prv, cur, nxt],
        out_specs=[cur, spec(LANES, 0)],
        out_shape=[jax.ShapeDtypeStruct(q.shape, BF16),
                   jax.ShapeDtypeStruct((bsz, ntile, dil, rpt, LANES), BF16)],
        scratch_shapes=[pltpu.VMEM((2, 2 * min(SUB_B, lq), 2 * min(SUB_B, lq)), F32)],
        compiler_params=_params("parallel", "parallel", "parallel"),
        name=f"attn_b_d{dil}",
    )(q, k, k, k, v, v, v)


def _attn_c_kernel(q_ref, kp_ref, kc_ref, kn_ref, vp_ref, vc_ref, vn_ref, bias_ref, o_ref, s_ref):
    tq = q_ref.shape[0]
    lane = _lane_iota(tq)
    lo = lane < HEAD_DIM
    npair = C_HEADS // 2

    def scores(pr):
        cols = slice(LANES * pr, LANES * (pr + 1))
        qs = q_ref[:, cols].astype(F32)
        k3 = jnp.concatenate([kp_ref[:, cols], kc_ref[:, cols], kn_ref[:, cols]], axis=0)
        q2 = jnp.concatenate([jnp.where(lo, qs, 0.0), jnp.where(lo, 0.0, qs)], axis=0).astype(BF16)
        s_ref[pr % 2] = lax.dot_general(q2, k3, (((1,), (1,)), ((), ())), preferred_element_type=F32)

    scores(0)
    for pr in range(npair):
        if pr + 1 < npair:
            scores(pr + 1)
        cols = slice(LANES * pr, LANES * (pr + 1))
        v3 = jnp.concatenate([vp_ref[:, cols], vc_ref[:, cols], vn_ref[:, cols]], axis=0)
        s = s_ref[pr % 2] + bias_ref[2 * pr:2 * pr + 2].reshape(2 * tq, 3 * tq)
        m = jnp.max(s, axis=1, keepdims=True)
        p = jnp.exp2(s - m)
        l = jnp.sum(p, axis=1, keepdims=True)
        o = jnp.dot(p.astype(BF16), v3, preferred_element_type=F32) * (1.0 / l)
        o_ref[:, cols] = jnp.where(lo, o[0:tq], o[tq:2 * tq]).astype(BF16)


def _bias_c(rpb):
    nq, nu = ROWS_C, 3 * ROWS_C
    assert nq == C_WIN_H // 2
    nrel = 2 * C_WIN_H - 1
    scaled = rpb.astype(F32) * LOG2E
    per_col = []
    for c in range(GRID_W):
        c0 = min(max(c - C_WIN_W // 2, 0), GRID_W - C_WIN_W)
        win = scaled[:, :, c0 - c + C_WIN_W - 1:c0 - c + 2 * C_WIN_W - 1]
        per_col.append(jnp.pad(win, ((0, 0), (0, 0), (c0, GRID_W - C_WIN_W - c0)), constant_values=NEG))
    tab = jnp.stack(per_col, axis=2)
    dead = jnp.full((C_HEADS, GRID_W, GRID_W), NEG, F32)
    kinds = []
    for kind in range(3):
        rows = []
        for j in range(nq):
            slots = []
            for u in range(nu):
                rel = (u - nq) - j + C_WIN_H - 1
                if kind == 0:
                    ok = nq <= u < nq + C_WIN_H
                elif kind == 2:
                    ok = 2 * nq - C_WIN_H <= u < 2 * nq
                else:
                    ok = nq - C_WIN_H // 2 <= u - j < nq + C_WIN_H // 2
                slots.append(tab[:, rel] if ok and 0 <= rel < nrel else dead)
            rows.append(jnp.concatenate(slots, axis=2))
        kinds.append(jnp.concatenate(rows, axis=1))
    return jnp.stack(kinds, axis=0)


def _attn_c(q, k, v, bias):
    bsz, s, w = q.shape
    tq = ROWS_C * GRID_W
    nblk = s // tq
    assert nblk >= 2
    kind = lambda i: jnp.where(i == 0, 0, jnp.where(i == nblk - 1, 2, 1))
    cur = pl.BlockSpec((None, tq, w), lambda b, i: (b, i, 0))
    prv = pl.BlockSpec((None, tq, w), lambda b, i: (b, jnp.maximum(i - 1, 0), 0))
    nxt = pl.BlockSpec((None, tq, w), lambda b, i: (b, jnp.minimum(i + 1, nblk - 1), 0))
    return pl.pallas_call(
        _attn_c_kernel,
        grid=(bsz, nblk),
        in_specs=[cur, prv, cur, nxt, prv, cur, nxt,
                  pl.BlockSpec((None, C_HEADS, tq, 3 * tq), lambda b, i: (kind(i), 0, 0, 0))],
        out_specs=cur,
        out_shape=jax.ShapeDtypeStruct((bsz, s, w), BF16),
        scratch_shapes=[pltpu.VMEM((2, 2 * tq, 3 * tq), F32)],
        compiler_params=_params("parallel", "arbitrary"),
        name="attn_c",
    )(q, k, k, k, v, v, v, bias)


S2_D = 64


def _dft_consts(s):
    s1 = s // S2_D
    th2 = 2 * np.pi * np.outer(np.arange(S2_D), np.arange(S2_D)) / S2_D
    c2, n2 = np.cos(th2) / 8.0, np.sin(th2) / 8.0
    w2big = np.block([[c2, n2], [-n2, c2]])
    th1 = 2 * np.pi * np.outer(np.arange(s1), np.arange(s1)) / s1
    w1cat = np.concatenate([np.cos(th1), np.sin(th1)], axis=1) / math.sqrt(s1)
    tht = 2 * np.pi * np.outer(np.arange(S2_D), np.arange(s1)) / s
    return (jnp.asarray(w2big, BF16), jnp.asarray(w1cat, BF16),
            jnp.asarray(np.cos(tht), F32), jnp.asarray(-np.sin(tht), F32))


def _chan_dft_const():
    th = 2 * np.pi * np.outer(np.arange(D_GROUP_DIM), np.arange(D_GROUP_DIM)) / D_GROUP_DIM
    eye = np.eye(D_GROUPS)
    wc = np.concatenate([np.kron(eye, np.cos(th)), np.kron(eye, -np.sin(th))], axis=1) / 8.0
    return jnp.asarray(wc, BF16)


def _dft1_kernel(v_ref, w_ref, tr_ref, ti_ref, o_ref):
    x = jnp.concatenate([v_ref[0], v_ref[1]], axis=0)
    b = jnp.dot(w_ref[...], x, preferred_element_type=F32)
    br, bi = b[0:S2_D], b[S2_D:2 * S2_D]
    tr, ti = tr_ref[...], ti_ref[...]
    o_ref[0] = (br * tr - bi * ti).astype(BF16)
    o_ref[1] = (br * ti + bi * tr).astype(BF16)


def _dft2_kernel(b_ref, w_ref, o_ref):
    for kk in range(b_ref.shape[1]):
        rhs = jnp.concatenate([b_ref[0, kk], b_ref[1, kk]], axis=0)
        x = jnp.dot(w_ref[...], rhs, preferred_element_type=F32)
        o_ref[:, D_WIDTH * kk:D_WIDTH * (kk + 1)] = x.astype(BF16)


def _fourier(vd, consts):
    bsz, _, s, w = vd.shape
    s1 = s // S2_D
    w2big, w1cat, tr, ti = consts
    ncol = s1 * w
    tn = min(TN_D1, ncol)
    twr = jnp.repeat(tr, w, axis=1)
    twi = jnp.repeat(ti, w, axis=1)
    b1 = pl.pallas_call(
        _dft1_kernel,
        grid=(ncol // tn, bsz),
        in_specs=[
            pl.BlockSpec((None, 2, S2_D, tn), lambda j, b: (b, 0, 0, j)),
            pl.BlockSpec((2 * S2_D, 2 * S2_D), lambda j, b: (0, 0)),
            pl.BlockSpec((S2_D, tn), lambda j, b: (0, j)),
            pl.BlockSpec((S2_D, tn), lambda j, b: (0, j)),
        ],
        out_specs=pl.BlockSpec((None, 2, S2_D, tn), lambda j, b: (b, 0, 0, j)),
        out_shape=jax.ShapeDtypeStruct((bsz, 2, S2_D, ncol), BF16),
        compiler_params=_params("parallel", "parallel"),
        name="dft_stage1",
    )(vd.reshape(bsz, 2, S2_D, ncol), w2big, twr, twi)
    tk2 = TK2_D
    f = pl.pallas_call(
        _dft2_kernel,
        grid=(bsz, S2_D // tk2),
        in_specs=[
            pl.BlockSpec((None, 2, tk2, s1, w), lambda b, j: (b, 0, j, 0, 0)),
            pl.BlockSpec((s1, 2 * s1), lambda b, j: (0, 0)),
        ],
        out_specs=pl.BlockSpec((None, s1, tk2 * w), lambda b, j: (b, 0, j)),
        out_shape=jax.ShapeDtypeStruct((bsz, s1, S2_D * w), BF16),
        compiler_params=_params("parallel", "parallel"),
        name="dft_stage2",
    )(b1.reshape(bsz, 2, S2_D, s1, w), w1cat)
    return f.reshape(bsz, s, w)


def _finish(y, x_ref, gate_ref, g_ref, o_ref):
    ms = jnp.mean(y * y, axis=-1, keepdims=True)
    yn = y * lax.rsqrt(ms + EPS) * g_ref[...]
    o_ref[...] = x_ref[...] + gate_ref[...] * yn


def _outproj_even_kernel(oa_ref, o1_ref, o4_ref, o16_ref, l1_ref, l4_ref, l16_ref, gates_ref,
                         ex_ref, p4_ref, p16_ref, w_ref, x_ref, gate_ref, g_ref, o_ref):
    tm = x_ref.shape[0]
    ex = ex_ref[...]

    def natural(o_res, l_res, perm_ref):
        tiles = []
        for t in range(tm // T_RES):
            both = jnp.concatenate([o_res[t].reshape(T_RES, 512), l_res[t].reshape(T_RES, LANES)], axis=1)
            tiles.append(jnp.dot(perm_ref[...], both, preferred_element_type=F32))
        nat = jnp.concatenate(tiles, axis=0)
        return nat[:, 0:512], nat[:, 512:512 + LANES]

    def lse(x):
        out = x
        for part in range(1, LSE_PARTS):
            out = out + pltpu.roll(x, LANES - 8 * part, 1)
        return out

    o2, l2 = natural(o4_ref, l4_ref, p4_ref)
    o3, l3 = natural(o16_ref, l16_ref, p16_ref)
    l1, l2, l3 = lse(l1_ref[...].astype(F32)), lse(l2), lse(l3)
    mx = jnp.maximum(jnp.maximum(l1, l2), l3)
    e1, e2, e3 = jnp.exp2(l1 - mx), jnp.exp2(l2 - mx), jnp.exp2(l3 - mx)
    inv = 1.0 / (e1 + e2 + e3)
    terms = [(e * inv).astype(BF16) for e in (e1, e2, e3)]
    wide = jnp.dot(jnp.concatenate(terms, axis=0), ex, preferred_element_type=F32)
    w1, w2, w3 = (wide[n * tm:(n + 1) * tm] for n in range(3))
    ob = w1 * o1_ref[...].astype(F32) + w2 * o2 + w3 * o3
    ma = (oa_ref[...].astype(F32) * gates_ref[:, 0:512].astype(F32)).astype(BF16)
    mb = (ob * gates_ref[:, 512:1024].astype(F32)).astype(BF16)
    y = jnp.dot(jnp.concatenate([ma, mb], axis=1), w_ref[...], preferred_element_type=F32)
    _finish(y, x_ref, gate_ref, g_ref, o_ref)


def _outproj_even(oa, obs, lses, gates, ex, w_out, x, gate, post_g):
    bsz, s, d = x.shape
    tm = TM_PROJ
    return pl.pallas_call(
        _outproj_even_kernel,
        grid=(bsz, s // tm),
        in_specs=[_tok_spec(tm, 512), _tok_spec(tm, 512), _res_spec(tm, 4, 512), _res_spec(tm, 16, 512),
                  _tok_spec(tm, LANES), _res_spec(tm, 4, LANES), _res_spec(tm, 16, LANES),
                  _tok_spec(tm, 1024), _const_spec((LANES, 512)),
                  _const_spec((T_RES, T_RES)), _const_spec((T_RES, T_RES)), _const_spec((d, d)),
                  _tok_spec(tm, d), _bcast_spec(d), _const_spec((1, d))],
        out_specs=_tok_spec(tm, d),
        out_shape=jax.ShapeDtypeStruct((bsz, s, d), F32),
        compiler_params=_params("parallel", "parallel"),
        name="outproj_even",
    )(oa, *obs, *lses, gates, ex,
      jnp.asarray(_residue_perm(T_RES, 4).T, BF16), jnp.asarray(_residue_perm(T_RES, 16).T, BF16),
      w_out, x, gate, post_g)


def _outproj_odd_kernel(oc_ref, f_ref, gates_ref, lin_ref, w_ref, x_ref, gate_ref, g_ref, o_ref):
    od = jnp.dot(f_ref[...], lin_ref[...], preferred_element_type=F32)
    mc = (oc_ref[...].astype(F32) * gates_ref[:, 0:768].astype(F32)).astype(BF16)
    md = (od * gates_ref[:, 768:1024].astype(F32)).astype(BF16)
    y = jnp.dot(jnp.concatenate([mc, md], axis=1), w_ref[...], preferred_element_type=F32)
    _finish(y, x_ref, gate_ref, g_ref, o_ref)


def _outproj_odd(oc, f, gates, lin, w_out, x, gate, post_g):
    bsz, s, d = x.shape
    tm = TM_PROJ
    return pl.pallas_call(
        _outproj_odd_kernel,
        grid=(bsz, s // tm),
        in_specs=[_tok_spec(tm, 768), _tok_spec(tm, D_WIDTH), _tok_spec(tm, 1024),
                  _const_spec((D_WIDTH, D_WIDTH)), _const_spec((d, d)),
                  _tok_spec(tm, d), _bcast_spec(d), _const_spec((1, d))],
        out_specs=_tok_spec(tm, d),
        out_shape=jax.ShapeDtypeStruct((bsz, s, d), F32),
        compiler_params=_params("parallel", "parallel"),
        name="outproj_odd",
    )(oc, f, gates, lin, w_out, x, gate, post_g)


def _rope_tables(s):
    t = jnp.arange(s)

    def tab(pos, dim, theta):
        inv = theta ** (-jnp.arange(0, dim, 2, dtype=F32) / dim)
        ang = pos[:, None] * inv[None, :]
        return jnp.cos(ang), jnp.sin(ang)

    cr, sr = tab((t // GRID_W).astype(F32), HEAD_DIM // 2, A_ROPE_THETA)
    cc, sc = tab((t % GRID_W).astype(F32), HEAD_DIM // 2, A_ROPE_THETA)
    cb, sb = tab(t.astype(F32), B_ROPE_DIMS, B_ROPE_THETA)
    rest = HEAD_DIM - B_ROPE_DIMS
    ca = jnp.concatenate([cr, cr, cc, cc] * 2, axis=-1)
    sa = jnp.concatenate([-sr, sr, -sc, sc] * 2, axis=-1)
    cb = jnp.concatenate([cb, cb, jnp.ones((s, rest), F32)] * 2, axis=-1)
    sb = jnp.concatenate([-sb, sb, jnp.zeros((s, rest), F32)] * 2, axis=-1)
    return ca, sa, cb, sb


def _even_layer(x, mod, pre_g, post_g, w_in, w_out, qn, kn, tabs, ex):
    bsz, s, _ = x.shape
    shift, scale, gate = mod
    qa, ka, va, gates, qkv_b = _inproj_even(x, scale, shift, pre_g, w_in, qn, kn, tabs)
    oa = _attn_a(qa, ka, va)
    obs, lses = [], []
    for n, dil in enumerate(B_DILATIONS):
        q, k, v = (t[n] for t in qkv_b)
        if dil == 1:
            lq = min(LQ_B, s)
            q, k, v = (t.reshape(bsz, s // lq, 1, lq, t.shape[-1]) for t in (q, k, v))
        o, lse = _attn_b(q, k, v, dil)
        if dil == 1:
            o, lse = o.reshape(bsz, s, o.shape[-1]), lse.reshape(bsz, s, LANES)
        obs.append(o)
        lses.append(lse)
    return _outproj_even(oa, obs, lses, gates, ex, w_out, x, gate, post_g)


def _odd_layer(x, mod, pre_g, post_g, w_in, w_out, bias, lin, wc, dft):
    shift, scale, gate = mod
    qc, kc, vc, gates, vd = _inproj_odd(x, scale, shift, pre_g, w_in, wc)
    oc = _attn_c(qc, kc, vc, bias)
    f = _fourier(vd, dft)
    return _outproj_odd(oc, f, gates, lin, w_out, x, gate, post_g)


def _trunk(x, mods, pre_g, post_g, w_in_ab, w_out_ab, qn_a, kn_a, w_in_cd, w_out_cd, biases, lin_d):
    bsz, s, d = x.shape
    assert s % TM_PROJ == 0 and s % (16 * SUB_B) == 0
    tabs = _rope_tables(s)
    dft = _dft_consts(s)
    wc = _chan_dft_const()
    lane = np.arange(LANES)
    ex = jnp.asarray(lane[:, None] == np.arange(512)[None, :] // HEAD_DIM, BF16)
    for i in range(DEPTH):
        j = i // 2
        mod = tuple(mods[i][:, None, k * d:(k + 1) * d] for k in range(3))
        pg, qg = pre_g[i][None, :], post_g[i][None, :]
        if i % 2 == 0:
            qn = jnp.tile(qn_a[j], 2)[None, :]
            kn = jnp.tile(kn_a[j], 2)[None, :]
            x = _even_layer(x, mod, pg, qg, w_in_ab[j], w_out_ab[j], qn, kn, tabs, ex)
        else:
            x = _odd_layer(x, mod, pg, qg, w_in_cd[j], w_out_cd[j], biases[j], lin_d[j], wc, dft)
    return x


def kernel(x_prompt, x_sample, c_prompt, c_sample, pre_g, post_g, ada_w, ada_b,
           w_in_ab, w_out_ab, qn_a, kn_a, w_in_cd, w_out_cd, rpb_c, lin_d):
    nb = x_prompt.shape[0]
    mods = _adaln(jnp.concatenate([c_prompt, c_sample], axis=0), ada_w, ada_b)
    biases = [_bias_c(rpb_c[j]) for j in range(rpb_c.shape[0])]
    args = (pre_g, post_g, w_in_ab.astype(BF16), w_out_ab.astype(BF16), qn_a, kn_a,
            w_in_cd.astype(BF16), w_out_cd.astype(BF16), biases, lin_d.astype(BF16))
    y_prompt = _trunk(x_prompt, mods[:, :nb], *args)
    y_sample = _trunk(x_sample, mods[:, nb:], *args)
    return (y_prompt, y_sample)
```
